```python
import jax, jax.numpy as jnp
from jax import lax
import numpy as np

D_MODEL = 1024
BATCH = 8
SEQ = 4096
DEPTH = 4

GRID_W = 64
CTX_LEN = 256
N_MIXERS = 3
Q_BLOCK = 128
ROPE_THETA = 10000.0
NORM_EPS = 1e-6
DIFF_HEADS = 8
DIFF_HEAD_DIM = 64
GQA_HEADS = 8
GQA_KV_HEADS = 2
GQA_HEAD_DIM = 128
FNET_GROUPS = 4
FNET_GROUP_DIM = D_MODEL // FNET_GROUPS
D_FF = -(-8 * D_MODEL // (3 * 256)) * 256
N_A = len(range(0, DEPTH, N_MIXERS))
N_B = len(range(1, DEPTH, N_MIXERS))
N_C = len(range(2, DEPTH, N_MIXERS))

kernel_name = "hybrid_diffattn_gqa_fnet_prefix_dit"


def rms_norm(x, g):
    xf = x.astype(jnp.float32)
    y = xf * lax.rsqrt(jnp.mean(xf * xf, axis=-1, keepdims=True) + NORM_EPS)
    return (y * g.astype(jnp.float32)).astype(x.dtype)


def modulate(h, shift, scale):
    return h * (1 + scale) + shift


def lambda_init_fn(layer_idx):
    return 0.8 - 0.6 * float(np.exp(-0.3 * layer_idx))


def axial_rope_tables(rows, cols, head_dim):
    axis_dim = head_dim // 2
    inv_freq = ROPE_THETA ** (-jnp.arange(0, axis_dim, 2, dtype=jnp.float32) / axis_dim)
    ang = jnp.concatenate([rows[:, None].astype(jnp.float32) * inv_freq,
                           cols[:, None].astype(jnp.float32) * inv_freq], axis=-1)
    return jnp.cos(ang), jnp.sin(ang)


def apply_rope(x, cos, sin):
    shape = (cos.shape[0],) + (1,) * (x.ndim - 3) + (cos.shape[-1],)
    c_, s_ = cos.reshape(shape), sin.reshape(shape)
    xf = x.astype(jnp.float32)
    x1, x2 = xf[..., 0::2], xf[..., 1::2]
    out = jnp.stack([x1 * c_ - x2 * s_, x1 * s_ + x2 * c_], axis=-1).reshape(x.shape)
    return out.astype(x.dtype)


def sweep_blocks(fn, q):
    B, S = q.shape[:2]
    nb = S // Q_BLOCK
    qb = q.reshape((B, nb, Q_BLOCK) + q.shape[2:]).swapaxes(0, 1)
    out = lax.map(fn, qb)
    return out.swapaxes(0, 1).reshape((B, S) + out.shape[3:])


def swiglu(h, w_gu, w_down):
    g, u = jnp.split(h @ w_gu, 2, axis=-1)
    return (jax.nn.silu(g) * u) @ w_down


def diff_core(q, k, v, lam):
    s = jnp.einsum('bqhmd,bkhmd->bhmqk', q, k).astype(jnp.float32) * (DIFF_HEAD_DIM ** -0.5)
    p = jax.nn.softmax(s, axis=-1)
    a = p[:, :, 0] - lam * p[:, :, 1]
    return jnp.einsum('bhqk,bkhe->bqhe', a.astype(v.dtype), v)


def diff_attention(h, hc, w_qkv, lam_vecs, subln_g, w_o, lambda_init, rope, need_ctx):
    B, L, _ = h.shape
    C = hc.shape[1]
    dq = DIFF_HEADS * 2 * DIFF_HEAD_DIM
    cos, sin = rope

    def split_q(t, n):
        return t.reshape(B, n, DIFF_HEADS, 2, DIFF_HEAD_DIM)

    q, k, v = jnp.split(h @ w_qkv, 3, axis=-1)
    q = apply_rope(split_q(q, L), cos, sin)
    k = apply_rope(split_q(k, L), cos, sin)
    v = v.reshape(B, L, DIFF_HEADS, 2 * DIFF_HEAD_DIM)
    kc, vc = jnp.split(hc @ w_qkv[:, dq:], 2, axis=-1)
    kc = split_q(kc, C)
    vc = vc.reshape(B, C, DIFF_HEADS, 2 * DIFF_HEAD_DIM)

    lv = lam_vecs.astype(jnp.float32)
    lam = jnp.exp(jnp.sum(lv[0] * lv[1])) - jnp.exp(jnp.sum(lv[2] * lv[3])) + lambda_init

    def head_out(o, n):
        o = rms_norm(o, subln_g) * (1.0 - lambda_init)
        return o.reshape(B, n, DIFF_HEADS * 2 * DIFF_HEAD_DIM) @ w_o

    k_all = jnp.concatenate([kc, k], axis=1)
    v_all = jnp.concatenate([vc, v], axis=1)
    o = sweep_blocks(lambda qb: diff_core(qb, k_all, v_all, lam), q)
    y = head_out(o, L)
    yc = None
    if need_ctx:
        qc = split_q(hc @ w_qkv[:, :dq], C)
        yc = head_out(diff_core(qc, kc, vc, lam), C)
    return y, yc


def gqa_core(q, k, v):
    s = jnp.einsum('bqhgd,bkhd->bhgqk', q, k).astype(jnp.float32) * (GQA_HEAD_DIM ** -0.5)
    p = jax.nn.softmax(s, axis=-1)
    return jnp.einsum('bhgqk,bkhd->bqhgd', p.astype(v.dtype), v)


def gqa_attention(h, hc, w_qkv, q_norm, k_norm, w_o, rope, need_ctx):
    B, L, _ = h.shape
    C = hc.shape[1]
    G = GQA_HEADS // GQA_KV_HEADS
    dq = GQA_HEADS * GQA_HEAD_DIM
    dkv = GQA_KV_HEADS * GQA_HEAD_DIM
    cos, sin = rope

    def heads(t, n):
        q = rms_norm(t[..., :dq].reshape(B, n, GQA_KV_HEADS, G, GQA_HEAD_DIM), q_norm)
        k = rms_norm(t[..., dq:dq + dkv].reshape(B, n, GQA_KV_HEADS, GQA_HEAD_DIM), k_norm)
        v = t[..., dq + dkv:].reshape(B, n, GQA_KV_HEADS, GQA_HEAD_DIM)
        return q, k, v

    q, k, v = heads(h @ w_qkv, L)
    q = apply_rope(q, cos, sin)
    k = apply_rope(k, cos, sin)
    qc, kc, vc = heads(hc @ w_qkv, C)

    k_all = jnp.concatenate([kc, k], axis=1)
    v_all = jnp.concatenate([vc, v], axis=1)
    o = sweep_blocks(lambda qb: gqa_core(qb, k_all, v_all), q)
    y = o.reshape(B, L, dq) @ w_o
    yc = gqa_core(qc, kc, vc).reshape(B, C, dq) @ w_o if need_ctx else None
    return y, yc


def fourier_mix(h, w_o, b_o):
    B, L, D = h.shape
    hf = h.astype(jnp.float32).reshape(B, L, FNET_GROUPS, FNET_GROUP_DIM)
    f = jnp.fft.fft2(hf, axes=(1, 3), norm='ortho').real.astype(h.dtype).reshape(B, L, D)
    return f @ w_o + b_o


def setup_inputs(seed: int = 0) -> dict:
    key = jax.random.key(seed)
    ks = jax.random.split(key, 24)
    f32 = jnp.float32

    def nrm(k, shape, fan_in, s=1.0):
        return jax.random.normal(k, shape, f32) * (s * fan_in ** -0.5)

    def gain(k, shape):
        return 1.0 + 0.05 * jax.random.normal(k, shape, f32)

    D = D_MODEL
    return {
        "x": jax.random.normal(ks[0], (BATCH, SEQ, D), f32),
        "c": jax.random.normal(ks[1], (BATCH, D), f32),
        "ctx": jax.random.normal(ks[2], (BATCH, CTX_LEN, D), f32),
        "c_ctx": jax.random.normal(ks[3], (D,), f32),
        "mod_w": nrm(ks[4], (DEPTH, D, 6 * D), D, 0.5),
        "mod_b": 0.01 * jax.random.normal(ks[5], (DEPTH, 6 * D), f32),
        "ln_mix": gain(ks[6], (DEPTH, D)),
        "ln_ffn": gain(ks[7], (DEPTH, D)),
        "ffn_w_gu": nrm(ks[8], (DEPTH, D, 2 * D_FF), D),
        "ffn_w_down": nrm(ks[9], (DEPTH, D_FF, D), D_FF),
        "a_w_qkv": nrm(ks[10], (N_A, D, 3 * DIFF_HEADS * 2 * DIFF_HEAD_DIM), D),
        "a_lam": 0.1 * jax.random.normal(ks[11], (N_A, 4, DIFF_HEAD_DIM), f32),
        "a_subln": gain(ks[12], (N_A, 2 * DIFF_HEAD_DIM)),
        "a_w_o": nrm(ks[13], (N_A, DIFF_HEADS * 2 * DIFF_HEAD_DIM, D), D),
        "b_w_qkv": nrm(ks[14], (N_B, D, (GQA_HEADS + 2 * GQA_KV_HEADS) * GQA_HEAD_DIM), D),
        "b_q_norm": gain(ks[15], (N_B, GQA_HEAD_DIM)),
        "b_k_norm": gain(ks[16], (N_B, GQA_HEAD_DIM)),
        "b_w_o": nrm(ks[17], (N_B, GQA_HEADS * GQA_HEAD_DIM, D), GQA_HEADS * GQA_HEAD_DIM),
        "c_w_o": nrm(ks[18], (N_C, D, D), D),
        "c_b_o": 0.01 * jax.random.normal(ks[19], (N_C, D), f32),
        "final_norm": gain(ks[20], (D,)),
    }


def reference(x, c, ctx, c_ctx, mod_w, mod_b, ln_mix, ln_ffn, ffn_w_gu, ffn_w_down,
              a_w_qkv, a_lam, a_subln, a_w_o, b_w_qkv, b_q_norm, b_k_norm, b_w_o,
              c_w_o, c_b_o, final_norm):
    B, S, D = x.shape
    ROWS = S // GRID_W
    rows = jnp.repeat(jnp.arange(ROWS, dtype=jnp.int32), GRID_W)
    cols = jnp.tile(jnp.arange(GRID_W, dtype=jnp.int32), ROWS)
    rope_a = axial_rope_tables(rows, cols, DIFF_HEAD_DIM)
    rope_b = axial_rope_tables(rows, cols, GQA_HEAD_DIM)

    xc = ctx
    for i in range(DEPTH):
        last = i == DEPTH - 1
        kind, j = i % N_MIXERS, i // N_MIXERS
        sh1, sc1, g1, sh2, sc2, g2 = [m[:, None, :] for m in
                                      jnp.split(jax.nn.silu(c) @ mod_w[i] + mod_b[i], 6, axis=-1)]
        csh1, csc1, cg1, csh2, csc2, cg2 = jnp.split(jax.nn.silu(c_ctx) @ mod_w[i] + mod_b[i], 6, axis=-1)

        h = modulate(rms_norm(x, ln_mix[i]), sh1, sc1)
        if kind == 0:
            hc = modulate(rms_norm(xc, ln_mix[i]), csh1, csc1)
            y, yc = diff_attention(h, hc, a_w_qkv[j], a_lam[j], a_subln[j], a_w_o[j],
                                   lambda_init_fn(i), rope_a, not last)
        elif kind == 1:
            hc = modulate(rms_norm(xc, ln_mix[i]), csh1, csc1)
            y, yc = gqa_attention(h, hc, b_w_qkv[j], b_q_norm[j], b_k_norm[j], b_w_o[j],
                                  rope_b, not last)
        else:
            y = fourier_mix(h, c_w_o[j], c_b_o[j])
            yc = None
            if not last:
                hc = modulate(rms_norm(xc, ln_mix[i]), csh1, csc1)
                yc = fourier_mix(hc, c_w_o[j], c_b_o[j])

        x = x + g1 * y
        x = x + g2 * swiglu(modulate(rms_norm(x, ln_ffn[i]), sh2, sc2), ffn_w_gu[i], ffn_w_down[i])
        if not last:
            xc = xc + cg1 * yc
            xc = xc + cg2 * swiglu(modulate(rms_norm(xc, ln_ffn[i]), csh2, csc2),
                                   ffn_w_gu[i], ffn_w_down[i])

    return rms_norm(x, final_norm)
```

```python
import functools
import math

import numpy as np
import jax
import jax.numpy as jnp
from jax import lax
from jax.experimental import pallas as pl
from jax.experimental.pallas import tpu as pltpu

GRID_W = 64
ROPE_THETA = 10000.0
NORM_EPS = 1e-6
DIFF_HEADS = 8
DIFF_HEAD_DIM = 64
GQA_HEADS = 8
GQA_KV_HEADS = 2
GQA_HEAD_DIM = 128
FNET_GROUPS = 4
N_MIXERS = 3

LANES = 128
SUBLANES = 8
VMEM_LIMIT = 56 * 1024 * 1024

ROW_TILE = 256
HEAD_W = 128
KEY_CHUNK = 256
MOD_ROWS = 16
LOG2E = math.log2(math.e)

BF16 = jnp.bfloat16
F32 = jnp.float32

SH1, SC1, G1, SH2, SC2, G2 = range(6)


def _lambda_init(layer_idx):
    return 0.8 - 0.6 * float(np.exp(-0.3 * layer_idx))


def _params(n_grid):
    return pltpu.CompilerParams(
        dimension_semantics=("arbitrary",) * n_grid, vmem_limit_bytes=VMEM_LIMIT)


def _resident(shape):
    nd = len(shape)
    return pl.BlockSpec(shape, lambda *_: (0,) * nd, pipeline_mode=pl.Buffered(1))


def _dot(a, b):
    return jnp.dot(a, b, preferred_element_type=F32)


def _dot_nt(a, b):
    return lax.dot_general(a, b, (((1,), (1,)), ((), ())), preferred_element_type=F32)


def _rms(x, g):
    return x * lax.rsqrt(jnp.mean(x * x, axis=-1, keepdims=True) + NORM_EPS) * g


def _silu(x):
    return x / (1.0 + jnp.exp(-x))


def _mod_kernel(cs_ref, w_ref, b_ref, o_ref):
    s = _silu(cs_ref[...]).astype(BF16)
    o_ref[0] = _dot(s, w_ref[0].astype(BF16)) + b_ref[0]


def _modulation(cs, mod_w, mod_b, tn=1536):
    depth, d, n = mod_w.shape
    return pl.pallas_call(
        _mod_kernel,
        grid=(depth, n // tn),
        in_specs=[
            pl.BlockSpec((MOD_ROWS, d), lambda i, j: (0, 0)),
            pl.BlockSpec((1, d, tn), lambda i, j: (i, 0, j)),
            pl.BlockSpec((1, 1, tn), lambda i, j: (i, 0, j)),
        ],
        out_specs=pl.BlockSpec((1, MOD_ROWS, tn), lambda i, j: (i, 0, j)),
        out_shape=jax.ShapeDtypeStruct((depth, MOD_ROWS, n), F32),
        compiler_params=_params(2),
        name="modulation",
    )(cs, mod_w, mod_b.reshape(depth, 1, n))


def _mod_spec(ctx_tile, batch):
    def index(b, j):
        return (jnp.where(j == ctx_tile, batch, b), 0, 0)
    return index


def _norm_mod(x, ln, mod, shift, scale):
    y = _rms(x, ln)
    return (y * (1.0 + mod[scale:scale + 1]) + mod[shift:shift + 1]).astype(BF16)


def _qkv_kernel(x_ref, mod_ref, ln_ref, w_ref, wvt_ref, cos_ref, sin_ref, qn_ref, kn_ref,
                q_ref, k_ref, vt_ref, *, n_q, n_k, head_norm, q_scale):
    h = _norm_mod(x_ref[0], ln_ref[...], mod_ref[0], SH1, SC1)
    cos, sin = cos_ref[...], sin_ref[...]
    for head in range((n_q + n_k) // HEAD_W):
        is_q = head < n_q // HEAD_W
        if head % 2 == 0:
            pair = _dot(h, w_ref[:, head * HEAD_W:(head + 2) * HEAD_W])
        t = pair[:, (head % 2) * HEAD_W:(head % 2 + 1) * HEAD_W]
        if head_norm:
            t = _rms(t, qn_ref[...] if is_q else kn_ref[...])
        t = t * cos + pltpu.roll(t, HEAD_W // 2, 1) * sin
        if is_q:
            q_ref[0, :, head * HEAD_W:(head + 1) * HEAD_W] = (t * q_scale).astype(BF16)
        else:
            c0 = head * HEAD_W - n_q
            k_ref[0, :, c0:c0 + HEAD_W] = t.astype(BF16)
    vt_ref[0] = _dot_nt(wvt_ref[...], h).astype(BF16)


def _qkv_project(x, mod, ln, w_qk, w_vt, cosf, sinf, qn, kn, *, n_q, n_k, head_norm, q_scale):
    batch, t, d = x.shape
    n_v = w_vt.shape[0]
    n_tiles = t // ROW_TILE
    row = lambda b, j: (b, j, 0)
    kern = functools.partial(_qkv_kernel, n_q=n_q, n_k=n_k, head_norm=head_norm, q_scale=q_scale)
    return pl.pallas_call(
        kern,
        grid=(batch, n_tiles),
        in_specs=[
            pl.BlockSpec((1, ROW_TILE, d), row),
            pl.BlockSpec((1, 6, d), _mod_spec(n_tiles - 1, batch)),
            _resident((1, d)),
            _resident((d, n_q + n_k)),
            _resident((n_v, d)),
            pl.BlockSpec((ROW_TILE, HEAD_W), lambda b, j: (j, 0)),
            pl.BlockSpec((ROW_TILE, HEAD_W), lambda b, j: (j, 0)),
            _resident((1, HEAD_W)),
            _resident((1, HEAD_W)),
        ],
        out_specs=[
            pl.BlockSpec((1, ROW_TILE, n_q), row),
            pl.BlockSpec((1, ROW_TILE, n_k), row),
            pl.BlockSpec((1, n_v, ROW_TILE), lambda b, j: (b, 0, j)),
        ],
        out_shape=[
            jax.ShapeDtypeStruct((batch, t, n_q), BF16),
            jax.ShapeDtypeStruct((batch, t, n_k), BF16),
            jax.ShapeDtypeStruct((batch, n_v, t), BF16),
        ],
        compiler_params=_params(2),
        name="qkv_project",
    )(x, mod, ln, w_qk, w_vt, cosf, sinf, qn, kn)


def _lin_kernel(x_ref, mod_ref, ln_ref, w_ref, o_ref):
    h = _norm_mod(x_ref[0], ln_ref[...], mod_ref[0], SH1, SC1)
    o_ref[0] = _dot(h, w_ref[...]).astype(BF16)


def _norm_project(x, mod, ln, w):
    batch, t, d = x.shape
    n = w.shape[1]
    n_tiles = t // ROW_TILE
    row = lambda b, j: (b, j, 0)
    return pl.pallas_call(
        _lin_kernel,
        grid=(batch, n_tiles),
        in_specs=[
            pl.BlockSpec((1, ROW_TILE, d), row),
            pl.BlockSpec((1, 6, d), _mod_spec(n_tiles - 1, batch)),
            _resident((1, d)),
            _resident((d, n)),
        ],
        out_specs=pl.BlockSpec((1, ROW_TILE, n), row),
        out_shape=jax.ShapeDtypeStruct((batch, t, n), BF16),
        compiler_params=_params(2),
        name="norm_project",
    )(x, mod, ln, w)


def _softmax_pv(q, k_ref, vt_ref, s_scr, e_scr, k_lo, n_keys):
    tq = q.shape[0]
    n_chunks = n_keys // KEY_CHUNK
    fold = (KEY_CHUNK // SUBLANES, SUBLANES, tq)
    m8 = jnp.full((SUBLANES, tq), -jnp.inf, F32)
    for c in range(n_chunks):
        s = _dot_nt(k_ref[0, k_lo + c * KEY_CHUNK:k_lo + (c + 1) * KEY_CHUNK, :], q)
        s_scr[c * KEY_CHUNK:(c + 1) * KEY_CHUNK, :] = s
        m8 = jnp.maximum(m8, jnp.max(s.reshape(fold), axis=0))
    mx = jnp.max(m8, axis=0, keepdims=True)
    l8 = jnp.zeros((SUBLANES, tq), F32)
    for c in range(n_chunks):
        e = jnp.exp2(s_scr[c * KEY_CHUNK:(c + 1) * KEY_CHUNK, :] - mx)
        l8 = l8 + jnp.sum(e.reshape(fold), axis=0)
        e_scr[c * KEY_CHUNK:(c + 1) * KEY_CHUNK, :] = e.astype(BF16)
    denom = jnp.sum(l8, axis=0, keepdims=True)
    o_t = _dot(vt_ref[0, :, k_lo:k_lo + n_keys], e_scr[0:n_keys, :])
    return o_t * (1.0 / denom)


def _attn_kernel(lam_ref, subln_ref, q_ref, k_ref, vt_ref, o_ref, s_scr, e_scr,
                 *, diff, lambda_init, ctx_tile, ctx_lo, n_all, n_ctx):
    def run(k_lo, n_keys):
        q = q_ref[0]
        if diff:
            lane = lax.broadcasted_iota(jnp.int32, q.shape, 1)
            first = (lane % (HEAD_W // 2)) < (HEAD_W // 4)
            zero = jnp.zeros_like(q)
            o1 = _softmax_pv(jnp.where(first, q, zero), k_ref, vt_ref, s_scr, e_scr, k_lo, n_keys)
            o2 = _softmax_pv(jnp.where(first, zero, q), k_ref, vt_ref, s_scr, e_scr, k_lo, n_keys)
            lv = lam_ref[...]
            lam = (jnp.exp(jnp.sum(lv[0:1] * lv[1:2], axis=-1, keepdims=True))
                   - jnp.exp(jnp.sum(lv[2:3] * lv[3:4], axis=-1, keepdims=True)) + lambda_init)
            o = (o1 - lam * o2).T
            o = _rms(o, subln_ref[...]) * (1.0 - lambda_init)
        else:
            o = _softmax_pv(q, k_ref, vt_ref, s_scr, e_scr, k_lo, n_keys).T
        o_ref[0] = o.astype(BF16)

    if ctx_tile is None:
        run(0, n_all)
    else:
        i = pl.program_id(2)

        @pl.when(i != ctx_tile)
        def _():
            run(0, n_all)

        @pl.when(i == ctx_tile)
        def _():
            run(ctx_lo, n_ctx)


def _attention(q, k, vt, lam, subln, *, diff, lambda_init, n_heads, kv_group, n_q_tiles, ctx_tile, n_ctx):
    batch, t, _ = q.shape
    kern = functools.partial(
        _attn_kernel, diff=diff, lambda_init=lambda_init, ctx_tile=ctx_tile,
        ctx_lo=t - n_ctx, n_all=t, n_ctx=n_ctx)
    return pl.pallas_call(
        kern,
        grid=(batch, n_heads, n_q_tiles),
        in_specs=[
            _resident(lam.shape),
            _resident(subln.shape),
            pl.BlockSpec((1, ROW_TILE, HEAD_W), lambda b, h, i: (b, i, h)),
            pl.BlockSpec((1, t, HEAD_W), lambda b, h, i: (b, 0, h // kv_group)),
            pl.BlockSpec((1, HEAD_W, t), lambda b, h, i: (b, h // kv_group, 0)),
        ],
        out_specs=pl.BlockSpec((1, ROW_TILE, HEAD_W), lambda b, h, i: (b, i, h)),
        out_shape=jax.ShapeDtypeStruct((batch, n_q_tiles * ROW_TILE, n_heads * HEAD_W), BF16),
        scratch_shapes=[pltpu.VMEM((t, ROW_TILE), F32), pltpu.VMEM((t, ROW_TILE), BF16)],
        compiler_params=_params(3),
        name="diff_attention" if diff else "gqa_attention",
    )(lam, subln, q, k, vt)


def _dft_kernel(cl_ref, sl_ref, cc_ref, sc_ref, xl_c_ref, xl_s_ref, xc_c_ref, xc_s_ref, o_ref,
                *, ctx_tile):
    i = pl.program_id(1)

    @pl.when(i != ctx_tile)
    def _():
        o_ref[0] = (_dot(cl_ref[...], xl_c_ref[0]) + _dot(sl_ref[...], xl_s_ref[0])).astype(BF16)

    @pl.when(i == ctx_tile)
    def _():
        o_ref[0] = (_dot(cc_ref[...], xc_c_ref[0]) + _dot(sc_ref[...], xc_s_ref[0])).astype(BF16)


def _position_dft(xcs, cl, sl, cc, sc, n_lat, n_tiles):
    batch, t, d2 = xcs.shape
    d = d2 // 2
    n_ctx = t - n_lat
    ctx_tile = n_lat // ROW_TILE
    last_lat = ctx_tile - 1
    lat_rows = lambda b, i: (jnp.minimum(i, last_lat), 0)
    return pl.pallas_call(
        functools.partial(_dft_kernel, ctx_tile=ctx_tile),
        grid=(batch, n_tiles),
        in_specs=[
            pl.BlockSpec((ROW_TILE, n_lat), lat_rows),
            pl.BlockSpec((ROW_TILE, n_lat), lat_rows),
            _resident((n_ctx, n_ctx)),
            _resident((n_ctx, n_ctx)),
            pl.BlockSpec((1, n_lat, d), lambda b, i: (b, 0, 0)),
            pl.BlockSpec((1, n_lat, d), lambda b, i: (b, 0, 1)),
            pl.BlockSpec((1, n_ctx, d), lambda b, i: (b, ctx_tile, 0)),
            pl.BlockSpec((1, n_ctx, d), lambda b, i: (b, ctx_tile, 1)),
        ],
        out_specs=pl.BlockSpec((1, ROW_TILE, d), lambda b, i: (b, i, 0)),
        out_shape=jax.ShapeDtypeStruct((batch, n_tiles * ROW_TILE, d), BF16),
        compiler_params=_params(2),
        name="position_dft",
    )(cl, sl, cc, sc, xcs, xcs, xcs, xcs)


def _res_kernel(a_ref, w_ref, b_ref, x_ref, mod_ref, o_ref):
    y = _dot(a_ref[0], w_ref[...]) + b_ref[...]
    o_ref[0] = x_ref[0] + mod_ref[0][G1:G1 + 1] * y


def _residual_project(a, w, bias, x, mod, n_tiles, ctx_tile):
    batch, _, d = x.shape
    k = a.shape[2]
    row = lambda b, j: (b, j, 0)
    return pl.pallas_call(
        _res_kernel,
        grid=(batch, n_tiles),
        in_specs=[
            pl.BlockSpec((1, ROW_TILE, k), row),
            _resident((k, d)),
            _resident((1, d)),
            pl.BlockSpec((1, ROW_TILE, d), row),
            pl.BlockSpec((1, 6, d), _mod_spec(ctx_tile, batch)),
        ],
        out_specs=pl.BlockSpec((1, ROW_TILE, d), row),
        out_shape=jax.ShapeDtypeStruct((batch, n_tiles * ROW_TILE, d), F32),
        compiler_params=_params(2),
        name="residual_project",
    )(a, w, bias, x, mod)


def _ffn_kernel(x_ref, mod_ref, ln_ref, wgu_ref, wd_ref, fn_ref, o_ref, a_scr, *, d_ff, final_norm):
    x = x_ref[0]
    mod = mod_ref[0]
    h = _norm_mod(x, ln_ref[...], mod, SH2, SC2)
    for c in range(d_ff // KEY_CHUNK):
        lo = c * KEY_CHUNK
        g = _dot(h, wgu_ref[:, lo:lo + KEY_CHUNK])
        u = _dot(h, wgu_ref[:, d_ff + lo:d_ff + lo + KEY_CHUNK])
        a_scr[:, lo:lo + KEY_CHUNK] = (_silu(g) * u).astype(BF16)
    y = x + mod[G2:G2 + 1] * _dot(a_scr[...], wd_ref[...])
    if final_norm:
        y = _rms(y, fn_ref[...])
    o_ref[0] = y


def _ffn(x, mod, ln, w_gu, w_down, final_g, n_tiles, ctx_tile, final_norm):
    batch, _, d = x.shape
    d_ff = w_down.shape[0]
    row = lambda b, j: (b, j, 0)
    return pl.pallas_call(
        functools.partial(_ffn_kernel, d_ff=d_ff, final_norm=final_norm),
        grid=(batch, n_tiles),
        in_specs=[
            pl.BlockSpec((1, ROW_TILE, d), row),
            pl.BlockSpec((1, 6, d), _mod_spec(ctx_tile, batch)),
            _resident((1, d)),
            _resident((d, 2 * d_ff)),
            _resident((d_ff, d)),
            _resident((1, d)),
        ],
        out_specs=pl.BlockSpec((1, ROW_TILE, d), row),
        out_shape=jax.ShapeDtypeStruct((batch, n_tiles * ROW_TILE, d), F32),
        scratch_shapes=[pltpu.VMEM((ROW_TILE, d_ff), BF16)],
        compiler_params=_params(2),
        name="ffn",
    )(x, mod, ln, w_gu, w_down, final_g)


def _half_split_perm(n_comp, comp_dim):
    src = np.zeros(HEAD_W, np.int32)
    for m in range(n_comp):
        for i in range(comp_dim // 2):
            for p in range(2):
                src[p * (HEAD_W // 2) + m * (comp_dim // 2) + i] = m * comp_dim + 2 * i + p
    return src


def _rope_lane_tables(seq, n_ctx, head_dim):
    axis_dim = head_dim // 2
    rows = jnp.arange(seq, dtype=jnp.int32) // GRID_W
    cols = jnp.arange(seq, dtype=jnp.int32) % GRID_W
    inv_freq = ROPE_THETA ** (-jnp.arange(0, axis_dim, 2, dtype=F32) / axis_dim)
    ang = jnp.concatenate([rows[:, None].astype(F32) * inv_freq,
                           cols[:, None].astype(F32) * inv_freq], axis=-1)
    reps = (HEAD_W // 2) // (head_dim // 2)
    cos = jnp.tile(jnp.cos(ang), (1, 2 * reps))
    sin = jnp.tile(jnp.sin(ang), (1, reps))
    sin = jnp.concatenate([-sin, sin], axis=-1)
    cos = jnp.concatenate([cos, jnp.ones((n_ctx, HEAD_W), F32)], axis=0)
    sin = jnp.concatenate([sin, jnp.zeros((n_ctx, HEAD_W), F32)], axis=0)
    return cos, sin


def _dft_tables(n, scale):
    r = 1
    while r * r < n:
        r *= 2
    q = n // r
    k = jnp.arange(n, dtype=jnp.int32)[:, None]
    a_idx = (k * jnp.arange(q, dtype=jnp.int32)[None, :] * r) % n
    b_idx = (k * jnp.arange(r, dtype=jnp.int32)[None, :]) % n
    w = 2.0 * math.pi / n
    ca, sa = jnp.cos(a_idx.astype(F32) * w), jnp.sin(a_idx.astype(F32) * w)
    cb, sb = jnp.cos(b_idx.astype(F32) * w) * scale, jnp.sin(b_idx.astype(F32) * w) * scale
    cos = ca[:, :, None] * cb[:, None, :] - sa[:, :, None] * sb[:, None, :]
    sin = sa[:, :, None] * cb[:, None, :] + ca[:, :, None] * sb[:, None, :]
    return cos.reshape(n, n).astype(BF16), (-sin).reshape(n, n).astype(BF16)


def _channel_dft_weight(d, groups):
    gd = d // groups
    idx = (np.arange(gd)[:, None] * np.arange(gd)[None, :]) % gd
    ang = 2.0 * np.pi * idx / gd
    eye = np.eye(groups)
    c = np.kron(eye, np.cos(ang)) / math.sqrt(gd)
    s = np.kron(eye, np.sin(ang)) / math.sqrt(gd)
    return jnp.asarray(np.concatenate([c, s], axis=1), dtype=BF16)


def kernel(x, c, ctx, c_ctx, mod_w, mod_b, ln_mix, ln_ffn, ffn_w_gu, ffn_w_down, a_w_qkv, a_lam, a_subln, a_w_o, b_w_qkv, b_q_norm, b_k_norm, b_w_o, c_w_o, c_b_o, final_norm):
    batch, seq, d = x.shape
    n_ctx = ctx.shape[1]
    depth = mod_w.shape[0]
    t = seq + n_ctx
    assert seq % ROW_TILE == 0 and n_ctx == ROW_TILE and batch < MOD_ROWS
    n_tiles_all = t // ROW_TILE
    n_tiles_lat = seq // ROW_TILE
    ctx_tile = n_tiles_all - 1

    cs = jnp.concatenate([c, c_ctx[None, :], jnp.zeros((MOD_ROWS - batch - 1, d), F32)], axis=0)
    mod_all = _modulation(cs, mod_w, mod_b).reshape(depth, MOD_ROWS, 6, d)

    xs = jnp.concatenate([x, ctx], axis=1)

    perm_a = _half_split_perm(2, DIFF_HEAD_DIM)
    perm_b = _half_split_perm(1, GQA_HEAD_DIM)
    rope_a = _rope_lane_tables(seq, n_ctx, DIFF_HEAD_DIM)
    rope_b = _rope_lane_tables(seq, n_ctx, GQA_HEAD_DIM)
    ones_head = jnp.ones((1, HEAD_W), F32)
    no_bias = jnp.zeros((1, d), F32)

    for i in range(depth):
        last = i == depth - 1
        kind, j = i % N_MIXERS, i // N_MIXERS
        mod = mod_all[i]
        ln1 = ln_mix[i][None, :]
        n_out_tiles = n_tiles_lat if last else n_tiles_all
        out_ctx_tile = None if last else ctx_tile

        if kind == 0:
            n_q = DIFF_HEADS * HEAD_W
            cols = (np.arange(2 * DIFF_HEADS)[:, None] * HEAD_W + perm_a[None, :]).reshape(-1)
            w = a_w_qkv[j]
            w_qk = w[:, cols].astype(BF16)
            w_vt = w[:, 2 * n_q:].T.astype(BF16)
            q, k, vt = _qkv_project(
                xs, mod, ln1, w_qk, w_vt, rope_a[0], rope_a[1], ones_head, ones_head,
                n_q=n_q, n_k=n_q, head_norm=False, q_scale=DIFF_HEAD_DIM ** -0.5 * LOG2E)
            o = _attention(
                q, k, vt, a_lam[j], a_subln[j][None, :], diff=True, lambda_init=_lambda_init(i),
                n_heads=DIFF_HEADS, kv_group=1, n_q_tiles=n_out_tiles, ctx_tile=out_ctx_tile, n_ctx=n_ctx)
            xs = _residual_project(o, a_w_o[j].astype(BF16), no_bias, xs, mod, n_out_tiles, ctx_tile)
        elif kind == 1:
            n_q = GQA_HEADS * HEAD_W
            n_k = GQA_KV_HEADS * HEAD_W
            cols = (np.arange(GQA_HEADS + GQA_KV_HEADS)[:, None] * HEAD_W + perm_b[None, :]).reshape(-1)
            w = b_w_qkv[j]
            w_qk = w[:, cols].astype(BF16)
            w_vt = w[:, n_q + n_k:].T.astype(BF16)
            q, k, vt = _qkv_project(
                xs, mod, ln1, w_qk, w_vt, rope_b[0], rope_b[1],
                b_q_norm[j][perm_b][None, :], b_k_norm[j][perm_b][None, :],
                n_q=n_q, n_k=n_k, head_norm=True, q_scale=GQA_HEAD_DIM ** -0.5 * LOG2E)
            o = _attention(
                q, k, vt, jnp.zeros((4, DIFF_HEAD_DIM), F32), ones_head, diff=False, lambda_init=0.0,
                n_heads=GQA_HEADS, kv_group=GQA_HEADS // GQA_KV_HEADS, n_q_tiles=n_out_tiles,
                ctx_tile=out_ctx_tile, n_ctx=n_ctx)
            xs = _residual_project(o, b_w_o[j].astype(BF16), no_bias, xs, mod, n_out_tiles, ctx_tile)
        else:
            xcs = _norm_project(xs, mod, ln1, _channel_dft_weight(d, FNET_GROUPS))
            cl, sl = _dft_tables(seq, seq ** -0.5)
            cc, sc = _dft_tables(n_ctx, n_ctx ** -0.5)
            f = _position_dft(xcs, cl, sl, cc, sc, seq, n_out_tiles)
            xs = _residual_project(f, c_w_o[j].astype(BF16), c_b_o[j][None, :], xs, mod, n_out_tiles, ctx_tile)

        xs = _ffn(xs, mod, ln_ffn[i][None, :], ffn_w_gu[i].astype(BF16), ffn_w_down[i].astype(BF16),
                  final_norm[None, :], n_out_tiles, ctx_tile, final_norm=last)
    return xs
```

```python
import functools
import math

import numpy as np
import jax
import jax.numpy as jnp
from jax import lax
from jax.experimental import pallas as pl
from jax.experimental.pallas import tpu as pltpu

GRID_W = 64
ROPE_THETA = 10000.0
NORM_EPS = 1e-6
DIFF_HEADS = 8
DIFF_HEAD_DIM = 64
GQA_HEADS = 8
GQA_KV_HEADS = 2
GQA_HEAD_DIM = 128
FNET_GROUPS = 4
N_MIXERS = 3

LANES = 128
SUBLANES = 8
VMEM_LIMIT = 56 * 1024 * 1024

ROW_TILE = 256
HEAD_W = 128
KEY_CHUNK = 256
MOD_ROWS = 16
LOG2E = math.log2(math.e)

BF16 = jnp.bfloat16
F32 = jnp.float32

SH1, SC1, G1, SH2, SC2, G2 = range(6)


def _lambda_init(layer_idx):
    return 0.8 - 0.6 * float(np.exp(-0.3 * layer_idx))


def _params(n_grid):
    return pltpu.CompilerParams(
        dimension_semantics=("arbitrary",) * n_grid, vmem_limit_bytes=VMEM_LIMIT)


def _resident(shape):
    nd = len(shape)
    return pl.BlockSpec(shape, lambda *_: (0,) * nd, pipeline_mode=pl.Buffered(1))


def _dot(a, b):
    return jnp.dot(a, b, preferred_element_type=F32)


def _dot_nt(a, b):
    return lax.dot_general(a, b, (((1,), (1,)), ((), ())), preferred_element_type=F32)


def _rms(x, g):
    return x * lax.rsqrt(jnp.mean(x * x, axis=-1, keepdims=True) + NORM_EPS) * g


def _silu(x):
    return x / (1.0 + jnp.exp(-x))


def _mod_kernel(cs_ref, w_ref, b_ref, o_ref):
    s = _silu(cs_ref[...]).astype(BF16)
    o_ref[0] = _dot(s, w_ref[0].astype(BF16)) + b_ref[0]


def _modulation(cs, mod_w, mod_b, tn=1536):
    depth, d, n = mod_w.shape
    return pl.pallas_call(
        _mod_kernel,
        grid=(depth, n // tn),
        in_specs=[
            pl.BlockSpec((MOD_ROWS, d), lambda i, j: (0, 0)),
            pl.BlockSpec((1, d, tn), lambda i, j: (i, 0, j)),
            pl.BlockSpec((1, 1, tn), lambda i, j: (i, 0, j)),
        ],
        out_specs=pl.BlockSpec((1, MOD_ROWS, tn), lambda i, j: (i, 0, j)),
        out_shape=jax.ShapeDtypeStruct((depth, MOD_ROWS, n), F32),
        compiler_params=_params(2),
        name="modulation",
    )(cs, mod_w, mod_b.reshape(depth, 1, n))


def _mod_spec(ctx_tile, batch):
    def index(b, j):
        return (jnp.where(j == ctx_tile, batch, b), 0, 0)
    return index


def _norm_mod(x, ln, mod, shift, scale):
    y = _rms(x, ln)
    return (y * (1.0 + mod[scale:scale + 1]) + mod[shift:shift + 1]).astype(BF16)


def _qkv_kernel(x_ref, mod_ref, ln_ref, w_ref, wvt_ref, cos_ref, sin_ref, qn_ref, kn_ref,
                q_ref, k_ref, vt_ref, *, n_q, n_k, head_norm, q_scale):
    h = _norm_mod(x_ref[0], ln_ref[...], mod_ref[0], SH1, SC1)
    cos, sin = cos_ref[...], sin_ref[...]
    for head in range((n_q + n_k) // HEAD_W):
        is_q = head < n_q // HEAD_W
        if head % 2 == 0:
            pair = _dot(h, w_ref[:, head * HEAD_W:(head + 2) * HEAD_W])
        t = pair[:, (head % 2) * HEAD_W:(head % 2 + 1) * HEAD_W]
        if head_norm:
            t = _rms(t, qn_ref[...] if is_q else kn_ref[...])
        t = t * cos + pltpu.roll(t, HEAD_W // 2, 1) * sin
        if is_q:
            q_ref[0, :, head * HEAD_W:(head + 1) * HEAD_W] = (t * q_scale).astype(BF16)
        else:
            c0 = head * HEAD_W - n_q
            k_ref[0, :, c0:c0 + HEAD_W] = t.astype(BF16)
    vt_ref[0] = _dot_nt(wvt_ref[...], h).astype(BF16)


def _qkv_project(x, mod, ln, w_qk, w_vt, cosf, sinf, qn, kn, *, n_q, n_k, head_norm, q_scale):
    batch, t, d = x.shape
    n_v = w_vt.shape[0]
    n_tiles = t // ROW_TILE
    row = lambda b, j: (b, j, 0)
    kern = functools.partial(_qkv_kernel, n_q=n_q, n_k=n_k, head_norm=head_norm, q_scale=q_scale)
    return pl.pallas_call(
        kern,
        grid=(batch, n_tiles),
        in_specs=[
            pl.BlockSpec((1, ROW_TILE, d), row),
            pl.BlockSpec((1, 6, d), _mod_spec(n_tiles - 1, batch)),
            _resident((1, d)),
            _resident((d, n_q + n_k)),
            _resident((n_v, d)),
            pl.BlockSpec((ROW_TILE, HEAD_W), lambda b, j: (j, 0)),
            pl.BlockSpec((ROW_TILE, HEAD_W), lambda b, j: (j, 0)),
            _resident((1, HEAD_W)),
            _resident((1, HEAD_W)),
        ],
        out_specs=[
            pl.BlockSpec((1, ROW_TILE, n_q), row),
            pl.BlockSpec((1, ROW_TILE, n_k), row),
            pl.BlockSpec((1, n_v, ROW_TILE), lambda b, j: (b, 0, j)),
        ],
        out_shape=[
            jax.ShapeDtypeStruct((batch, t, n_q), BF16),
            jax.ShapeDtypeStruct((batch, t, n_k), BF16),
            jax.ShapeDtypeStruct((batch, n_v, t), BF16),
        ],
        compiler_params=_params(2),
        name="qkv_project",
    )(x, mod, ln, w_qk, w_vt, cosf, sinf, qn, kn)


def _lin_kernel(x_ref, mod_ref, ln_ref, w_ref, o_ref):
    h = _norm_mod(x_ref[0], ln_ref[...], mod_ref[0], SH1, SC1)
    o_ref[0] = _dot(h, w_ref[...]).astype(BF16)


def _norm_project(x, mod, ln, w):
    batch, t, d = x.shape
    n = w.shape[1]
    n_tiles = t // ROW_TILE
    row = lambda b, j: (b, j, 0)
    return pl.pallas_call(
        _lin_kernel,
        grid=(batch, n_tiles),
        in_specs=[
            pl.BlockSpec((1, ROW_TILE, d), row),
            pl.BlockSpec((1, 6, d), _mod_spec(n_tiles - 1, batch)),
            _resident((1, d)),
            _resident((d, n)),
        ],
        out_specs=pl.BlockSpec((1, ROW_TILE, n), row),
        out_shape=jax.ShapeDtypeStruct((batch, t, n), BF16),
        compiler_params=_params(2),
        name="norm_project",
    )(x, mod, ln, w)


def _softmax_pv_pipeline(problems, k_ref, vt_ref, s_scr, k_lo, n_keys):
    tq = problems[0][0].shape[0]
    n_chunks = n_keys // KEY_CHUNK
    fold = (KEY_CHUNK // SUBLANES, SUBLANES, tq)

    def scores(p, c, m8):
        q, kvh = problems[p]
        rows = slice(k_lo + c * KEY_CHUNK, k_lo + (c + 1) * KEY_CHUNK)
        s = _dot_nt(k_ref[0, rows, kvh * HEAD_W:(kvh + 1) * HEAD_W], q)
        s_scr[p % 2, c * KEY_CHUNK:(c + 1) * KEY_CHUNK, :] = s
        return jnp.maximum(m8, jnp.max(s.reshape(fold), axis=0))

    def weigh(p, c, mx, l8, acc):
        kvh = problems[p][1]
        e = jnp.exp2(s_scr[p % 2, c * KEY_CHUNK:(c + 1) * KEY_CHUNK, :] - mx)
        l8 = l8 + jnp.sum(e.reshape(fold), axis=0)
        cols = slice(k_lo + c * KEY_CHUNK, k_lo + (c + 1) * KEY_CHUNK)
        acc = acc + _dot(vt_ref[0, kvh * HEAD_W:(kvh + 1) * HEAD_W, cols], e.astype(BF16))
        return l8, acc

    outs = []
    m8 = jnp.full((SUBLANES, tq), -jnp.inf, F32)
    for c in range(n_chunks):
        m8 = scores(0, c, m8)
    for p in range(1, len(problems) + 1):
        mx = jnp.max(m8, axis=0, keepdims=True)
        m8 = jnp.full((SUBLANES, tq), -jnp.inf, F32)
        l8 = jnp.zeros((SUBLANES, tq), F32)
        acc = jnp.zeros((HEAD_W, tq), F32)
        for c in range(n_chunks):
            if p < len(problems):
                m8 = scores(p, c, m8)
            l8, acc = weigh(p - 1, c, mx, l8, acc)
        outs.append(acc * (1.0 / jnp.sum(l8, axis=0, keepdims=True)))
    return outs


def _attn_kernel(lam_ref, subln_ref, q_ref, k_ref, vt_ref, o_ref, s_scr,
                 *, diff, lambda_init, heads_per_step, kv_heads_per_step, ctx_tile, ctx_lo, n_all, n_ctx):
    def run(k_lo, n_keys):
        problems = []
        for hh in range(heads_per_step):
            q = q_ref[0, :, hh * HEAD_W:(hh + 1) * HEAD_W]
            kvh = hh * kv_heads_per_step // heads_per_step
            if diff:
                lane = lax.broadcasted_iota(jnp.int32, q.shape, 1)
                first = (lane % (HEAD_W // 2)) < (HEAD_W // 4)
                zero = jnp.zeros_like(q)
                problems += [(jnp.where(first, q, zero), kvh), (jnp.where(first, zero, q), kvh)]
            else:
                problems.append((q, kvh))
        outs = _softmax_pv_pipeline(problems, k_ref, vt_ref, s_scr, k_lo, n_keys)
        for hh in range(heads_per_step):
            if diff:
                lv = lam_ref[...]
                lam = (jnp.exp(jnp.sum(lv[0:1] * lv[1:2], axis=-1, keepdims=True))
                       - jnp.exp(jnp.sum(lv[2:3] * lv[3:4], axis=-1, keepdims=True)) + lambda_init)
                o = (outs[2 * hh] - lam * outs[2 * hh + 1]).T
                o = _rms(o, subln_ref[...]) * (1.0 - lambda_init)
            else:
                o = outs[hh].T
            o_ref[0, :, hh * HEAD_W:(hh + 1) * HEAD_W] = o.astype(BF16)

    if ctx_tile is None:
        run(0, n_all)
    else:
        i = pl.program_id(2)

        @pl.when(i != ctx_tile)
        def _():
            run(0, n_all)

        @pl.when(i == ctx_tile)
        def _():
            run(ctx_lo, n_ctx)


def _attention(q, k, vt, lam, subln, *, diff, lambda_init, n_heads, heads_per_step, kv_heads_per_step,
               n_q_tiles, ctx_tile, n_ctx):
    batch, t, _ = q.shape
    qw = heads_per_step * HEAD_W
    kw = kv_heads_per_step * HEAD_W
    kern = functools.partial(
        _attn_kernel, diff=diff, lambda_init=lambda_init, heads_per_step=heads_per_step,
        kv_heads_per_step=kv_heads_per_step, ctx_tile=ctx_tile, ctx_lo=t - n_ctx, n_all=t, n_ctx=n_ctx)
    return pl.pallas_call(
        kern,
        grid=(batch, n_heads // heads_per_step, n_q_tiles),
        in_specs=[
            _resident(lam.shape),
            _resident(subln.shape),
            pl.BlockSpec((1, ROW_TILE, qw), lambda b, h, i: (b, i, h)),
            pl.BlockSpec((1, t, kw), lambda b, h, i: (b, 0, h)),
            pl.BlockSpec((1, kw, t), lambda b, h, i: (b, h, 0)),
        ],
        out_specs=pl.BlockSpec((1, ROW_TILE, qw), lambda b, h, i: (b, i, h)),
        out_shape=jax.ShapeDtypeStruct((batch, n_q_tiles * ROW_TILE, n_heads * HEAD_W), BF16),
        scratch_shapes=[pltpu.VMEM((2, t, ROW_TILE), F32)],
        compiler_params=_params(3),
        name="diff_attention" if diff else "gqa_attention",
    )(lam, subln, q, k, vt)


def _dft_kernel(cl_ref, sl_ref, cc_ref, sc_ref, xl_c_ref, xl_s_ref, xc_c_ref, xc_s_ref, o_ref,
                *, ctx_tile):
    i = pl.program_id(1)

    @pl.when(i != ctx_tile)
    def _():
        o_ref[0] = (_dot(cl_ref[...], xl_c_ref[0]) + _dot(sl_ref[...], xl_s_ref[0])).astype(BF16)

    @pl.when(i == ctx_tile)
    def _():
        o_ref[0] = (_dot(cc_ref[...], xc_c_ref[0]) + _dot(sc_ref[...], xc_s_ref[0])).astype(BF16)


def _position_dft(xcs, cl, sl, cc, sc, n_lat, n_tiles):
    batch, t, d2 = xcs.shape
    d = d2 // 2
    n_ctx = t - n_lat
    ctx_tile = n_lat // ROW_TILE
    last_lat = ctx_tile - 1
    lat_rows = lambda b, i: (jnp.minimum(i, last_lat), 0)
    return pl.pallas_call(
        functools.partial(_dft_kernel, ctx_tile=ctx_tile),
        grid=(batch, n_tiles),
        in_specs=[
            pl.BlockSpec((ROW_TILE, n_lat), lat_rows),
            pl.BlockSpec((ROW_TILE, n_lat), lat_rows),
            _resident((n_ctx, n_ctx)),
            _resident((n_ctx, n_ctx)),
            pl.BlockSpec((1, n_lat, d), lambda b, i: (b, 0, 0)),
            pl.BlockSpec((1, n_lat, d), lambda b, i: (b, 0, 1)),
            pl.BlockSpec((1, n_ctx, d), lambda b, i: (b, ctx_tile, 0)),
            pl.BlockSpec((1, n_ctx, d), lambda b, i: (b, ctx_tile, 1)),
        ],
        out_specs=pl.BlockSpec((1, ROW_TILE, d), lambda b, i: (b, i, 0)),
        out_shape=jax.ShapeDtypeStruct((batch, n_tiles * ROW_TILE, d), BF16),
        compiler_params=_params(2),
        name="position_dft",
    )(cl, sl, cc, sc, xcs, xcs, xcs, xcs)


def _res_kernel(a_ref, w_ref, b_ref, x_ref, mod_ref, o_ref):
    y = _dot(a_ref[0], w_ref[...]) + b_ref[...]
    o_ref[0] = x_ref[0] + mod_ref[0][G1:G1 + 1] * y


def _residual_project(a, w, bias, x, mod, n_tiles, ctx_tile):
    batch, _, d = x.shape
    k = a.shape[2]
    row = lambda b, j: (b, j, 0)
    return pl.pallas_call(
        _res_kernel,
        grid=(batch, n_tiles),
        in_specs=[
            pl.BlockSpec((1, ROW_TILE, k), row),
            _resident((k, d)),
            _resident((1, d)),
            pl.BlockSpec((1, ROW_TILE, d), row),
            pl.BlockSpec((1, 6, d), _mod_spec(ctx_tile, batch)),
        ],
        out_specs=pl.BlockSpec((1, ROW_TILE, d), row),
        out_shape=jax.ShapeDtypeStruct((batch, n_tiles * ROW_TILE, d), F32),
        compiler_params=_params(2),
        name="residual_project",
    )(a, w, bias, x, mod)


def _ffn_kernel(x_ref, mod_ref, ln_ref, wgu_ref, wd_ref, fn_ref, o_ref, a_scr, *, d_ff, final_norm):
    x = x_ref[0]
    mod = mod_ref[0]
    h = _norm_mod(x, ln_ref[...], mod, SH2, SC2)
    for c in range(d_ff // KEY_CHUNK):
        lo = c * KEY_CHUNK
        g = _dot(h, wgu_ref[:, lo:lo + KEY_CHUNK])
        u = _dot(h, wgu_ref[:, d_ff + lo:d_ff + lo + KEY_CHUNK])
        a_scr[:, lo:lo + KEY_CHUNK] = (_silu(g) * u).astype(BF16)
    y = x + mod[G2:G2 + 1] * _dot(a_scr[...], wd_ref[...])
    if final_norm:
        y = _rms(y, fn_ref[...])
    o_ref[0] = y


def _ffn(x, mod, ln, w_gu, w_down, final_g, n_tiles, ctx_tile, final_norm):
    batch, _, d = x.shape
    d_ff = w_down.shape[0]
    row = lambda b, j: (b, j, 0)
    return pl.pallas_call(
        functools.partial(_ffn_kernel, d_ff=d_ff, final_norm=final_norm),
        grid=(batch, n_tiles),
        in_specs=[
            pl.BlockSpec((1, ROW_TILE, d), row),
            pl.BlockSpec((1, 6, d), _mod_spec(ctx_tile, batch)),
            _resident((1, d)),
            _resident((d, 2 * d_ff)),
            _resident((d_ff, d)),
            _resident((1, d)),
        ],
        out_specs=pl.BlockSpec((1, ROW_TILE, d), row),
        out_shape=jax.ShapeDtypeStruct((batch, n_tiles * ROW_TILE, d), F32),
        scratch_shapes=[pltpu.VMEM((ROW_TILE, d_ff), BF16)],
        compiler_params=_params(2),
        name="ffn",
    )(x, mod, ln, w_gu, w_down, final_g)


def _half_split_perm(n_comp, comp_dim):
    src = np.zeros(HEAD_W, np.int32)
    for m in range(n_comp):
        for i in range(comp_dim // 2):
            for p in range(2):
                src[p * (HEAD_W // 2) + m * (comp_dim // 2) + i] = m * comp_dim + 2 * i + p
    return src


def _rope_lane_tables(seq, n_ctx, head_dim):
    axis_dim = head_dim // 2
    rows = jnp.arange(seq, dtype=jnp.int32) // GRID_W
    cols = jnp.arange(seq, dtype=jnp.int32) % GRID_W
    inv_freq = ROPE_THETA ** (-jnp.arange(0, axis_dim, 2, dtype=F32) / axis_dim)
    ang = jnp.concatenate([rows[:, None].astype(F32) * inv_freq,
                           cols[:, None].astype(F32) * inv_freq], axis=-1)
    reps = (HEAD_W // 2) // (head_dim // 2)
    cos = jnp.tile(jnp.cos(ang), (1, 2 * reps))
    sin = jnp.tile(jnp.sin(ang), (1, reps))
    sin = jnp.concatenate([-sin, sin], axis=-1)
    cos = jnp.concatenate([cos, jnp.ones((n_ctx, HEAD_W), F32)], axis=0)
    sin = jnp.concatenate([sin, jnp.zeros((n_ctx, HEAD_W), F32)], axis=0)
    return cos, sin


def _dft_tables(n, scale):
    r = 1
    while r * r < n:
        r *= 2
    q = n // r
    k = jnp.arange(n, dtype=jnp.int32)[:, None]
    a_idx = (k * jnp.arange(q, dtype=jnp.int32)[None, :] * r) % n
    b_idx = (k * jnp.arange(r, dtype=jnp.int32)[None, :]) % n
    w = 2.0 * math.pi / n
    ca, sa = jnp.cos(a_idx.astype(F32) * w), jnp.sin(a_idx.astype(F32) * w)
    cb, sb = jnp.cos(b_idx.astype(F32) * w) * scale, jnp.sin(b_idx.astype(F32) * w) * scale
    cos = ca[:, :, None] * cb[:, None, :] - sa[:, :, None] * sb[:, None, :]
    sin = sa[:, :, None] * cb[:, None, :] + ca[:, :, None] * sb[:, None, :]
    return cos.reshape(n, n).astype(BF16), (-sin).reshape(n, n).astype(BF16)


def _channel_dft_weight(d, groups):
    gd = d // groups
    idx = (np.arange(gd)[:, None] * np.arange(gd)[None, :]) % gd
    ang = 2.0 * np.pi * idx / gd
    eye = np.eye(groups)
    c = np.kron(eye, np.cos(ang)) / math.sqrt(gd)
    s = np.kron(eye, np.sin(ang)) / math.sqrt(gd)
    return jnp.asarray(np.concatenate([c, s], axis=1), dtype=BF16)


def kernel(x, c, ctx, c_ctx, mod_w, mod_b, ln_mix, ln_ffn, ffn_w_gu, ffn_w_down, a_w_qkv, a_lam, a_subln, a_w_o, b_w_qkv, b_q_norm, b_k_norm, b_w_o, c_w_o, c_b_o, final_norm):
    batch, seq, d = x.shape
    n_ctx = ctx.shape[1]
    depth = mod_w.shape[0]
    t = seq + n_ctx
    assert seq % ROW_TILE == 0 and n_ctx == ROW_TILE and batch < MOD_ROWS
    n_tiles_all = t // ROW_TILE
    n_tiles_lat = seq // ROW_TILE
    ctx_tile = n_tiles_all - 1

    cs = jnp.concatenate([c, c_ctx[None, :], jnp.zeros((MOD_ROWS - batch - 1, d), F32)], axis=0)
    mod_all = _modulation(cs, mod_w, mod_b).reshape(depth, MOD_ROWS, 6, d)

    xs = jnp.concatenate([x, ctx], axis=1)

    perm_a = _half_split_perm(2, DIFF_HEAD_DIM)
    perm_b = _half_split_perm(1, GQA_HEAD_DIM)
    rope_a = _rope_lane_tables(seq, n_ctx, DIFF_HEAD_DIM)
    rope_b = _rope_lane_tables(seq, n_ctx, GQA_HEAD_DIM)
    ones_head = jnp.ones((1, HEAD_W), F32)
    no_bias = jnp.zeros((1, d), F32)

    for i in range(depth):
        last = i == depth - 1
        kind, j = i % N_MIXERS, i // N_MIXERS
        mod = mod_all[i]
        ln1 = ln_mix[i][None, :]
        n_out_tiles = n_tiles_lat if last else n_tiles_all
        out_ctx_tile = None if last else ctx_tile

        if kind == 0:
            n_q = DIFF_HEADS * HEAD_W
            cols = (np.arange(2 * DIFF_HEADS)[:, None] * HEAD_W + perm_a[None, :]).reshape(-1)
            w = a_w_qkv[j]
            w_qk = w[:, cols].astype(BF16)
            w_vt = w[:, 2 * n_q:].T.astype(BF16)
            q, k, vt = _qkv_project(
                xs, mod, ln1, w_qk, w_vt, rope_a[0], rope_a[1], ones_head, ones_head,
                n_q=n_q, n_k=n_q, head_norm=False, q_scale=DIFF_HEAD_DIM ** -0.5 * LOG2E)
            o = _attention(
                q, k, vt, a_lam[j], a_subln[j][None, :], diff=True, lambda_init=_lambda_init(i),
                n_heads=DIFF_HEADS, heads_per_step=2, kv_heads_per_step=2,
                n_q_tiles=n_out_tiles, ctx_tile=out_ctx_tile, n_ctx=n_ctx)
            xs = _residual_project(o, a_w_o[j].astype(BF16), no_bias, xs, mod, n_out_tiles, ctx_tile)
        elif kind == 1:
            n_q = GQA_HEADS * HEAD_W
            n_k = GQA_KV_HEADS * HEAD_W
            cols = (np.arange(GQA_HEADS + GQA_KV_HEADS)[:, None] * HEAD_W + perm_b[None, :]).reshape(-1)
            w = b_w_qkv[j]
            w_qk = w[:, cols].astype(BF16)
            w_vt = w[:, n_q + n_k:].T.astype(BF16)
            q, k, vt = _qkv_project(
                xs, mod, ln1, w_qk, w_vt, rope_b[0], rope_b[1],
                b_q_norm[j][perm_b][None, :], b_k_norm[j][perm_b][None, :],
                n_q=n_q, n_k=n_k, head_norm=True, q_scale=GQA_HEAD_DIM ** -0.5 * LOG2E)
            o = _attention(
                q, k, vt, jnp.zeros((4, DIFF_HEAD_DIM), F32), ones_head, diff=False, lambda_init=0.0,
                n_heads=GQA_HEADS, heads_per_step=GQA_HEADS // GQA_KV_HEADS, kv_heads_per_step=1,
                n_q_tiles=n_out_tiles, ctx_tile=out_ctx_tile, n_ctx=n_ctx)
            xs = _residual_project(o, b_w_o[j].astype(BF16), no_bias, xs, mod, n_out_tiles, ctx_tile)
        else:
            xcs = _norm_project(xs, mod, ln1, _channel_dft_weight(d, FNET_GROUPS))
            cl, sl = _dft_tables(seq, seq ** -0.5)
            cc, sc = _dft_tables(n_ctx, n_ctx ** -0.5)
            f = _position_dft(xcs, cl, sl, cc, sc, seq, n_out_tiles)
            xs = _residual_project(f, c_w_o[j].astype(BF16), c_b_o[j][None, :], xs, mod, n_out_tiles, ctx_tile)

        xs = _ffn(xs, mod, ln_ffn[i][None, :], ffn_w_gu[i].astype(BF16), ffn_w_down[i].astype(BF16),
                  final_norm[None, :], n_out_tiles, ctx_tile, final_norm=last)
    return xs
```

```python
import functools
import math

import numpy as np
import jax
import jax.numpy as jnp
from jax import lax
from jax.experimental import pallas as pl
from jax.experimental.pallas import tpu as pltpu

GRID_W = 64
ROPE_THETA = 10000.0
NORM_EPS = 1e-6
DIFF_HEADS = 8
DIFF_HEAD_DIM = 64
GQA_HEADS = 8
GQA_KV_HEADS = 2
GQA_HEAD_DIM = 128
FNET_GROUPS = 4
N_MIXERS = 3

LANES = 128
SUBLANES = 8
VMEM_LIMIT = 56 * 1024 * 1024

ROW_TILE = 256
HEAD_W = 128
KEY_CHUNK = 256
SCORE_LEAD = 2
FF_CHUNK = 256
MOD_ROWS = 16
LOG2E = math.log2(math.e)

BF16 = jnp.bfloat16
F32 = jnp.float32

SH1, SC1, G1, SH2, SC2, G2 = range(6)


def _lambda_init(layer_idx):
    return 0.8 - 0.6 * float(np.exp(-0.3 * layer_idx))


def _params(n_grid):
    return pltpu.CompilerParams(
        dimension_semantics=("arbitrary",) * n_grid, vmem_limit_bytes=VMEM_LIMIT)


def _resident(shape):
    nd = len(shape)
    return pl.BlockSpec(shape, lambda *_: (0,) * nd, pipeline_mode=pl.Buffered(1))


def _dot(a, b):
    return jnp.dot(a, b, preferred_element_type=F32)


def _dot_nt(a, b):
    return lax.dot_general(a, b, (((1,), (1,)), ((), ())), preferred_element_type=F32)


def _rms(x, g):
    return x * lax.rsqrt(jnp.mean(x * x, axis=-1, keepdims=True) + NORM_EPS) * g


def _silu(x):
    return x / (1.0 + jnp.exp(-x))


def _mod_kernel(cs_ref, w_ref, b_ref, o_ref):
    s = _silu(cs_ref[...]).astype(BF16)
    o_ref[0] = _dot(s, w_ref[0].astype(BF16)) + b_ref[0]


def _modulation(cs, mod_w, mod_b, tn=1536):
    depth, d, n = mod_w.shape
    return pl.pallas_call(
        _mod_kernel,
        grid=(depth, n // tn),
        in_specs=[
            pl.BlockSpec((MOD_ROWS, d), lambda i, j: (0, 0)),
            pl.BlockSpec((1, d, tn), lambda i, j: (i, 0, j)),
            pl.BlockSpec((1, 1, tn), lambda i, j: (i, 0, j)),
        ],
        out_specs=pl.BlockSpec((1, MOD_ROWS, tn), lambda i, j: (i, 0, j)),
        out_shape=jax.ShapeDtypeStruct((depth, MOD_ROWS, n), F32),
        compiler_params=_params(2),
        name="modulation",
    )(cs, mod_w, mod_b.reshape(depth, 1, n))


def _mod_spec(ctx_tile, batch):
    def index(b, j):
        return (jnp.where(j == ctx_tile, batch, b), 0, 0)
    return index


def _norm_mod(x, ln, mod, shift, scale):
    y = _rms(x, ln)
    return (y * (1.0 + mod[scale:scale + 1]) + mod[shift:shift + 1]).astype(BF16)


def _qkv_kernel(x_ref, mod_ref, ln_ref, w_ref, wvt_ref, cos_ref, sin_ref, qn_ref, kn_ref,
                q_ref, k_ref, vt_ref, *, n_q, n_k, head_norm, q_scale):
    h = _norm_mod(x_ref[0], ln_ref[...], mod_ref[0], SH1, SC1)
    cos, sin = cos_ref[...], sin_ref[...]
    for head in range((n_q + n_k) // HEAD_W):
        is_q = head < n_q // HEAD_W
        if head % 2 == 0:
            pair = _dot(h, w_ref[:, head * HEAD_W:(head + 2) * HEAD_W])
        t = pair[:, (head % 2) * HEAD_W:(head % 2 + 1) * HEAD_W]
        if head_norm:
            t = _rms(t, qn_ref[...] if is_q else kn_ref[...])
        t = t * cos + pltpu.roll(t, HEAD_W // 2, 1) * sin
        if is_q:
            q_ref[0, :, head * HEAD_W:(head + 1) * HEAD_W] = (t * q_scale).astype(BF16)
        else:
            c0 = head * HEAD_W - n_q
            k_ref[0, :, c0:c0 + HEAD_W] = t.astype(BF16)
    vt_ref[0] = _dot_nt(wvt_ref[...], h).astype(BF16)


def _qkv_project(x, mod, ln, w_qk, w_vt, cosf, sinf, qn, kn, *, n_q, n_k, head_norm, q_scale):
    batch, t, d = x.shape
    n_v = w_vt.shape[0]
    n_tiles = t // ROW_TILE
    row = lambda b, j: (b, j, 0)
    kern = functools.partial(_qkv_kernel, n_q=n_q, n_k=n_k, head_norm=head_norm, q_scale=q_scale)
    return pl.pallas_call(
        kern,
        grid=(batch, n_tiles),
        in_specs=[
            pl.BlockSpec((1, ROW_TILE, d), row),
            pl.BlockSpec((1, 6, d), _mod_spec(n_tiles - 1, batch)),
            _resident((1, d)),
            _resident((d, n_q + n_k)),
            _resident((n_v, d)),
            pl.BlockSpec((ROW_TILE, HEAD_W), lambda b, j: (j, 0)),
            pl.BlockSpec((ROW_TILE, HEAD_W), lambda b, j: (j, 0)),
            _resident((1, HEAD_W)),
            _resident((1, HEAD_W)),
        ],
        out_specs=[
            pl.BlockSpec((1, ROW_TILE, n_q), row),
            pl.BlockSpec((1, ROW_TILE, n_k), row),
            pl.BlockSpec((1, n_v, ROW_TILE), lambda b, j: (b, 0, j)),
        ],
        out_shape=[
            jax.ShapeDtypeStruct((batch, t, n_q), BF16),
            jax.ShapeDtypeStruct((batch, t, n_k), BF16),
            jax.ShapeDtypeStruct((batch, n_v, t), BF16),
        ],
        compiler_params=_params(2),
        name="qkv_project",
    )(x, mod, ln, w_qk, w_vt, cosf, sinf, qn, kn)


def _lin_kernel(x_ref, mod_ref, ln_ref, w_ref, o_ref):
    h = _norm_mod(x_ref[0], ln_ref[...], mod_ref[0], SH1, SC1)
    o_ref[0] = _dot(h, w_ref[...]).astype(BF16)


def _norm_project(x, mod, ln, w):
    batch, t, d = x.shape
    n = w.shape[1]
    n_tiles = t // ROW_TILE
    row = lambda b, j: (b, j, 0)
    return pl.pallas_call(
        _lin_kernel,
        grid=(batch, n_tiles),
        in_specs=[
            pl.BlockSpec((1, ROW_TILE, d), row),
            pl.BlockSpec((1, 6, d), _mod_spec(n_tiles - 1, batch)),
            _resident((1, d)),
            _resident((d, n)),
        ],
        out_specs=pl.BlockSpec((1, ROW_TILE, n), row),
        out_shape=jax.ShapeDtypeStruct((batch, t, n), BF16),
        compiler_params=_params(2),
        name="norm_project",
    )(x, mod, ln, w)


def _softmax_pv_pipeline(problems, k_ref, vt_ref, s_scr, k_lo, n_keys, on_done):
    tq = problems[0][0].shape[0]
    n_chunks = n_keys // KEY_CHUNK
    n_prob = len(problems)
    lead = min(SCORE_LEAD, n_chunks - 1)
    ring = n_chunks + lead + 1
    fold = (KEY_CHUNK // SUBLANES, SUBLANES, tq)

    def slot(p, c):
        lo = ((p * n_chunks + c) % ring) * KEY_CHUNK
        return slice(lo, lo + KEY_CHUNK)

    def scores(p, c, m8):
        q, kvh = problems[p]
        rows = slice(k_lo + c * KEY_CHUNK, k_lo + (c + 1) * KEY_CHUNK)
        s = _dot_nt(k_ref[0, rows, kvh * HEAD_W:(kvh + 1) * HEAD_W], q)
        s_scr[slot(p, c), :] = s
        return jnp.maximum(m8, jnp.max(s.reshape(fold), axis=0))

    def weigh(p, c, mx, l8, acc):
        kvh = problems[p][1]
        e = jnp.exp2(s_scr[slot(p, c), :] - mx)
        l8 = l8 + jnp.sum(e.reshape(fold), axis=0)
        cols = slice(k_lo + c * KEY_CHUNK, k_lo + (c + 1) * KEY_CHUNK)
        acc = acc + _dot(vt_ref[0, kvh * HEAD_W:(kvh + 1) * HEAD_W, cols], e.astype(BF16))
        return l8, acc

    m8 = [None] * n_prob
    mx = l8 = acc = None
    for g in range(n_prob * n_chunks + n_chunks + lead):
        if g < n_prob * n_chunks:
            p, c = divmod(g, n_chunks)
            if c == 0:
                m8[p] = jnp.full((SUBLANES, tq), -jnp.inf, F32)
            m8[p] = scores(p, c, m8[p])
        ge = g - n_chunks - lead
        if ge >= 0:
            p, c = divmod(ge, n_chunks)
            if c == 0:
                mx = jnp.max(m8[p], axis=0, keepdims=True)
                l8 = jnp.zeros((SUBLANES, tq), F32)
                acc = jnp.zeros((HEAD_W, tq), F32)
            l8, acc = weigh(p, c, mx, l8, acc)
            if c == n_chunks - 1:
                on_done(p, acc * (1.0 / jnp.sum(l8, axis=0, keepdims=True)))


def _attn_kernel(lam_ref, subln_ref, q_ref, k_ref, vt_ref, o_ref, s_scr,
                 *, diff, lambda_init, heads_per_step, kv_heads_per_step, ctx_tile, ctx_lo, n_all, n_ctx):
    def run(k_lo, n_keys):
        problems = []
        for hh in range(heads_per_step):
            q = q_ref[0, :, hh * HEAD_W:(hh + 1) * HEAD_W]
            kvh = hh * kv_heads_per_step // heads_per_step
            if diff:
                lane = lax.broadcasted_iota(jnp.int32, q.shape, 1)
                first = (lane % (HEAD_W // 2)) < (HEAD_W // 4)
                zero = jnp.zeros_like(q)
                problems += [(jnp.where(first, q, zero), kvh), (jnp.where(first, zero, q), kvh)]
            else:
                problems.append((q, kvh))
        first_comp = {}

        def on_done(p, o_t):
            if not diff:
                o_ref[0, :, p * HEAD_W:(p + 1) * HEAD_W] = o_t.T.astype(BF16)
                return
            hh = p // 2
            if p % 2 == 0:
                first_comp[hh] = o_t
                return
            lv = lam_ref[...]
            lam = (jnp.exp(jnp.sum(lv[0:1] * lv[1:2], axis=-1, keepdims=True))
                   - jnp.exp(jnp.sum(lv[2:3] * lv[3:4], axis=-1, keepdims=True)) + lambda_init)
            o = (first_comp.pop(hh) - lam * o_t).T
            o = _rms(o, subln_ref[...]) * (1.0 - lambda_init)
            o_ref[0, :, hh * HEAD_W:(hh + 1) * HEAD_W] = o.astype(BF16)

        _softmax_pv_pipeline(problems, k_ref, vt_ref, s_scr, k_lo, n_keys, on_done)

    if ctx_tile is None:
        run(0, n_all)
    else:
        i = pl.program_id(2)

        @pl.when(i != ctx_tile)
        def _():
            run(0, n_all)

        @pl.when(i == ctx_tile)
        def _():
            run(ctx_lo, n_ctx)


def _attention(q, k, vt, lam, subln, *, diff, lambda_init, n_heads, heads_per_step, kv_heads_per_step,
               n_q_tiles, ctx_tile, n_ctx):
    batch, t, _ = q.shape
    qw = heads_per_step * HEAD_W
    kw = kv_heads_per_step * HEAD_W
    kern = functools.partial(
        _attn_kernel, diff=diff, lambda_init=lambda_init, heads_per_step=heads_per_step,
        kv_heads_per_step=kv_heads_per_step, ctx_tile=ctx_tile, ctx_lo=t - n_ctx, n_all=t, n_ctx=n_ctx)
    return pl.pallas_call(
        kern,
        grid=(batch, n_heads // heads_per_step, n_q_tiles),
        in_specs=[
            _resident(lam.shape),
            _resident(subln.shape),
            pl.BlockSpec((1, ROW_TILE, qw), lambda b, h, i: (b, i, h)),
            pl.BlockSpec((1, t, kw), lambda b, h, i: (b, 0, h)),
            pl.BlockSpec((1, kw, t), lambda b, h, i: (b, h, 0)),
        ],
        out_specs=pl.BlockSpec((1, ROW_TILE, qw), lambda b, h, i: (b, i, h)),
        out_shape=jax.ShapeDtypeStruct((batch, n_q_tiles * ROW_TILE, n_heads * HEAD_W), BF16),
        scratch_shapes=[pltpu.VMEM((t + (SCORE_LEAD + 1) * KEY_CHUNK, ROW_TILE), F32)],
        compiler_params=_params(3),
        name="diff_attention" if diff else "gqa_attention",
    )(lam, subln, q, k, vt)


def _dft_kernel(cl_ref, sl_ref, cc_ref, sc_ref, xl_c_ref, xl_s_ref, xc_c_ref, xc_s_ref, o_ref,
                *, ctx_tile):
    i = pl.program_id(1)

    @pl.when(i != ctx_tile)
    def _():
        o_ref[0] = (_dot(cl_ref[...], xl_c_ref[0]) + _dot(sl_ref[...], xl_s_ref[0])).astype(BF16)

    @pl.when(i == ctx_tile)
    def _():
        o_ref[0] = (_dot(cc_ref[...], xc_c_ref[0]) + _dot(sc_ref[...], xc_s_ref[0])).astype(BF16)


def _position_dft(xcs, cl, sl, cc, sc, n_lat, n_tiles):
    batch, t, d2 = xcs.shape
    d = d2 // 2
    n_ctx = t - n_lat
    ctx_tile = n_lat // ROW_TILE
    last_lat = ctx_tile - 1
    lat_rows = lambda b, i: (jnp.minimum(i, last_lat), 0)
    return pl.pallas_call(
        functools.partial(_dft_kernel, ctx_tile=ctx_tile),
        grid=(batch, n_tiles),
        in_specs=[
            pl.BlockSpec((ROW_TILE, n_lat), lat_rows),
            pl.BlockSpec((ROW_TILE, n_lat), lat_rows),
            _resident((n_ctx, n_ctx)),
            _resident((n_ctx, n_ctx)),
            pl.BlockSpec((1, n_lat, d), lambda b, i: (b, 0, 0)),
            pl.BlockSpec((1, n_lat, d), lambda b, i: (b, 0, 1)),
            pl.BlockSpec((1, n_ctx, d), lambda b, i: (b, ctx_tile, 0)),
            pl.BlockSpec((1, n_ctx, d), lambda b, i: (b, ctx_tile, 1)),
        ],
        out_specs=pl.BlockSpec((1, ROW_TILE, d), lambda b, i: (b, i, 0)),
        out_shape=jax.ShapeDtypeStruct((batch, n_tiles * ROW_TILE, d), BF16),
        compiler_params=_params(2),
        name="position_dft",
    )(cl, sl, cc, sc, xcs, xcs, xcs, xcs)


def _out_ffn_kernel(a_ref, wo_ref, bo_ref, x_ref, mod_ref, ln_ref, wgu_ref, wd_ref, fn_ref, o_ref, a_scr,
                    *, d_ff, final_norm):
    mod = mod_ref[0]
    x = x_ref[0] + mod[G1:G1 + 1] * (_dot(a_ref[0], wo_ref[...]) + bo_ref[...])
    h = _norm_mod(x, ln_ref[...], mod, SH2, SC2)
    for c in range(d_ff // FF_CHUNK):
        lo = c * FF_CHUNK
        g = _dot(h, wgu_ref[:, lo:lo + FF_CHUNK])
        u = _dot(h, wgu_ref[:, d_ff + lo:d_ff + lo + FF_CHUNK])
        a_scr[:, lo:lo + FF_CHUNK] = (_silu(g) * u).astype(BF16)
    y = x + mod[G2:G2 + 1] * _dot(a_scr[...], wd_ref[...])
    if final_norm:
        y = _rms(y, fn_ref[...])
    o_ref[0] = y


def _out_ffn(a, w_o, b_o, x, mod, ln, w_gu, w_down, final_g, n_tiles, ctx_tile, final_norm):
    batch, _, d = x.shape
    k = a.shape[2]
    d_ff = w_down.shape[0]
    row = lambda b, j: (b, j, 0)
    return pl.pallas_call(
        functools.partial(_out_ffn_kernel, d_ff=d_ff, final_norm=final_norm),
        grid=(batch, n_tiles),
        in_specs=[
            pl.BlockSpec((1, ROW_TILE, k), row),
            _resident((k, d)),
            _resident((1, d)),
            pl.BlockSpec((1, ROW_TILE, d), row),
            pl.BlockSpec((1, 6, d), _mod_spec(ctx_tile, batch)),
            _resident((1, d)),
            _resident((d, 2 * d_ff)),
            _resident((d_ff, d)),
            _resident((1, d)),
        ],
        out_specs=pl.BlockSpec((1, ROW_TILE, d), row),
        out_shape=jax.ShapeDtypeStruct((batch, n_tiles * ROW_TILE, d), F32),
        scratch_shapes=[pltpu.VMEM((ROW_TILE, d_ff), BF16)],
        compiler_params=_params(2),
        name="out_ffn",
    )(a, w_o, b_o, x, mod, ln, w_gu, w_down, final_g)


def _half_split_perm(n_comp, comp_dim):
    src = np.zeros(HEAD_W, np.int32)
    for m in range(n_comp):
        for i in range(comp_dim // 2):
            for p in range(2):
                src[p * (HEAD_W // 2) + m * (comp_dim // 2) + i] = m * comp_dim + 2 * i + p
    return src


def _rope_lane_tables(seq, n_ctx, head_dim):
    axis_dim = head_dim // 2
    rows = jnp.arange(seq, dtype=jnp.int32) // GRID_W
    cols = jnp.arange(seq, dtype=jnp.int32) % GRID_W
    inv_freq = ROPE_THETA ** (-jnp.arange(0, axis_dim, 2, dtype=F32) / axis_dim)
    ang = jnp.concatenate([rows[:, None].astype(F32) * inv_freq,
                           cols[:, None].astype(F32) * inv_freq], axis=-1)
    reps = (HEAD_W // 2) // (head_dim // 2)
    cos = jnp.tile(jnp.cos(ang), (1, 2 * reps))
    sin = jnp.tile(jnp.sin(ang), (1, reps))
    sin = jnp.concatenate([-sin, sin], axis=-1)
    cos = jnp.concatenate([cos, jnp.ones((n_ctx, HEAD_W), F32)], axis=0)
    sin = jnp.concatenate([sin, jnp.zeros((n_ctx, HEAD_W), F32)], axis=0)
    return cos, sin


def _dft_tables(n, scale):
    r = 1
    while r * r < n:
        r *= 2
    q = n // r
    k = jnp.arange(n, dtype=jnp.int32)[:, None]
    a_idx = (k * jnp.arange(q, dtype=jnp.int32)[None, :] * r) % n
    b_idx = (k * jnp.arange(r, dtype=jnp.int32)[None, :]) % n
    w = 2.0 * math.pi / n
    ca, sa = jnp.cos(a_idx.astype(F32) * w), jnp.sin(a_idx.astype(F32) * w)
    cb, sb = jnp.cos(b_idx.astype(F32) * w) * scale, jnp.sin(b_idx.astype(F32) * w) * scale
    cos = ca[:, :, None] * cb[:, None, :] - sa[:, :, None] * sb[:, None, :]
    sin = sa[:, :, None] * cb[:, None, :] + ca[:, :, None] * sb[:, None, :]
    return cos.reshape(n, n).astype(BF16), (-sin).reshape(n, n).astype(BF16)


def _channel_dft_weight(d, groups):
    gd = d // groups
    idx = (np.arange(gd)[:, None] * np.arange(gd)[None, :]) % gd
    ang = 2.0 * np.pi * idx / gd
    eye = np.eye(groups)
    c = np.kron(eye, np.cos(ang)) / math.sqrt(gd)
    s = np.kron(eye, np.sin(ang)) / math.sqrt(gd)
    return jnp.asarray(np.concatenate([c, s], axis=1), dtype=BF16)


def kernel(x, c, ctx, c_ctx, mod_w, mod_b, ln_mix, ln_ffn, ffn_w_gu, ffn_w_down, a_w_qkv, a_lam, a_subln, a_w_o, b_w_qkv, b_q_norm, b_k_norm, b_w_o, c_w_o, c_b_o, final_norm):
    batch, seq, d = x.shape
    n_ctx = ctx.shape[1]
    depth = mod_w.shape[0]
    t = seq + n_ctx
    assert seq % ROW_TILE == 0 and n_ctx == ROW_TILE and batch < MOD_ROWS
    n_tiles_all = t // ROW_TILE
    n_tiles_lat = seq // ROW_TILE
    ctx_tile = n_tiles_all - 1

    cs = jnp.concatenate([c, c_ctx[None, :], jnp.zeros((MOD_ROWS - batch - 1, d), F32)], axis=0)
    mod_all = _modulation(cs, mod_w, mod_b).reshape(depth, MOD_ROWS, 6, d)

    xs = jnp.concatenate([x, ctx], axis=1)

    perm_a = _half_split_perm(2, DIFF_HEAD_DIM)
    perm_b = _half_split_perm(1, GQA_HEAD_DIM)
    rope_a = _rope_lane_tables(seq, n_ctx, DIFF_HEAD_DIM)
    rope_b = _rope_lane_tables(seq, n_ctx, GQA_HEAD_DIM)
    ones_head = jnp.ones((1, HEAD_W), F32)
    no_bias = jnp.zeros((1, d), F32)

    for i in range(depth):
        last = i == depth - 1
        kind, j = i % N_MIXERS, i // N_MIXERS
        mod = mod_all[i]
        ln1 = ln_mix[i][None, :]
        n_out_tiles = n_tiles_lat if last else n_tiles_all
        out_ctx_tile = None if last else ctx_tile

        if kind == 0:
            n_q = DIFF_HEADS * HEAD_W
            cols = (np.arange(2 * DIFF_HEADS)[:, None] * HEAD_W + perm_a[None, :]).reshape(-1)
            w = a_w_qkv[j]
            w_qk = w[:, cols].astype(BF16)
            w_vt = w[:, 2 * n_q:].T.astype(BF16)
            q, k, vt = _qkv_project(
                xs, mod, ln1, w_qk, w_vt, rope_a[0], rope_a[1], ones_head, ones_head,
                n_q=n_q, n_k=n_q, head_norm=False, q_scale=DIFF_HEAD_DIM ** -0.5 * LOG2E)
            o = _attention(
                q, k, vt, a_lam[j], a_subln[j][None, :], diff=True, lambda_init=_lambda_init(i),
                n_heads=DIFF_HEADS, heads_per_step=4, kv_heads_per_step=4,
                n_q_tiles=n_out_tiles, ctx_tile=out_ctx_tile, n_ctx=n_ctx)
            mixed, w_o, b_o = o, a_w_o[j], no_bias
        elif kind == 1:
            n_q = GQA_HEADS * HEAD_W
            n_k = GQA_KV_HEADS * HEAD_W
            cols = (np.arange(GQA_HEADS + GQA_KV_HEADS)[:, None] * HEAD_W + perm_b[None, :]).reshape(-1)
            w = b_w_qkv[j]
            w_qk = w[:, cols].astype(BF16)
            w_vt = w[:, n_q + n_k:].T.astype(BF16)
            q, k, vt = _qkv_project(
                xs, mod, ln1, w_qk, w_vt, rope_b[0], rope_b[1],
                b_q_norm[j][perm_b][None, :], b_k_norm[j][perm_b][None, :],
                n_q=n_q, n_k=n_k, head_norm=True, q_scale=GQA_HEAD_DIM ** -0.5 * LOG2E)
            o = _attention(
                q, k, vt, jnp.zeros((4, DIFF_HEAD_DIM), F32), ones_head, diff=False, lambda_init=0.0,
                n_heads=GQA_HEADS, heads_per_step=GQA_HEADS, kv_heads_per_step=GQA_KV_HEADS,
                n_q_tiles=n_out_tiles, ctx_tile=out_ctx_tile, n_ctx=n_ctx)
            mixed, w_o, b_o = o, b_w_o[j], no_bias
        else:
            xcs = _norm_project(xs, mod, ln1, _channel_dft_weight(d, FNET_GROUPS))
            cl, sl = _dft_tables(seq, seq ** -0.5)
            cc, sc = _dft_tables(n_ctx, n_ctx ** -0.5)
            f = _position_dft(xcs, cl, sl, cc, sc, seq, n_out_tiles)
            mixed, w_o, b_o = f, c_w_o[j], c_b_o[j][None, :]

        xs = _out_ffn(mixed, w_o.astype(BF16), b_o, xs, mod, ln_ffn[i][None, :],
                      ffn_w_gu[i].astype(BF16), ffn_w_down[i].astype(BF16), final_norm[None, :],
                      n_out_tiles, ctx_tile, final_norm=last)
    return xs
```

```python
import functools
import math

import numpy as np
import jax
import jax.numpy as jnp
from jax import lax
from jax.experimental import pallas as pl
from jax.experimental.pallas import tpu as pltpu

GRID_W = 64
ROPE_THETA = 10000.0
NORM_EPS = 1e-6
DIFF_HEADS = 8
DIFF_HEAD_DIM = 64
GQA_HEADS = 8
GQA_KV_HEADS = 2
GQA_HEAD_DIM = 128
FNET_GROUPS = 4
N_MIXERS = 3

LANES = 128
SUBLANES = 8
VMEM_LIMIT = 56 * 1024 * 1024

ROW_TILE = 256
HEAD_W = 128
KEY_CHUNK = 256
SCORE_LEAD = 2
FF_CHUNK = 256
ONES_ROWS = 16
MOD_ROWS = 16
LOG2E = math.log2(math.e)

BF16 = jnp.bfloat16
F32 = jnp.float32

SH1, SC1, G1, SH2, SC2, G2 = range(6)


def _lambda_init(layer_idx):
    return 0.8 - 0.6 * float(np.exp(-0.3 * layer_idx))


def _params(n_grid, flags=None):
    return pltpu.CompilerParams(
        dimension_semantics=("arbitrary",) * n_grid, vmem_limit_bytes=VMEM_LIMIT, flags=flags)


def _resident(shape):
    nd = len(shape)
    return pl.BlockSpec(shape, lambda *_: (0,) * nd, pipeline_mode=pl.Buffered(1))


def _dot(a, b):
    return jnp.dot(a, b, preferred_element_type=F32)


def _dot_nt(a, b):
    return lax.dot_general(a, b, (((1,), (1,)), ((), ())), preferred_element_type=F32)


def _rms(x, g):
    return x * lax.rsqrt(jnp.mean(x * x, axis=-1, keepdims=True) + NORM_EPS) * g


def _silu(x):
    return x / (1.0 + jnp.exp(-x))


def _mod_kernel(cs_ref, w_ref, b_ref, o_ref):
    s = _silu(cs_ref[...]).astype(BF16)
    o_ref[0] = _dot(s, w_ref[0].astype(BF16)) + b_ref[0]


def _modulation(cs, mod_w, mod_b, tn=1536):
    depth, d, n = mod_w.shape
    return pl.pallas_call(
        _mod_kernel,
        grid=(depth, n // tn),
        in_specs=[
            pl.BlockSpec((MOD_ROWS, d), lambda i, j: (0, 0)),
            pl.BlockSpec((1, d, tn), lambda i, j: (i, 0, j)),
            pl.BlockSpec((1, 1, tn), lambda i, j: (i, 0, j)),
        ],
        out_specs=pl.BlockSpec((1, MOD_ROWS, tn), lambda i, j: (i, 0, j)),
        out_shape=jax.ShapeDtypeStruct((depth, MOD_ROWS, n), F32),
        compiler_params=_params(2),
        name="modulation",
    )(cs, mod_w, mod_b.reshape(depth, 1, n))


def _mod_spec(ctx_tile, batch):
    def index(b, j):
        return (jnp.where(j == ctx_tile, batch, b), 0, 0)
    return index


def _norm_mod(x, ln, mod, shift, scale):
    y = _rms(x, ln)
    return (y * (1.0 + mod[scale:scale + 1]) + mod[shift:shift + 1]).astype(BF16)


def _qkv_kernel(x_ref, mod_ref, ln_ref, w_ref, wvt_ref, cos_ref, sin_ref, qn_ref, kn_ref,
                q_ref, k_ref, vt_ref, *, n_q, n_k, head_norm, q_scale):
    h = _norm_mod(x_ref[0], ln_ref[...], mod_ref[0], SH1, SC1)
    cos, sin = cos_ref[...], sin_ref[...]
    for head in range((n_q + n_k) // HEAD_W):
        is_q = head < n_q // HEAD_W
        if head % 2 == 0:
            pair = _dot(h, w_ref[:, head * HEAD_W:(head + 2) * HEAD_W])
        t = pair[:, (head % 2) * HEAD_W:(head % 2 + 1) * HEAD_W]
        if head_norm:
            t = _rms(t, qn_ref[...] if is_q else kn_ref[...])
        t = t * cos + pltpu.roll(t, HEAD_W // 2, 1) * sin
        if is_q:
            q_ref[0, :, head * HEAD_W:(head + 1) * HEAD_W] = (t * q_scale).astype(BF16)
        else:
            c0 = head * HEAD_W - n_q
            k_ref[0, :, c0:c0 + HEAD_W] = t.astype(BF16)
    vt_ref[0] = _dot_nt(wvt_ref[...], h).astype(BF16)


def _qkv_project(x, mod, ln, w_qk, w_vt, cosf, sinf, qn, kn, *, n_q, n_k, head_norm, q_scale):
    batch, t, d = x.shape
    n_v = w_vt.shape[0]
    n_tiles = t // ROW_TILE
    row = lambda b, j: (b, j, 0)
    kern = functools.partial(_qkv_kernel, n_q=n_q, n_k=n_k, head_norm=head_norm, q_scale=q_scale)
    return pl.pallas_call(
        kern,
        grid=(batch, n_tiles),
        in_specs=[
            pl.BlockSpec((1, ROW_TILE, d), row),
            pl.BlockSpec((1, 6, d), _mod_spec(n_tiles - 1, batch)),
            _resident((1, d)),
            _resident((d, n_q + n_k)),
            _resident((n_v, d)),
            pl.BlockSpec((ROW_TILE, HEAD_W), lambda b, j: (j, 0)),
            pl.BlockSpec((ROW_TILE, HEAD_W), lambda b, j: (j, 0)),
            _resident((1, HEAD_W)),
            _resident((1, HEAD_W)),
        ],
        out_specs=[
            pl.BlockSpec((1, ROW_TILE, n_q), row),
            pl.BlockSpec((1, ROW_TILE, n_k), row),
            pl.BlockSpec((1, n_v, ROW_TILE), lambda b, j: (b, 0, j)),
        ],
        out_shape=[
            jax.ShapeDtypeStruct((batch, t, n_q), BF16),
            jax.ShapeDtypeStruct((batch, t, n_k), BF16),
            jax.ShapeDtypeStruct((batch, n_v, t), BF16),
        ],
        compiler_params=_params(2),
        name="qkv_project",
    )(x, mod, ln, w_qk, w_vt, cosf, sinf, qn, kn)


def _chan_dft_kernel(x_ref, mod_ref, ln_ref, w_ref, o_ref):
    h = _norm_mod(x_ref[0], ln_ref[...], mod_ref[0], SH1, SC1)
    d = h.shape[1]
    gd = w_ref.shape[0]
    for g in range(d // gd):
        y = _dot(h[:, g * gd:(g + 1) * gd], w_ref[...])
        o_ref[0, :, g * gd:(g + 1) * gd] = y[:, :gd].astype(BF16)
        o_ref[0, :, d + g * gd:d + (g + 1) * gd] = y[:, gd:].astype(BF16)


def _channel_dft(x, mod, ln, w):
    batch, t, d = x.shape
    n_tiles = t // ROW_TILE
    row = lambda b, j: (b, j, 0)
    return pl.pallas_call(
        _chan_dft_kernel,
        grid=(batch, n_tiles),
        in_specs=[
            pl.BlockSpec((1, ROW_TILE, d), row),
            pl.BlockSpec((1, 6, d), _mod_spec(n_tiles - 1, batch)),
            _resident((1, d)),
            _resident(w.shape),
        ],
        out_specs=pl.BlockSpec((1, ROW_TILE, 2 * d), row),
        out_shape=jax.ShapeDtypeStruct((batch, t, 2 * d), BF16),
        compiler_params=_params(2),
        name="channel_dft",
    )(x, mod, ln, w)


def _softmax_pv_pipeline(problems, k_ref, vt_ref, s_scr, k_lo, n_keys, on_done):
    tq = problems[0][0].shape[0]
    n_chunks = n_keys // KEY_CHUNK
    n_prob = len(problems)
    lead = min(SCORE_LEAD, n_chunks - 1)
    ring = n_chunks + lead + 1
    fold = (KEY_CHUNK // SUBLANES, SUBLANES, tq)

    def slot(p, c):
        lo = ((p * n_chunks + c) % ring) * KEY_CHUNK
        return slice(lo, lo + KEY_CHUNK)

    def scores(p, c, m8):
        q, kvh = problems[p]
        rows = slice(k_lo + c * KEY_CHUNK, k_lo + (c + 1) * KEY_CHUNK)
        s = _dot_nt(k_ref[0, rows, kvh * HEAD_W:(kvh + 1) * HEAD_W], q)
        s_scr[slot(p, c), :] = s
        return jnp.maximum(m8, jnp.max(s.reshape(fold), axis=0))

    ones_rows = jnp.ones((ONES_ROWS, KEY_CHUNK), BF16)

    def weigh(p, c, mx, acc):
        kvh = problems[p][1]
        e = jnp.exp2(s_scr[slot(p, c), :] - mx)
        cols = slice(k_lo + c * KEY_CHUNK, k_lo + (c + 1) * KEY_CHUNK)
        v_ones = jnp.concatenate([vt_ref[0, kvh * HEAD_W:(kvh + 1) * HEAD_W, cols], ones_rows], axis=0)
        return acc + _dot(v_ones, e.astype(BF16))

    m8 = [None] * n_prob
    mx = acc = None
    for g in range(n_prob * n_chunks + n_chunks + lead):
        if g < n_prob * n_chunks:
            p, c = divmod(g, n_chunks)
            if c == 0:
                m8[p] = jnp.full((SUBLANES, tq), -jnp.inf, F32)
            m8[p] = scores(p, c, m8[p])
        ge = g - n_chunks - lead
        if ge >= 0:
            p, c = divmod(ge, n_chunks)
            if c == 0:
                mx = jnp.max(m8[p], axis=0, keepdims=True)
                acc = jnp.zeros((HEAD_W + ONES_ROWS, tq), F32)
            acc = weigh(p, c, mx, acc)
            if c == n_chunks - 1:
                on_done(p, acc[:HEAD_W] * (1.0 / acc[HEAD_W:HEAD_W + 1]))


def _attn_kernel(lam_ref, subln_ref, q_ref, k_ref, vt_ref, o_ref, s_scr,
                 *, diff, lambda_init, heads_per_step, kv_heads_per_step, n_q_tiles, ctx_tile, ctx_lo, n_all,
                 n_ctx):
    def run(k_lo, n_keys):
        problems = []
        for hh in range(heads_per_step):
            q = q_ref[0, :, hh * HEAD_W:(hh + 1) * HEAD_W]
            kvh = hh * kv_heads_per_step // heads_per_step
            if diff:
                lane = lax.broadcasted_iota(jnp.int32, q.shape, 1)
                first = (lane % (HEAD_W // 2)) < (HEAD_W // 4)
                zero = jnp.zeros_like(q)
                problems += [(jnp.where(first, q, zero), kvh), (jnp.where(first, zero, q), kvh)]
            else:
                problems.append((q, kvh))
        first_comp = {}

        def on_done(p, o_t):
            if not diff:
                o_ref[0, :, p * HEAD_W:(p + 1) * HEAD_W] = o_t.T.astype(BF16)
                return
            hh = p // 2
            if p % 2 == 0:
                first_comp[hh] = o_t
                return
            lv = lam_ref[...]
            lam = (jnp.exp(jnp.sum(lv[0:1] * lv[1:2], axis=-1, keepdims=True))
                   - jnp.exp(jnp.sum(lv[2:3] * lv[3:4], axis=-1, keepdims=True)) + lambda_init)
            o = (first_comp.pop(hh) - lam * o_t).T
            o = _rms(o, subln_ref[...]) * (1.0 - lambda_init)
            o_ref[0, :, hh * HEAD_W:(hh + 1) * HEAD_W] = o.astype(BF16)

        _softmax_pv_pipeline(problems, k_ref, vt_ref, s_scr, k_lo, n_keys, on_done)

    i = pl.program_id(2)

    @pl.when(i != ctx_tile)
    def _():
        run(0, n_all)

    if ctx_tile < n_q_tiles:
        @pl.when(i == ctx_tile)
        def _():
            run(ctx_lo, n_ctx)


def _attention(q, k, vt, lam, subln, *, diff, lambda_init, n_heads, heads_per_step, kv_heads_per_step,
               n_q_tiles, ctx_tile, n_ctx):
    batch, t, _ = q.shape
    qw = heads_per_step * HEAD_W
    kw = kv_heads_per_step * HEAD_W
    kern = functools.partial(
        _attn_kernel, diff=diff, lambda_init=lambda_init, heads_per_step=heads_per_step,
        kv_heads_per_step=kv_heads_per_step, n_q_tiles=n_q_tiles, ctx_tile=ctx_tile, ctx_lo=t - n_ctx,
        n_all=t, n_ctx=n_ctx)
    return pl.pallas_call(
        kern,
        grid=(batch, n_heads // heads_per_step, n_q_tiles),
        in_specs=[
            _resident(lam.shape),
            _resident(subln.shape),
            pl.BlockSpec((1, ROW_TILE, qw), lambda b, h, i: (b, i, h)),
            pl.BlockSpec((1, t, kw), lambda b, h, i: (b, 0, h)),
            pl.BlockSpec((1, kw, t), lambda b, h, i: (b, h, 0)),
        ],
        out_specs=pl.BlockSpec((1, ROW_TILE, qw), lambda b, h, i: (b, i, h)),
        out_shape=jax.ShapeDtypeStruct((batch, n_q_tiles * ROW_TILE, n_heads * HEAD_W), BF16),
        scratch_shapes=[pltpu.VMEM((t + (SCORE_LEAD + 1) * KEY_CHUNK, ROW_TILE), F32)],
        compiler_params=_params(3),
        name="diff_attention" if diff else "gqa_attention",
    )(lam, subln, q, k, vt)


def _dft_kernel(cl_ref, sl_ref, cc_ref, sc_ref, xl_c_ref, xl_s_ref, xc_c_ref, xc_s_ref, o_ref,
                *, ctx_tile):
    i = pl.program_id(1)

    @pl.when(i != ctx_tile)
    def _():
        o_ref[0] = (_dot(cl_ref[...], xl_c_ref[0]) + _dot(sl_ref[...], xl_s_ref[0])).astype(BF16)

    @pl.when(i == ctx_tile)
    def _():
        o_ref[0] = (_dot(cc_ref[...], xc_c_ref[0]) + _dot(sc_ref[...], xc_s_ref[0])).astype(BF16)


def _position_dft(xcs, cl, sl, cc, sc, n_lat, n_tiles):
    batch, t, d2 = xcs.shape
    d = d2 // 2
    n_ctx = t - n_lat
    ctx_tile = n_lat // ROW_TILE
    last_lat = ctx_tile - 1
    lat_rows = lambda b, i: (jnp.minimum(i, last_lat), 0)
    return pl.pallas_call(
        functools.partial(_dft_kernel, ctx_tile=ctx_tile),
        grid=(batch, n_tiles),
        in_specs=[
            pl.BlockSpec((ROW_TILE, n_lat), lat_rows),
            pl.BlockSpec((ROW_TILE, n_lat), lat_rows),
            _resident((n_ctx, n_ctx)),
            _resident((n_ctx, n_ctx)),
            pl.BlockSpec((1, n_lat, d), lambda b, i: (b, 0, 0)),
            pl.BlockSpec((1, n_lat, d), lambda b, i: (b, 0, 1)),
            pl.BlockSpec((1, n_ctx, d), lambda b, i: (b, ctx_tile, 0)),
            pl.BlockSpec((1, n_ctx, d), lambda b, i: (b, ctx_tile, 1)),
        ],
        out_specs=pl.BlockSpec((1, ROW_TILE, d), lambda b, i: (b, i, 0)),
        out_shape=jax.ShapeDtypeStruct((batch, n_tiles * ROW_TILE, d), BF16),
        compiler_params=_params(2),
        name="position_dft",
    )(cl, sl, cc, sc, xcs, xcs, xcs, xcs)


def _out_ffn_kernel(a_ref, wo_ref, bo_ref, x_ref, mod_ref, ln_ref, wgu_ref, wd_ref, fn_ref, o_ref, a_scr,
                    *, d_ff, final_norm):
    mod = mod_ref[0]
    x = x_ref[0] + mod[G1:G1 + 1] * (_dot(a_ref[0], wo_ref[...]) + bo_ref[...])
    h = _norm_mod(x, ln_ref[...], mod, SH2, SC2)
    for c in range(d_ff // FF_CHUNK):
        lo = c * FF_CHUNK
        g = _dot(h, wgu_ref[:, lo:lo + FF_CHUNK])
        u = _dot(h, wgu_ref[:, d_ff + lo:d_ff + lo + FF_CHUNK])
        a_scr[:, lo:lo + FF_CHUNK] = (_silu(g) * u).astype(BF16)
    y = x + mod[G2:G2 + 1] * _dot(a_scr[...], wd_ref[...])
    if final_norm:
        y = _rms(y, fn_ref[...])
    o_ref[0] = y


def _out_ffn(a, w_o, b_o, x, mod, ln, w_gu, w_down, final_g, n_tiles, ctx_tile, final_norm):
    batch, _, d = x.shape
    k = a.shape[2]
    d_ff = w_down.shape[0]
    row = lambda b, j: (b, j, 0)
    return pl.pallas_call(
        functools.partial(_out_ffn_kernel, d_ff=d_ff, final_norm=final_norm),
        grid=(batch, n_tiles),
        in_specs=[
            pl.BlockSpec((1, ROW_TILE, k), row),
            _resident((k, d)),
            _resident((1, d)),
            pl.BlockSpec((1, ROW_TILE, d), row),
            pl.BlockSpec((1, 6, d), _mod_spec(ctx_tile, batch)),
            _resident((1, d)),
            _resident((d, 2 * d_ff)),
            _resident((d_ff, d)),
            _resident((1, d)),
        ],
        out_specs=pl.BlockSpec((1, ROW_TILE, d), row),
        out_shape=jax.ShapeDtypeStruct((batch, n_tiles * ROW_TILE, d), F32),
        scratch_shapes=[pltpu.VMEM((ROW_TILE, d_ff), BF16)],
        compiler_params=_params(2),
        name="out_ffn",
    )(a, w_o, b_o, x, mod, ln, w_gu, w_down, final_g)


def _half_split_perm(n_comp, comp_dim):
    src = np.zeros(HEAD_W, np.int32)
    for m in range(n_comp):
        for i in range(comp_dim // 2):
            for p in range(2):
                src[p * (HEAD_W // 2) + m * (comp_dim // 2) + i] = m * comp_dim + 2 * i + p
    return src


def _rope_lane_tables(seq, n_ctx, head_dim):
    axis_dim = head_dim // 2
    rows = jnp.arange(seq, dtype=jnp.int32) // GRID_W
    cols = jnp.arange(seq, dtype=jnp.int32) % GRID_W
    inv_freq = ROPE_THETA ** (-jnp.arange(0, axis_dim, 2, dtype=F32) / axis_dim)
    ang = jnp.concatenate([rows[:, None].astype(F32) * inv_freq,
                           cols[:, None].astype(F32) * inv_freq], axis=-1)
    reps = (HEAD_W // 2) // (head_dim // 2)
    cos = jnp.tile(jnp.cos(ang), (1, 2 * reps))
    sin = jnp.tile(jnp.sin(ang), (1, reps))
    sin = jnp.concatenate([-sin, sin], axis=-1)
    cos = jnp.concatenate([cos, jnp.ones((n_ctx, HEAD_W), F32)], axis=0)
    sin = jnp.concatenate([sin, jnp.zeros((n_ctx, HEAD_W), F32)], axis=0)
    return cos, sin


def _dft_tables(n, scale):
    r = 1
    while r * r < n:
        r *= 2
    q = n // r
    k = jnp.arange(n, dtype=jnp.int32)[:, None]
    a_idx = (k * jnp.arange(q, dtype=jnp.int32)[None, :] * r) % n
    b_idx = (k * jnp.arange(r, dtype=jnp.int32)[None, :]) % n
    w = 2.0 * math.pi / n
    ca, sa = jnp.cos(a_idx.astype(F32) * w), jnp.sin(a_idx.astype(F32) * w)
    cb, sb = jnp.cos(b_idx.astype(F32) * w) * scale, jnp.sin(b_idx.astype(F32) * w) * scale
    cos = ca[:, :, None] * cb[:, None, :] - sa[:, :, None] * sb[:, None, :]
    sin = sa[:, :, None] * cb[:, None, :] + ca[:, :, None] * sb[:, None, :]
    return cos.reshape(n, n).astype(BF16), (-sin).reshape(n, n).astype(BF16)


def _channel_dft_weight(gd):
    idx = (np.arange(gd)[:, None] * np.arange(gd)[None, :]) % gd
    ang = 2.0 * np.pi * idx / gd
    w = np.concatenate([np.cos(ang), np.sin(ang)], axis=1) / math.sqrt(gd)
    return jnp.asarray(w, dtype=BF16)


def kernel(x, c, ctx, c_ctx, mod_w, mod_b, ln_mix, ln_ffn, ffn_w_gu, ffn_w_down, a_w_qkv, a_lam, a_subln, a_w_o, b_w_qkv, b_q_norm, b_k_norm, b_w_o, c_w_o, c_b_o, final_norm):
    batch, seq, d = x.shape
    n_ctx = ctx.shape[1]
    depth = mod_w.shape[0]
    t = seq + n_ctx
    assert seq % ROW_TILE == 0 and n_ctx == ROW_TILE and batch < MOD_ROWS
    n_tiles_all = t // ROW_TILE
    n_tiles_lat = seq // ROW_TILE
    ctx_tile = n_tiles_all - 1

    cs = jnp.concatenate([c, c_ctx[None, :], jnp.zeros((MOD_ROWS - batch - 1, d), F32)], axis=0)
    mod_all = _modulation(cs, mod_w, mod_b).reshape(depth, MOD_ROWS, 6, d)

    xs = jnp.concatenate([x, ctx], axis=1)

    perm_a = _half_split_perm(2, DIFF_HEAD_DIM)
    perm_b = _half_split_perm(1, GQA_HEAD_DIM)
    rope_a = _rope_lane_tables(seq, n_ctx, DIFF_HEAD_DIM)
    rope_b = _rope_lane_tables(seq, n_ctx, GQA_HEAD_DIM)
    ones_head = jnp.ones((1, HEAD_W), F32)
    no_bias = jnp.zeros((1, d), F32)

    for i in range(depth):
        last = i == depth - 1
        kind, j = i % N_MIXERS, i // N_MIXERS
        mod = mod_all[i]
        ln1 = ln_mix[i][None, :]
        n_out_tiles = n_tiles_lat if last else n_tiles_all

        if kind == 0:
            n_q = DIFF_HEADS * HEAD_W
            cols = (np.arange(2 * DIFF_HEADS)[:, None] * HEAD_W + perm_a[None, :]).reshape(-1)
            w = a_w_qkv[j]
            w_qk = w[:, cols].astype(BF16)
            w_vt = w[:, 2 * n_q:].T.astype(BF16)
            q, k, vt = _qkv_project(
                xs, mod, ln1, w_qk, w_vt, rope_a[0], rope_a[1], ones_head, ones_head,
                n_q=n_q, n_k=n_q, head_norm=False, q_scale=DIFF_HEAD_DIM ** -0.5 * LOG2E)
            o = _attention(
                q, k, vt, a_lam[j], a_subln[j][None, :], diff=True, lambda_init=_lambda_init(i),
                n_heads=DIFF_HEADS, heads_per_step=4, kv_heads_per_step=4,
                n_q_tiles=n_tiles_all, ctx_tile=ctx_tile, n_ctx=n_ctx)
            mixed, w_o, b_o = o, a_w_o[j], no_bias
        elif kind == 1:
            n_q = GQA_HEADS * HEAD_W
            n_k = GQA_KV_HEADS * HEAD_W
            cols = (np.arange(GQA_HEADS + GQA_KV_HEADS)[:, None] * HEAD_W + perm_b[None, :]).reshape(-1)
            w = b_w_qkv[j]
            w_qk = w[:, cols].astype(BF16)
            w_vt = w[:, n_q + n_k:].T.astype(BF16)
            q, k, vt = _qkv_project(
                xs, mod, ln1, w_qk, w_vt, rope_b[0], rope_b[1],
                b_q_norm[j][perm_b][None, :], b_k_norm[j][perm_b][None, :],
                n_q=n_q, n_k=n_k, head_norm=True, q_scale=GQA_HEAD_DIM ** -0.5 * LOG2E)
            o = _attention(
                q, k, vt, jnp.zeros((4, DIFF_HEAD_DIM), F32), ones_head, diff=False, lambda_init=0.0,
                n_heads=GQA_HEADS, heads_per_step=GQA_HEADS, kv_heads_per_step=GQA_KV_HEADS,
                n_q_tiles=n_tiles_all, ctx_tile=ctx_tile, n_ctx=n_ctx)
            mixed, w_o, b_o = o, b_w_o[j], no_bias
        else:
            xcs = _channel_dft(xs, mod, ln1, _channel_dft_weight(d // FNET_GROUPS))
            cl, sl = _dft_tables(seq, seq ** -0.5)
            cc, sc = _dft_tables(n_ctx, n_ctx ** -0.5)
            f = _position_dft(xcs, cl, sl, cc, sc, seq, n_out_tiles)
            mixed, w_o, b_o = f, c_w_o[j], c_b_o[j][None, :]

        xs = _out_ffn(mixed, w_o.astype(BF16), b_o, xs, mod, ln_ffn[i][None, :],
                      ffn_w_gu[i].astype(BF16), ffn_w_down[i].astype(BF16), final_norm[None, :],
                      n_out_tiles, ctx_tile, final_norm=last)
    return xs
```

```python
import functools
import math

import numpy as np
import jax
import jax.numpy as jnp
from jax import lax
from jax.experimental import pallas as pl
from jax.experimental.pallas import tpu as pltpu

GRID_W = 64
ROPE_THETA = 10000.0
NORM_EPS = 1e-6
DIFF_HEADS = 8
DIFF_HEAD_DIM = 64
GQA_HEADS = 8
GQA_KV_HEADS = 2
GQA_HEAD_DIM = 128
FNET_GROUPS = 4
N_MIXERS = 3

LANES = 128
SUBLANES = 8
VMEM_LIMIT = 56 * 1024 * 1024

ROW_TILE = 256
HEAD_W = 128
KEY_CHUNK = 256
SCORE_LEAD = 2
FF_CHUNK = 256
ONES_ROWS = 16
MOD_ROWS = 16
LOG2E = math.log2(math.e)

BF16 = jnp.bfloat16
F32 = jnp.float32

SH1, SC1, G1, SH2, SC2, G2 = range(6)


def _lambda_init(layer_idx):
    return 0.8 - 0.6 * float(np.exp(-0.3 * layer_idx))


def _params(n_grid, flags=None):
    return pltpu.CompilerParams(
        dimension_semantics=("arbitrary",) * n_grid, vmem_limit_bytes=VMEM_LIMIT, flags=flags)


def _resident(shape):
    nd = len(shape)
    return pl.BlockSpec(shape, lambda *_: (0,) * nd, pipeline_mode=pl.Buffered(1))


def _dot(a, b):
    return jnp.dot(a, b, preferred_element_type=F32)


def _dot_nt(a, b):
    return lax.dot_general(a, b, (((1,), (1,)), ((), ())), preferred_element_type=F32)


def _rms(x, g):
    return x * lax.rsqrt(jnp.mean(x * x, axis=-1, keepdims=True) + NORM_EPS) * g


def _silu(x):
    return x / (1.0 + jnp.exp(-x))


def _mod_kernel(cs_ref, w_ref, b_ref, o_ref):
    s = _silu(cs_ref[...]).astype(BF16)
    o_ref[0] = _dot(s, w_ref[0].astype(BF16)) + b_ref[0]


def _modulation(cs, mod_w, mod_b, tn=1536):
    depth, d, n = mod_w.shape
    return pl.pallas_call(
        _mod_kernel,
        grid=(depth, n // tn),
        in_specs=[
            pl.BlockSpec((MOD_ROWS, d), lambda i, j: (0, 0)),
            pl.BlockSpec((1, d, tn), lambda i, j: (i, 0, j)),
            pl.BlockSpec((1, 1, tn), lambda i, j: (i, 0, j)),
        ],
        out_specs=pl.BlockSpec((1, MOD_ROWS, tn), lambda i, j: (i, 0, j)),
        out_shape=jax.ShapeDtypeStruct((depth, MOD_ROWS, n), F32),
        compiler_params=_params(2),
        name="modulation",
    )(cs, mod_w, mod_b.reshape(depth, 1, n))


def _mod_spec(ctx_tile, batch):
    def index(b, j):
        return (jnp.where(j == ctx_tile, batch, b), 0, 0)
    return index


def _norm_mod(x, ln, mod, shift, scale):
    y = _rms(x, ln)
    return (y * (1.0 + mod[scale:scale + 1]) + mod[shift:shift + 1]).astype(BF16)


def _qkv_kernel(x_ref, mod_ref, ln_ref, w_ref, wvt_ref, cos_ref, sin_ref, qn_ref, kn_ref,
                q_ref, k_ref, vt_ref, *, n_q, n_k, head_norm, q_scale):
    h = _norm_mod(x_ref[0], ln_ref[...], mod_ref[0], SH1, SC1)
    cos, sin = cos_ref[...], sin_ref[...]
    for head in range((n_q + n_k) // HEAD_W):
        is_q = head < n_q // HEAD_W
        if head % 2 == 0:
            pair = _dot(h, w_ref[:, head * HEAD_W:(head + 2) * HEAD_W])
        t = pair[:, (head % 2) * HEAD_W:(head % 2 + 1) * HEAD_W]
        if head_norm:
            t = _rms(t, qn_ref[...] if is_q else kn_ref[...])
        t = t * cos + pltpu.roll(t, HEAD_W // 2, 1) * sin
        if is_q:
            q_ref[0, :, head * HEAD_W:(head + 1) * HEAD_W] = (t * q_scale).astype(BF16)
        else:
            c0 = head * HEAD_W - n_q
            k_ref[0, :, c0:c0 + HEAD_W] = t.astype(BF16)
    vt_ref[0] = _dot_nt(wvt_ref[...], h).astype(BF16)


def _qkv_project(x, mod, ln, w_qk, w_vt, cosf, sinf, qn, kn, *, n_q, n_k, head_norm, q_scale):
    batch, t, d = x.shape
    n_v = w_vt.shape[0]
    n_tiles = t // ROW_TILE
    row = lambda b, j: (b, j, 0)
    kern = functools.partial(_qkv_kernel, n_q=n_q, n_k=n_k, head_norm=head_norm, q_scale=q_scale)
    return pl.pallas_call(
        kern,
        grid=(batch, n_tiles),
        in_specs=[
            pl.BlockSpec((1, ROW_TILE, d), row),
            pl.BlockSpec((1, 6, d), _mod_spec(n_tiles - 1, batch)),
            _resident((1, d)),
            _resident((d, n_q + n_k)),
            _resident((n_v, d)),
            pl.BlockSpec((ROW_TILE, HEAD_W), lambda b, j: (j, 0)),
            pl.BlockSpec((ROW_TILE, HEAD_W), lambda b, j: (j, 0)),
            _resident((1, HEAD_W)),
            _resident((1, HEAD_W)),
        ],
        out_specs=[
            pl.BlockSpec((1, ROW_TILE, n_q), row),
            pl.BlockSpec((1, ROW_TILE, n_k), row),
            pl.BlockSpec((1, n_v, ROW_TILE), lambda b, j: (b, 0, j)),
        ],
        out_shape=[
            jax.ShapeDtypeStruct((batch, t, n_q), BF16),
            jax.ShapeDtypeStruct((batch, t, n_k), BF16),
            jax.ShapeDtypeStruct((batch, n_v, t), BF16),
        ],
        compiler_params=_params(2),
        name="qkv_project",
    )(x, mod, ln, w_qk, w_vt, cosf, sinf, qn, kn)


def _chan_dft_kernel(x_ref, mod_ref, ln_ref, w_ref, o_ref):
    h = _norm_mod(x_ref[0], ln_ref[...], mod_ref[0], SH1, SC1)
    d = h.shape[1]
    gd = w_ref.shape[0]
    for g in range(d // gd):
        y = _dot(h[:, g * gd:(g + 1) * gd], w_ref[...])
        o_ref[0, :, g * gd:(g + 1) * gd] = y[:, :gd].astype(BF16)
        o_ref[0, :, d + g * gd:d + (g + 1) * gd] = y[:, gd:].astype(BF16)


def _channel_dft(x, mod, ln, w):
    batch, t, d = x.shape
    n_tiles = t // ROW_TILE
    row = lambda b, j: (b, j, 0)
    return pl.pallas_call(
        _chan_dft_kernel,
        grid=(batch, n_tiles),
        in_specs=[
            pl.BlockSpec((1, ROW_TILE, d), row),
            pl.BlockSpec((1, 6, d), _mod_spec(n_tiles - 1, batch)),
            _resident((1, d)),
            _resident(w.shape),
        ],
        out_specs=pl.BlockSpec((1, ROW_TILE, 2 * d), row),
        out_shape=jax.ShapeDtypeStruct((batch, t, 2 * d), BF16),
        compiler_params=_params(2),
        name="channel_dft",
    )(x, mod, ln, w)


def _score_chunk(k_ref, s_scr, q, kvh, buf, c, k_lo, n_chunks, m8):
    tq = q.shape[0]
    rows = slice(k_lo + c * KEY_CHUNK, k_lo + (c + 1) * KEY_CHUNK)
    s = _dot_nt(k_ref[0, rows, kvh * HEAD_W:(kvh + 1) * HEAD_W], q)
    lo = (buf * n_chunks + c) * KEY_CHUNK
    s_scr[lo:lo + KEY_CHUNK, :] = s
    return jnp.maximum(m8, jnp.max(s.reshape(KEY_CHUNK // SUBLANES, SUBLANES, tq), axis=0))


def _softmax_pv_streams(score_list, weigh_list, offset, k_ref, vt_ref, s_scr, m8_carried, k_lo, n_keys,
                        on_done):
    tq = score_list[0][0].shape[0]
    n_chunks = n_keys // KEY_CHUNK
    ones_rows = jnp.ones((ONES_ROWS, KEY_CHUNK), BF16)

    def weigh(kvh, buf, c, mx, acc):
        lo = (buf * n_chunks + c) * KEY_CHUNK
        e = jnp.exp2(s_scr[lo:lo + KEY_CHUNK, :] - mx)
        cols = slice(k_lo + c * KEY_CHUNK, k_lo + (c + 1) * KEY_CHUNK)
        v_ones = jnp.concatenate([vt_ref[0, kvh * HEAD_W:(kvh + 1) * HEAD_W, cols], ones_rows], axis=0)
        return acc + _dot(v_ones, e.astype(BF16))

    m8 = [None] * len(score_list)
    mx = acc = None
    n_steps = max(len(score_list) * n_chunks, offset + len(weigh_list) * n_chunks)
    for g in range(n_steps):
        if g < len(score_list) * n_chunks:
            j, c = divmod(g, n_chunks)
            q, kvh, buf = score_list[j]
            if c == 0:
                m8[j] = jnp.full((SUBLANES, tq), -jnp.inf, F32)
            m8[j] = _score_chunk(k_ref, s_scr, q, kvh, buf, c, k_lo, n_chunks, m8[j])
        ge = g - offset
        if 0 <= ge < len(weigh_list) * n_chunks:
            p, c = divmod(ge, n_chunks)
            kvh, buf, max_src = weigh_list[p]
            if c == 0:
                src = m8_carried if max_src is None else m8[max_src]
                mx = jnp.max(src, axis=0, keepdims=True)
                acc = jnp.zeros((HEAD_W + ONES_ROWS, tq), F32)
            acc = weigh(kvh, buf, c, mx, acc)
            if c == n_chunks - 1:
                on_done(p, acc[:HEAD_W] * (1.0 / acc[HEAD_W:HEAD_W + 1]))
    return m8


def _head_problems(q_ref, diff, heads_per_step, kv_heads_per_step, n_heads=None):
    problems = []
    for hh in range(heads_per_step if n_heads is None else n_heads):
        q = q_ref[0, :, hh * HEAD_W:(hh + 1) * HEAD_W]
        kvh = hh * kv_heads_per_step // heads_per_step
        if diff:
            lane = lax.broadcasted_iota(jnp.int32, q.shape, 1)
            first = (lane % (HEAD_W // 2)) < (HEAD_W // 4)
            zero = jnp.zeros_like(q)
            problems += [(jnp.where(first, q, zero), kvh), (jnp.where(first, zero, q), kvh)]
        else:
            problems.append((q, kvh))
    return problems


def _head_writer(o_ref, lam_ref, subln_ref, diff, lambda_init):
    first_comp = {}

    def on_done(p, o_t):
        if not diff:
            o_ref[0, :, p * HEAD_W:(p + 1) * HEAD_W] = o_t.T.astype(BF16)
            return
        hh = p // 2
        if p % 2 == 0:
            first_comp[hh] = o_t
            return
        lv = lam_ref[...]
        lam = (jnp.exp(jnp.sum(lv[0:1] * lv[1:2], axis=-1, keepdims=True))
               - jnp.exp(jnp.sum(lv[2:3] * lv[3:4], axis=-1, keepdims=True)) + lambda_init)
        o = (first_comp.pop(hh) - lam * o_t).T
        o = _rms(o, subln_ref[...]) * (1.0 - lambda_init)
        o_ref[0, :, hh * HEAD_W:(hh + 1) * HEAD_W] = o.astype(BF16)

    return on_done


def _attn_latent_kernel(lam_ref, subln_ref, q_ref, qn_ref, k_ref, vt_ref, o_ref, s_scr, m8_scr,
                        *, diff, lambda_init, heads_per_step, kv_heads_per_step, n_keys):
    n_chunks = n_keys // KEY_CHUNK
    cur = _head_problems(q_ref, diff, heads_per_step, kv_heads_per_step)
    nxt_q, nxt_kvh = _head_problems(qn_ref, diff, heads_per_step, kv_heads_per_step, n_heads=1)[0]
    n_prob = len(cur)
    assert n_prob % 2 == 0

    @pl.when(pl.program_id(2) == 0)
    def _():
        q, kvh = cur[0]
        m8 = jnp.full((SUBLANES, q.shape[0]), -jnp.inf, F32)
        for c in range(n_chunks):
            m8 = _score_chunk(k_ref, s_scr, q, kvh, 0, c, 0, n_chunks, m8)
        m8_scr[...] = m8

    score_list = [(q, kvh, p % 2) for p, (q, kvh) in enumerate(cur) if p > 0] + [(nxt_q, nxt_kvh, 0)]
    weigh_list = [(kvh, p % 2, None if p == 0 else p - 1) for p, (_, kvh) in enumerate(cur)]
    m8 = _softmax_pv_streams(
        score_list, weigh_list, min(SCORE_LEAD, n_chunks - 1), k_ref, vt_ref, s_scr, m8_scr[...], 0, n_keys,
        _head_writer(o_ref, lam_ref, subln_ref, diff, lambda_init))
    m8_scr[...] = m8[-1]


def _attn_ctx_kernel(lam_ref, subln_ref, q_ref, k_ref, vt_ref, o_ref, s_scr,
                     *, diff, lambda_init, heads_per_step, kv_heads_per_step, n_keys):
    n_chunks = n_keys // KEY_CHUNK
    cur = _head_problems(q_ref, diff, heads_per_step, kv_heads_per_step)
    score_list = [(q, kvh, p % 2) for p, (q, kvh) in enumerate(cur)]
    weigh_list = [(kvh, p % 2, p) for p, (_, kvh) in enumerate(cur)]
    _softmax_pv_streams(
        score_list, weigh_list, n_chunks + min(SCORE_LEAD, n_chunks - 1), k_ref, vt_ref, s_scr, None, 0, n_keys,
        _head_writer(o_ref, lam_ref, subln_ref, diff, lambda_init))


def _attention(q, k, vt, lam, subln, *, diff, lambda_init, n_heads, heads_per_step, kv_heads_per_step,
               n_lat, need_ctx):
    batch, t, _ = q.shape
    n_ctx = t - n_lat
    qw = heads_per_step * HEAD_W
    kw = kv_heads_per_step * HEAD_W
    n_groups = n_heads // heads_per_step
    n_lat_tiles = n_lat // ROW_TILE
    ctx_tile = n_lat // n_ctx
    static = dict(diff=diff, lambda_init=lambda_init, heads_per_step=heads_per_step,
                  kv_heads_per_step=kv_heads_per_step)
    name = "diff_attention" if diff else "gqa_attention"
    o_lat = pl.pallas_call(
        functools.partial(_attn_latent_kernel, n_keys=t, **static),
        grid=(batch, n_groups, n_lat_tiles),
        in_specs=[
            _resident(lam.shape),
            _resident(subln.shape),
            pl.BlockSpec((1, ROW_TILE, qw), lambda b, h, i: (b, i, h)),
            pl.BlockSpec((1, ROW_TILE, qw), lambda b, h, i: (b, jnp.minimum(i + 1, n_lat_tiles - 1), h)),
            pl.BlockSpec((1, t, kw), lambda b, h, i: (b, 0, h)),
            pl.BlockSpec((1, kw, t), lambda b, h, i: (b, h, 0)),
        ],
        out_specs=pl.BlockSpec((1, ROW_TILE, qw), lambda b, h, i: (b, i, h)),
        out_shape=jax.ShapeDtypeStruct((batch, n_lat, n_heads * HEAD_W), BF16),
        scratch_shapes=[pltpu.VMEM((2 * t, ROW_TILE), F32), pltpu.VMEM((SUBLANES, ROW_TILE), F32)],
        compiler_params=_params(3),
        name=name,
    )(lam, subln, q, q, k, vt)
    if not need_ctx:
        return o_lat, None
    o_ctx = pl.pallas_call(
        functools.partial(_attn_ctx_kernel, n_keys=n_ctx, **static),
        grid=(batch, n_groups),
        in_specs=[
            _resident(lam.shape),
            _resident(subln.shape),
            pl.BlockSpec((1, n_ctx, qw), lambda b, h: (b, ctx_tile, h)),
            pl.BlockSpec((1, n_ctx, kw), lambda b, h: (b, ctx_tile, h)),
            pl.BlockSpec((1, kw, n_ctx), lambda b, h: (b, h, ctx_tile)),
        ],
        out_specs=pl.BlockSpec((1, n_ctx, qw), lambda b, h: (b, 0, h)),
        out_shape=jax.ShapeDtypeStruct((batch, n_ctx, n_heads * HEAD_W), BF16),
        scratch_shapes=[pltpu.VMEM((2 * n_ctx, n_ctx), F32)],
        compiler_params=_params(2),
        name=name + "_ctx",
    )(lam, subln, q, k, vt)
    return o_lat, o_ctx


def _dft_kernel(cl_ref, sl_ref, cc_ref, sc_ref, xl_c_ref, xl_s_ref, xc_c_ref, xc_s_ref, o_ref,
                *, ctx_tile):
    i = pl.program_id(1)

    @pl.when(i != ctx_tile)
    def _():
        o_ref[0] = (_dot(cl_ref[...], xl_c_ref[0]) + _dot(sl_ref[...], xl_s_ref[0])).astype(BF16)

    @pl.when(i == ctx_tile)
    def _():
        o_ref[0] = (_dot(cc_ref[...], xc_c_ref[0]) + _dot(sc_ref[...], xc_s_ref[0])).astype(BF16)


def _position_dft(xcs, cl, sl, cc, sc, n_lat, n_tiles):
    batch, t, d2 = xcs.shape
    d = d2 // 2
    n_ctx = t - n_lat
    ctx_tile = n_lat // ROW_TILE
    last_lat = ctx_tile - 1
    lat_rows = lambda b, i: (jnp.minimum(i, last_lat), 0)
    return pl.pallas_call(
        functools.partial(_dft_kernel, ctx_tile=ctx_tile),
        grid=(batch, n_tiles),
        in_specs=[
            pl.BlockSpec((ROW_TILE, n_lat), lat_rows),
            pl.BlockSpec((ROW_TILE, n_lat), lat_rows),
            _resident((n_ctx, n_ctx)),
            _resident((n_ctx, n_ctx)),
            pl.BlockSpec((1, n_lat, d), lambda b, i: (b, 0, 0)),
            pl.BlockSpec((1, n_lat, d), lambda b, i: (b, 0, 1)),
            pl.BlockSpec((1, n_ctx, d), lambda b, i: (b, ctx_tile, 0)),
            pl.BlockSpec((1, n_ctx, d), lambda b, i: (b, ctx_tile, 1)),
        ],
        out_specs=pl.BlockSpec((1, ROW_TILE, d), lambda b, i: (b, i, 0)),
        out_shape=jax.ShapeDtypeStruct((batch, n_tiles * ROW_TILE, d), BF16),
        compiler_params=_params(2),
        name="position_dft",
    )(cl, sl, cc, sc, xcs, xcs, xcs, xcs)


def _out_ffn_kernel(a_ref, actx_ref, wo_ref, bo_ref, x_ref, mod_ref, ln_ref, wgu_ref, wd_ref, fn_ref, o_ref,
                    a_scr, *, d_ff, final_norm, ctx_tile):
    mod = mod_ref[0]
    a = jnp.where(pl.program_id(1) == ctx_tile, actx_ref[0], a_ref[0])
    x = x_ref[0] + mod[G1:G1 + 1] * (_dot(a, wo_ref[...]) + bo_ref[...])
    h = _norm_mod(x, ln_ref[...], mod, SH2, SC2)
    for c in range(d_ff // FF_CHUNK):
        lo = c * FF_CHUNK
        g = _dot(h, wgu_ref[:, lo:lo + FF_CHUNK])
        u = _dot(h, wgu_ref[:, d_ff + lo:d_ff + lo + FF_CHUNK])
        a_scr[:, lo:lo + FF_CHUNK] = (_silu(g) * u).astype(BF16)
    y = x + mod[G2:G2 + 1] * _dot(a_scr[...], wd_ref[...])
    if final_norm:
        y = _rms(y, fn_ref[...])
    o_ref[0] = y


def _out_ffn(a, a_ctx, a_ctx_block, w_o, b_o, x, mod, ln, w_gu, w_down, final_g, n_tiles, ctx_tile, final_norm):
    batch, _, d = x.shape
    k = a.shape[2]
    d_ff = w_down.shape[0]
    row = lambda b, j: (b, j, 0)
    return pl.pallas_call(
        functools.partial(_out_ffn_kernel, d_ff=d_ff, final_norm=final_norm, ctx_tile=ctx_tile),
        grid=(batch, n_tiles),
        in_specs=[
            pl.BlockSpec((1, ROW_TILE, k), lambda b, j: (b, jnp.minimum(j, ctx_tile - 1), 0)),
            pl.BlockSpec((1, ROW_TILE, k), lambda b, j: (b, a_ctx_block, 0)),
            _resident((k, d)),
            _resident((1, d)),
            pl.BlockSpec((1, ROW_TILE, d), row),
            pl.BlockSpec((1, 6, d), _mod_spec(ctx_tile, batch)),
            _resident((1, d)),
            _resident((d, 2 * d_ff)),
            _resident((d_ff, d)),
            _resident((1, d)),
        ],
        out_specs=pl.BlockSpec((1, ROW_TILE, d), row),
        out_shape=jax.ShapeDtypeStruct((batch, n_tiles * ROW_TILE, d), F32),
        scratch_shapes=[pltpu.VMEM((ROW_TILE, d_ff), BF16)],
        compiler_params=_params(2),
        name="out_ffn",
    )(a, a_ctx, w_o, b_o, x, mod, ln, w_gu, w_down, final_g)


def _half_split_perm(n_comp, comp_dim):
    src = np.zeros(HEAD_W, np.int32)
    for m in range(n_comp):
        for i in range(comp_dim // 2):
            for p in range(2):
                src[p * (HEAD_W // 2) + m * (comp_dim // 2) + i] = m * comp_dim + 2 * i + p
    return src


def _rope_lane_tables(seq, n_ctx, head_dim):
    axis_dim = head_dim // 2
    rows = jnp.arange(seq, dtype=jnp.int32) // GRID_W
    cols = jnp.arange(seq, dtype=jnp.int32) % GRID_W
    inv_freq = ROPE_THETA ** (-jnp.arange(0, axis_dim, 2, dtype=F32) / axis_dim)
    ang = jnp.concatenate([rows[:, None].astype(F32) * inv_freq,
                           cols[:, None].astype(F32) * inv_freq], axis=-1)
    reps = (HEAD_W // 2) // (head_dim // 2)
    cos = jnp.tile(jnp.cos(ang), (1, 2 * reps))
    sin = jnp.tile(jnp.sin(ang), (1, reps))
    sin = jnp.concatenate([-sin, sin], axis=-1)
    cos = jnp.concatenate([cos, jnp.ones((n_ctx, HEAD_W), F32)], axis=0)
    sin = jnp.concatenate([sin, jnp.zeros((n_ctx, HEAD_W), F32)], axis=0)
    return cos, sin


def _dft_tables(n, scale):
    r = 1
    while r * r < n:
        r *= 2
    q = n // r
    k = jnp.arange(n, dtype=jnp.int32)[:, None]
    a_idx = (k * jnp.arange(q, dtype=jnp.int32)[None, :] * r) % n
    b_idx = (k * jnp.arange(r, dtype=jnp.int32)[None, :]) % n
    w = 2.0 * math.pi / n
    ca, sa = jnp.cos(a_idx.astype(F32) * w), jnp.sin(a_idx.astype(F32) * w)
    cb, sb = jnp.cos(b_idx.astype(F32) * w) * scale, jnp.sin(b_idx.astype(F32) * w) * scale
    cos = ca[:, :, None] * cb[:, None, :] - sa[:, :, None] * sb[:, None, :]
    sin = sa[:, :, None] * cb[:, None, :] + ca[:, :, None] * sb[:, None, :]
    return cos.reshape(n, n).astype(BF16), (-sin).reshape(n, n).astype(BF16)


def _channel_dft_weight(gd):
    idx = (np.arange(gd)[:, None] * np.arange(gd)[None, :]) % gd
    ang = 2.0 * np.pi * idx / gd
    w = np.concatenate([np.cos(ang), np.sin(ang)], axis=1) / math.sqrt(gd)
    return jnp.asarray(w, dtype=BF16)


def kernel(x, c, ctx, c_ctx, mod_w, mod_b, ln_mix, ln_ffn, ffn_w_gu, ffn_w_down, a_w_qkv, a_lam, a_subln, a_w_o, b_w_qkv, b_q_norm, b_k_norm, b_w_o, c_w_o, c_b_o, final_norm):
    batch, seq, d = x.shape
    n_ctx = ctx.shape[1]
    depth = mod_w.shape[0]
    t = seq + n_ctx
    assert seq % ROW_TILE == 0 and n_ctx == ROW_TILE and batch < MOD_ROWS
    n_tiles_all = t // ROW_TILE
    n_tiles_lat = seq // ROW_TILE
    ctx_tile = n_tiles_all - 1

    cs = jnp.concatenate([c, c_ctx[None, :], jnp.zeros((MOD_ROWS - batch - 1, d), F32)], axis=0)
    mod_all = _modulation(cs, mod_w, mod_b).reshape(depth, MOD_ROWS, 6, d)

    xs = jnp.concatenate([x, ctx], axis=1)

    perm_a = _half_split_perm(2, DIFF_HEAD_DIM)
    perm_b = _half_split_perm(1, GQA_HEAD_DIM)
    rope_a = _rope_lane_tables(seq, n_ctx, DIFF_HEAD_DIM)
    rope_b = _rope_lane_tables(seq, n_ctx, GQA_HEAD_DIM)
    ones_head = jnp.ones((1, HEAD_W), F32)
    no_bias = jnp.zeros((1, d), F32)

    for i in range(depth):
        last = i == depth - 1
        kind, j = i % N_MIXERS, i // N_MIXERS
        mod = mod_all[i]
        ln1 = ln_mix[i][None, :]
        n_out_tiles = n_tiles_lat if last else n_tiles_all

        if kind == 0:
            n_q = DIFF_HEADS * HEAD_W
            cols = (np.arange(2 * DIFF_HEADS)[:, None] * HEAD_W + perm_a[None, :]).reshape(-1)
            w = a_w_qkv[j]
            w_qk = w[:, cols].astype(BF16)
            w_vt = w[:, 2 * n_q:].T.astype(BF16)
            q, k, vt = _qkv_project(
                xs, mod, ln1, w_qk, w_vt, rope_a[0], rope_a[1], ones_head, ones_head,
                n_q=n_q, n_k=n_q, head_norm=False, q_scale=DIFF_HEAD_DIM ** -0.5 * LOG2E)
            mixed, mixed_ctx = _attention(
                q, k, vt, a_lam[j], a_subln[j][None, :], diff=True, lambda_init=_lambda_init(i),
                n_heads=DIFF_HEADS, heads_per_step=4, kv_heads_per_step=4, n_lat=seq, need_ctx=not last)
            mixed_ctx_block, w_o, b_o = 0, a_w_o[j], no_bias
        elif kind == 1:
            n_q = GQA_HEADS * HEAD_W
            n_k = GQA_KV_HEADS * HEAD_W
            cols = (np.arange(GQA_HEADS + GQA_KV_HEADS)[:, None] * HEAD_W + perm_b[None, :]).reshape(-1)
            w = b_w_qkv[j]
            w_qk = w[:, cols].astype(BF16)
            w_vt = w[:, n_q + n_k:].T.astype(BF16)
            q, k, vt = _qkv_project(
                xs, mod, ln1, w_qk, w_vt, rope_b[0], rope_b[1],
                b_q_norm[j][perm_b][None, :], b_k_norm[j][perm_b][None, :],
                n_q=n_q, n_k=n_k, head_norm=True, q_scale=GQA_HEAD_DIM ** -0.5 * LOG2E)
            mixed, mixed_ctx = _attention(
                q, k, vt, jnp.zeros((4, DIFF_HEAD_DIM), F32), ones_head, diff=False, lambda_init=0.0,
                n_heads=GQA_HEADS, heads_per_step=GQA_HEADS, kv_heads_per_step=GQA_KV_HEADS,
                n_lat=seq, need_ctx=not last)
            mixed_ctx_block, w_o, b_o = 0, b_w_o[j], no_bias
        else:
            xcs = _channel_dft(xs, mod, ln1, _channel_dft_weight(d // FNET_GROUPS))
            cl, sl = _dft_tables(seq, seq ** -0.5)
            cc, sc = _dft_tables(n_ctx, n_ctx ** -0.5)
            f = _position_dft(xcs, cl, sl, cc, sc, seq, n_out_tiles)
            mixed, mixed_ctx, mixed_ctx_block = f, f, ctx_tile
            w_o, b_o = c_w_o[j], c_b_o[j][None, :]

        if last:
            mixed_ctx, mixed_ctx_block = mixed, 0
        xs = _out_ffn(mixed, mixed_ctx, mixed_ctx_block, w_o.astype(BF16), b_o, xs, mod, ln_ffn[i][None, :],
                      ffn_w_gu[i].astype(BF16), ffn_w_down[i].astype(BF16), final_norm[None, :],
                      n_out_tiles, ctx_tile, final_norm=last)
    return xs
```

```python
import functools
import math

import numpy as np
import jax
import jax.numpy as jnp
from jax import lax
from jax.experimental import pallas as pl
from jax.experimental.pallas import tpu as pltpu

GRID_W = 64
ROPE_THETA = 10000.0
NORM_EPS = 1e-6
DIFF_HEADS = 8
DIFF_HEAD_DIM = 64
GQA_HEADS = 8
GQA_KV_HEADS = 2
GQA_HEAD_DIM = 128
FNET_GROUPS = 4
N_MIXERS = 3

LANES = 128
SUBLANES = 8
VMEM_LIMIT = 56 * 1024 * 1024

ROW_TILE = 256
HEAD_W = 128
KEY_CHUNK = 256
SCORE_LEAD = 2
FF_CHUNK = 256
ONES_ROWS = 16
MOD_ROWS = 16
LOG2E = math.log2(math.e)

BF16 = jnp.bfloat16
F32 = jnp.float32

SH1, SC1, G1, SH2, SC2, G2 = range(6)


def _lambda_init(layer_idx):
    return 0.8 - 0.6 * float(np.exp(-0.3 * layer_idx))


def _params(n_grid, flags=None):
    return pltpu.CompilerParams(
        dimension_semantics=("arbitrary",) * n_grid, vmem_limit_bytes=VMEM_LIMIT, flags=flags)


def _resident(shape):
    nd = len(shape)
    return pl.BlockSpec(shape, lambda *_: (0,) * nd, pipeline_mode=pl.Buffered(1))


def _dot(a, b):
    return jnp.dot(a, b, preferred_element_type=F32)


def _dot_nt(a, b):
    return lax.dot_general(a, b, (((1,), (1,)), ((), ())), preferred_element_type=F32)


def _rms(x, g):
    return x * lax.rsqrt(jnp.mean(x * x, axis=-1, keepdims=True) + NORM_EPS) * g


def _silu(x):
    return x / (1.0 + jnp.exp(-x))


def _mod_kernel(cs_ref, w_ref, b_ref, o_ref):
    s = _silu(cs_ref[...]).astype(BF16)
    o_ref[0] = _dot(s, w_ref[0].astype(BF16)) + b_ref[0]


def _modulation(cs, mod_w, mod_b, tn=1536):
    depth, d, n = mod_w.shape
    return pl.pallas_call(
        _mod_kernel,
        grid=(depth, n // tn),
        in_specs=[
            pl.BlockSpec((MOD_ROWS, d), lambda i, j: (0, 0)),
            pl.BlockSpec((1, d, tn), lambda i, j: (i, 0, j)),
            pl.BlockSpec((1, 1, tn), lambda i, j: (i, 0, j)),
        ],
        out_specs=pl.BlockSpec((1, MOD_ROWS, tn), lambda i, j: (i, 0, j)),
        out_shape=jax.ShapeDtypeStruct((depth, MOD_ROWS, n), F32),
        compiler_params=_params(2),
        name="modulation",
    )(cs, mod_w, mod_b.reshape(depth, 1, n))


def _mod_spec(ctx_tile, batch):
    def index(b, j):
        return (jnp.where(j == ctx_tile, batch, b), 0, 0)
    return index


def _norm_mod(x, ln, mod, shift, scale):
    y = _rms(x, ln)
    return (y * (1.0 + mod[scale:scale + 1]) + mod[shift:shift + 1]).astype(BF16)


def _qkv_kernel(x_ref, mod_ref, ln_ref, w_ref, wvt_ref, cos_ref, sin_ref, qn_ref, kn_ref,
                q_ref, k_ref, vt_ref, *, n_q, n_k, head_norm, q_scale):
    h = _norm_mod(x_ref[0], ln_ref[...], mod_ref[0], SH1, SC1)
    cos, sin = cos_ref[...], sin_ref[...]
    for head in range((n_q + n_k) // HEAD_W):
        is_q = head < n_q // HEAD_W
        if head % 2 == 0:
            pair = _dot(h, w_ref[:, head * HEAD_W:(head + 2) * HEAD_W])
        t = pair[:, (head % 2) * HEAD_W:(head % 2 + 1) * HEAD_W]
        if head_norm:
            t = _rms(t, qn_ref[...] if is_q else kn_ref[...])
        t = t * cos + pltpu.roll(t, HEAD_W // 2, 1) * sin
        if is_q:
            q_ref[0, :, head * HEAD_W:(head + 1) * HEAD_W] = (t * q_scale).astype(BF16)
        else:
            c0 = head * HEAD_W - n_q
            k_ref[0, :, c0:c0 + HEAD_W] = t.astype(BF16)
    vt_ref[0] = _dot_nt(wvt_ref[...], h).astype(BF16)


def _qkv_project(x, mod, ln, w_qk, w_vt, cosf, sinf, qn, kn, *, n_q, n_k, head_norm, q_scale):
    batch, t, d = x.shape
    n_v = w_vt.shape[0]
    n_tiles = t // ROW_TILE
    row = lambda b, j: (b, j, 0)
    kern = functools.partial(_qkv_kernel, n_q=n_q, n_k=n_k, head_norm=head_norm, q_scale=q_scale)
    return pl.pallas_call(
        kern,
        grid=(batch, n_tiles),
        in_specs=[
            pl.BlockSpec((1, ROW_TILE, d), row),
            pl.BlockSpec((1, 6, d), _mod_spec(n_tiles - 1, batch)),
            _resident((1, d)),
            _resident((d, n_q + n_k)),
            _resident((n_v, d)),
            pl.BlockSpec((ROW_TILE, HEAD_W), lambda b, j: (j, 0)),
            pl.BlockSpec((ROW_TILE, HEAD_W), lambda b, j: (j, 0)),
            _resident((1, HEAD_W)),
            _resident((1, HEAD_W)),
        ],
        out_specs=[
            pl.BlockSpec((1, ROW_TILE, n_q), row),
            pl.BlockSpec((1, ROW_TILE, n_k), row),
            pl.BlockSpec((1, n_v, ROW_TILE), lambda b, j: (b, 0, j)),
        ],
        out_shape=[
            jax.ShapeDtypeStruct((batch, t, n_q), BF16),
            jax.ShapeDtypeStruct((batch, t, n_k), BF16),
            jax.ShapeDtypeStruct((batch, n_v, t), BF16),
        ],
        compiler_params=_params(2),
        name="qkv_project",
    )(x, mod, ln, w_qk, w_vt, cosf, sinf, qn, kn)


def _chan_dft_kernel(x_ref, mod_ref, ln_ref, w_ref, oc_ref, os_ref):
    h = _norm_mod(x_ref[0], ln_ref[...], mod_ref[0], SH1, SC1)
    d = h.shape[1]
    gd = w_ref.shape[0]
    for g in range(d // gd):
        y = _dot(h[:, g * gd:(g + 1) * gd], w_ref[...])
        oc_ref[0, :, g * gd:(g + 1) * gd] = y[:, :gd].astype(BF16)
        os_ref[0, :, g * gd:(g + 1) * gd] = y[:, gd:].astype(BF16)


def _channel_dft(x, mod, ln, w):
    batch, t, d = x.shape
    n_tiles = t // ROW_TILE
    row = lambda b, j: (b, j, 0)
    return pl.pallas_call(
        _chan_dft_kernel,
        grid=(batch, n_tiles),
        in_specs=[
            pl.BlockSpec((1, ROW_TILE, d), row),
            pl.BlockSpec((1, 6, d), _mod_spec(n_tiles - 1, batch)),
            _resident((1, d)),
            _resident(w.shape),
        ],
        out_specs=[pl.BlockSpec((1, ROW_TILE, d), row)] * 2,
        out_shape=[jax.ShapeDtypeStruct((batch, t, d), BF16)] * 2,
        compiler_params=_params(2),
        name="channel_dft",
    )(x, mod, ln, w)


def _score_chunk(k_ref, s_scr, q, kvh, buf, c, k_lo, n_chunks, m8):
    tq = q.shape[0]
    rows = slice(k_lo + c * KEY_CHUNK, k_lo + (c + 1) * KEY_CHUNK)
    s = _dot_nt(k_ref[0, rows, kvh * HEAD_W:(kvh + 1) * HEAD_W], q)
    lo = (buf * n_chunks + c) * KEY_CHUNK
    s_scr[lo:lo + KEY_CHUNK, :] = s
    return jnp.maximum(m8, jnp.max(s.reshape(KEY_CHUNK // SUBLANES, SUBLANES, tq), axis=0))


def _softmax_pv_streams(score_list, weigh_list, offset, k_ref, vt_ref, s_scr, m8_carried, k_lo, n_keys,
                        on_done):
    tq = score_list[0][0].shape[0]
    n_chunks = n_keys // KEY_CHUNK
    ones_rows = jnp.ones((ONES_ROWS, KEY_CHUNK), BF16)

    def weigh(kvh, buf, c, mx, acc):
        lo = (buf * n_chunks + c) * KEY_CHUNK
        e = jnp.exp2(s_scr[lo:lo + KEY_CHUNK, :] - mx)
        cols = slice(k_lo + c * KEY_CHUNK, k_lo + (c + 1) * KEY_CHUNK)
        v_ones = jnp.concatenate([vt_ref[0, kvh * HEAD_W:(kvh + 1) * HEAD_W, cols], ones_rows], axis=0)
        return acc + _dot(v_ones, e.astype(BF16))

    m8 = [None] * len(score_list)
    mx = acc = None
    n_steps = max(len(score_list) * n_chunks, offset + len(weigh_list) * n_chunks)
    for g in range(n_steps):
        if g < len(score_list) * n_chunks:
            j, c = divmod(g, n_chunks)
            q, kvh, buf = score_list[j]
            if c == 0:
                m8[j] = jnp.full((SUBLANES, tq), -jnp.inf, F32)
            m8[j] = _score_chunk(k_ref, s_scr, q, kvh, buf, c, k_lo, n_chunks, m8[j])
        ge = g - offset
        if 0 <= ge < len(weigh_list) * n_chunks:
            p, c = divmod(ge, n_chunks)
            kvh, buf, max_src = weigh_list[p]
            if c == 0:
                src = m8_carried if max_src is None else m8[max_src]
                mx = jnp.max(src, axis=0, keepdims=True)
                acc = jnp.zeros((HEAD_W + ONES_ROWS, tq), F32)
            acc = weigh(kvh, buf, c, mx, acc)
            if c == n_chunks - 1:
                on_done(p, acc[:HEAD_W] * (1.0 / acc[HEAD_W:HEAD_W + 1]))
    return m8


def _head_problems(q_ref, diff, heads_per_step, kv_heads_per_step, n_heads=None):
    problems = []
    for hh in range(heads_per_step if n_heads is None else n_heads):
        q = q_ref[0, :, hh * HEAD_W:(hh + 1) * HEAD_W]
        kvh = hh * kv_heads_per_step // heads_per_step
        if diff:
            lane = lax.broadcasted_iota(jnp.int32, q.shape, 1)
            first = (lane % (HEAD_W // 2)) < (HEAD_W // 4)
            zero = jnp.zeros_like(q)
            problems += [(jnp.where(first, q, zero), kvh), (jnp.where(first, zero, q), kvh)]
        else:
            problems.append((q, kvh))
    return problems


def _head_writer(o_ref, lam_ref, subln_ref, diff, lambda_init):
    first_comp = {}

    def on_done(p, o_t):
        if not diff:
            o_ref[0, :, p * HEAD_W:(p + 1) * HEAD_W] = o_t.T.astype(BF16)
            return
        hh = p // 2
        if p % 2 == 0:
            first_comp[hh] = o_t
            return
        lv = lam_ref[...]
        lam = (jnp.exp(jnp.sum(lv[0:1] * lv[1:2], axis=-1, keepdims=True))
               - jnp.exp(jnp.sum(lv[2:3] * lv[3:4], axis=-1, keepdims=True)) + lambda_init)
        o = (first_comp.pop(hh) - lam * o_t).T
        o = _rms(o, subln_ref[...]) * (1.0 - lambda_init)
        o_ref[0, :, hh * HEAD_W:(hh + 1) * HEAD_W] = o.astype(BF16)

    return on_done


def _attn_latent_kernel(lam_ref, subln_ref, q_ref, qn_ref, k_ref, vt_ref, o_ref, s_scr, m8_scr,
                        *, diff, lambda_init, heads_per_step, kv_heads_per_step, n_keys):
    n_chunks = n_keys // KEY_CHUNK
    cur = _head_problems(q_ref, diff, heads_per_step, kv_heads_per_step)
    nxt_q, nxt_kvh = _head_problems(qn_ref, diff, heads_per_step, kv_heads_per_step, n_heads=1)[0]
    n_prob = len(cur)
    assert n_prob % 2 == 0

    @pl.when(pl.program_id(2) == 0)
    def _():
        q, kvh = cur[0]
        m8 = jnp.full((SUBLANES, q.shape[0]), -jnp.inf, F32)
        for c in range(n_chunks):
            m8 = _score_chunk(k_ref, s_scr, q, kvh, 0, c, 0, n_chunks, m8)
        m8_scr[...] = m8

    score_list = [(q, kvh, p % 2) for p, (q, kvh) in enumerate(cur) if p > 0] + [(nxt_q, nxt_kvh, 0)]
    weigh_list = [(kvh, p % 2, None if p == 0 else p - 1) for p, (_, kvh) in enumerate(cur)]
    m8 = _softmax_pv_streams(
        score_list, weigh_list, min(SCORE_LEAD, n_chunks - 1), k_ref, vt_ref, s_scr, m8_scr[...], 0, n_keys,
        _head_writer(o_ref, lam_ref, subln_ref, diff, lambda_init))
    m8_scr[...] = m8[-1]


def _attn_ctx_kernel(lam_ref, subln_ref, q_ref, k_ref, vt_ref, o_ref, s_scr,
                     *, diff, lambda_init, heads_per_step, kv_heads_per_step, n_keys):
    n_chunks = n_keys // KEY_CHUNK
    cur = _head_problems(q_ref, diff, heads_per_step, kv_heads_per_step)
    score_list = [(q, kvh, p % 2) for p, (q, kvh) in enumerate(cur)]
    weigh_list = [(kvh, p % 2, p) for p, (_, kvh) in enumerate(cur)]
    _softmax_pv_streams(
        score_list, weigh_list, n_chunks + min(SCORE_LEAD, n_chunks - 1), k_ref, vt_ref, s_scr, None, 0, n_keys,
        _head_writer(o_ref, lam_ref, subln_ref, diff, lambda_init))


def _attention(q, k, vt, lam, subln, *, diff, lambda_init, n_heads, heads_per_step, kv_heads_per_step,
               n_lat, need_ctx):
    batch, t, _ = q.shape
    n_ctx = t - n_lat
    qw = heads_per_step * HEAD_W
    kw = kv_heads_per_step * HEAD_W
    n_groups = n_heads // heads_per_step
    n_lat_tiles = n_lat // ROW_TILE
    ctx_tile = n_lat // n_ctx
    static = dict(diff=diff, lambda_init=lambda_init, heads_per_step=heads_per_step,
                  kv_heads_per_step=kv_heads_per_step)
    name = "diff_attention" if diff else "gqa_attention"
    o_lat = pl.pallas_call(
        functools.partial(_attn_latent_kernel, n_keys=t, **static),
        grid=(batch, n_groups, n_lat_tiles),
        in_specs=[
            _resident(lam.shape),
            _resident(subln.shape),
            pl.BlockSpec((1, ROW_TILE, qw), lambda b, h, i: (b, i, h)),
            pl.BlockSpec((1, ROW_TILE, qw), lambda b, h, i: (b, jnp.minimum(i + 1, n_lat_tiles - 1), h)),
            pl.BlockSpec((1, t, kw), lambda b, h, i: (b, 0, h)),
            pl.BlockSpec((1, kw, t), lambda b, h, i: (b, h, 0)),
        ],
        out_specs=pl.BlockSpec((1, ROW_TILE, qw), lambda b, h, i: (b, i, h)),
        out_shape=jax.ShapeDtypeStruct((batch, n_lat, n_heads * HEAD_W), BF16),
        scratch_shapes=[pltpu.VMEM((2 * t, ROW_TILE), F32), pltpu.VMEM((SUBLANES, ROW_TILE), F32)],
        compiler_params=_params(3),
        name=name,
    )(lam, subln, q, q, k, vt)
    if not need_ctx:
        return o_lat, None
    o_ctx = pl.pallas_call(
        functools.partial(_attn_ctx_kernel, n_keys=n_ctx, **static),
        grid=(batch, n_groups),
        in_specs=[
            _resident(lam.shape),
            _resident(subln.shape),
            pl.BlockSpec((1, n_ctx, qw), lambda b, h: (b, ctx_tile, h)),
            pl.BlockSpec((1, n_ctx, kw), lambda b, h: (b, ctx_tile, h)),
            pl.BlockSpec((1, kw, n_ctx), lambda b, h: (b, h, ctx_tile)),
        ],
        out_specs=pl.BlockSpec((1, n_ctx, qw), lambda b, h: (b, 0, h)),
        out_shape=jax.ShapeDtypeStruct((batch, n_ctx, n_heads * HEAD_W), BF16),
        scratch_shapes=[pltpu.VMEM((2 * n_ctx, n_ctx), F32)],
        compiler_params=_params(2),
        name=name + "_ctx",
    )(lam, subln, q, k, vt)
    return o_lat, o_ctx


DFT_N1 = 16
DFT_LANE_CHUNK = 16384


def _dft_stage1_kernel(m_ref, xc_ref, xs_ref, o_ref):
    n1 = xc_ref.shape[1]
    rhs = jnp.concatenate([xc_ref[0], xs_ref[0]], axis=0)
    a = _dot(m_ref[...], rhs).astype(BF16)
    o_ref[0, 0] = a[:n1]
    o_ref[0, 1] = a[n1:]


def _dft_stage2_kernel(g_ref, a_ref, o_ref):
    rhs = jnp.concatenate([a_ref[0, 0, 0], a_ref[0, 1, 0]], axis=0)
    o_ref[0] = _dot(g_ref[0], rhs).astype(BF16)


def _dft_ctx_kernel(c_ref, s_ref, xc_ref, xs_ref, o_ref):
    o_ref[0] = (_dot(c_ref[...], xc_ref[0]) + _dot(s_ref[...], xs_ref[0])).astype(BF16)


def _position_dft(xc, xs, n_lat, need_ctx):
    batch, t, d = xc.shape
    n_ctx = t - n_lat
    n1, n2 = DFT_N1, n_lat // DFT_N1
    assert t % n2 == 0 and (n2 * d) % DFT_LANE_CHUNK == 0
    m32, g = _dft_factor_tables(n_lat, n1)
    cols = n2 * d
    view = (batch, t // n2, cols)
    a = pl.pallas_call(
        _dft_stage1_kernel,
        grid=(batch, cols // DFT_LANE_CHUNK),
        in_specs=[
            _resident((2 * n1, 2 * n1)),
            pl.BlockSpec((1, n1, DFT_LANE_CHUNK), lambda b, j: (b, 0, j)),
            pl.BlockSpec((1, n1, DFT_LANE_CHUNK), lambda b, j: (b, 0, j)),
        ],
        out_specs=pl.BlockSpec((1, 2, n1, DFT_LANE_CHUNK), lambda b, j: (b, 0, 0, j)),
        out_shape=jax.ShapeDtypeStruct((batch, 2, n1, cols), BF16),
        compiler_params=_params(2),
        name="position_dft_stage1",
    )(m32, xc.reshape(view), xs.reshape(view))
    z = pl.pallas_call(
        _dft_stage2_kernel,
        grid=(batch, n1),
        in_specs=[
            pl.BlockSpec((1, n2, 2 * n2), lambda b, k1: (k1, 0, 0)),
            pl.BlockSpec((1, 2, 1, n2, d), lambda b, k1: (b, 0, k1, 0, 0)),
        ],
        out_specs=pl.BlockSpec((1, n2, d), lambda b, k1: (b, 0, k1)),
        out_shape=jax.ShapeDtypeStruct((batch, n2, n1 * d), BF16),
        compiler_params=_params(2),
        name="position_dft_stage2",
    )(g, a.reshape(batch, 2, n1, n2, d))
    f_lat = z.reshape(batch, n_lat, d)
    if not need_ctx:
        return f_lat, None
    cc, sc = _dft_tables(n_ctx, n_ctx ** -0.5)
    ctx_block = n_lat // n_ctx
    f_ctx = pl.pallas_call(
        _dft_ctx_kernel,
        grid=(batch,),
        in_specs=[
            _resident((n_ctx, n_ctx)),
            _resident((n_ctx, n_ctx)),
            pl.BlockSpec((1, n_ctx, d), lambda b: (b, ctx_block, 0)),
            pl.BlockSpec((1, n_ctx, d), lambda b: (b, ctx_block, 0)),
        ],
        out_specs=pl.BlockSpec((1, n_ctx, d), lambda b: (b, 0, 0)),
        out_shape=jax.ShapeDtypeStruct((batch, n_ctx, d), BF16),
        compiler_params=_params(1),
        name="position_dft_ctx",
    )(cc, sc, xc, xs)
    return f_lat, f_ctx


def _out_ffn_kernel(a_ref, actx_ref, wo_ref, bo_ref, x_ref, mod_ref, ln_ref, wgu_ref, wd_ref, fn_ref, o_ref,
                    a_scr, *, d_ff, final_norm, ctx_tile):
    mod = mod_ref[0]
    a = jnp.where(pl.program_id(1) == ctx_tile, actx_ref[0], a_ref[0])
    x = x_ref[0] + mod[G1:G1 + 1] * (_dot(a, wo_ref[...]) + bo_ref[...])
    h = _norm_mod(x, ln_ref[...], mod, SH2, SC2)
    for c in range(d_ff // FF_CHUNK):
        lo = c * FF_CHUNK
        g = _dot(h, wgu_ref[:, lo:lo + FF_CHUNK])
        u = _dot(h, wgu_ref[:, d_ff + lo:d_ff + lo + FF_CHUNK])
        a_scr[:, lo:lo + FF_CHUNK] = (_silu(g) * u).astype(BF16)
    y = x + mod[G2:G2 + 1] * _dot(a_scr[...], wd_ref[...])
    if final_norm:
        y = _rms(y, fn_ref[...])
    o_ref[0] = y


def _out_ffn(a, a_ctx, w_o, b_o, x, mod, ln, w_gu, w_down, final_g, n_tiles, ctx_tile, final_norm):
    batch, _, d = x.shape
    k = a.shape[2]
    d_ff = w_down.shape[0]
    row = lambda b, j: (b, j, 0)
    return pl.pallas_call(
        functools.partial(_out_ffn_kernel, d_ff=d_ff, final_norm=final_norm, ctx_tile=ctx_tile),
        grid=(batch, n_tiles),
        in_specs=[
            pl.BlockSpec((1, ROW_TILE, k), lambda b, j: (b, jnp.minimum(j, ctx_tile - 1), 0)),
            pl.BlockSpec((1, ROW_TILE, k), lambda b, j: (b, 0, 0)),
            _resident((k, d)),
            _resident((1, d)),
            pl.BlockSpec((1, ROW_TILE, d), row),
            pl.BlockSpec((1, 6, d), _mod_spec(ctx_tile, batch)),
            _resident((1, d)),
            _resident((d, 2 * d_ff)),
            _resident((d_ff, d)),
            _resident((1, d)),
        ],
        out_specs=pl.BlockSpec((1, ROW_TILE, d), row),
        out_shape=jax.ShapeDtypeStruct((batch, n_tiles * ROW_TILE, d), F32),
        scratch_shapes=[pltpu.VMEM((ROW_TILE, d_ff), BF16)],
        compiler_params=_params(2),
        name="out_ffn",
    )(a, a_ctx, w_o, b_o, x, mod, ln, w_gu, w_down, final_g)


def _half_split_perm(n_comp, comp_dim):
    src = np.zeros(HEAD_W, np.int32)
    for m in range(n_comp):
        for i in range(comp_dim // 2):
            for p in range(2):
                src[p * (HEAD_W // 2) + m * (comp_dim // 2) + i] = m * comp_dim + 2 * i + p
    return src


def _rope_lane_tables(seq, n_ctx, head_dim):
    axis_dim = head_dim // 2
    rows = jnp.arange(seq, dtype=jnp.int32) // GRID_W
    cols = jnp.arange(seq, dtype=jnp.int32) % GRID_W
    inv_freq = ROPE_THETA ** (-jnp.arange(0, axis_dim, 2, dtype=F32) / axis_dim)
    ang = jnp.concatenate([rows[:, None].astype(F32) * inv_freq,
                           cols[:, None].astype(F32) * inv_freq], axis=-1)
    reps = (HEAD_W // 2) // (head_dim // 2)
    cos = jnp.tile(jnp.cos(ang), (1, 2 * reps))
    sin = jnp.tile(jnp.sin(ang), (1, reps))
    sin = jnp.concatenate([-sin, sin], axis=-1)
    cos = jnp.concatenate([cos, jnp.ones((n_ctx, HEAD_W), F32)], axis=0)
    sin = jnp.concatenate([sin, jnp.zeros((n_ctx, HEAD_W), F32)], axis=0)
    return cos, sin


def _dft_tables(n, scale):
    r = 1
    while r * r < n:
        r *= 2
    q = n // r
    k = jnp.arange(n, dtype=jnp.int32)[:, None]
    a_idx = (k * jnp.arange(q, dtype=jnp.int32)[None, :] * r) % n
    b_idx = (k * jnp.arange(r, dtype=jnp.int32)[None, :]) % n
    w = 2.0 * math.pi / n
    ca, sa = jnp.cos(a_idx.astype(F32) * w), jnp.sin(a_idx.astype(F32) * w)
    cb, sb = jnp.cos(b_idx.astype(F32) * w) * scale, jnp.sin(b_idx.astype(F32) * w) * scale
    cos = ca[:, :, None] * cb[:, None, :] - sa[:, :, None] * sb[:, None, :]
    sin = sa[:, :, None] * cb[:, None, :] + ca[:, :, None] * sb[:, None, :]
    return cos.reshape(n, n).astype(BF16), (-sin).reshape(n, n).astype(BF16)


def _dft_factor_tables(n, n1):
    n2 = n // n1
    a1 = 2.0 * np.pi * ((np.arange(n1)[:, None] * np.arange(n1)[None, :]) % n1) / n1
    c1, s1 = np.cos(a1), np.sin(a1)
    m = np.block([[c1, -s1], [-s1, -c1]])
    k = np.arange(n1)[:, None, None] + n1 * np.arange(n2)[None, :, None]
    a2 = 2.0 * np.pi * ((k * np.arange(n2)[None, None, :]) % n) / n
    g = np.concatenate([np.cos(a2), np.sin(a2)], axis=-1) / math.sqrt(n)
    return jnp.asarray(m, dtype=BF16), jnp.asarray(g, dtype=BF16)


def _channel_dft_weight(gd):
    idx = (np.arange(gd)[:, None] * np.arange(gd)[None, :]) % gd
    ang = 2.0 * np.pi * idx / gd
    w = np.concatenate([np.cos(ang), np.sin(ang)], axis=1) / math.sqrt(gd)
    return jnp.asarray(w, dtype=BF16)


def kernel(x, c, ctx, c_ctx, mod_w, mod_b, ln_mix, ln_ffn, ffn_w_gu, ffn_w_down, a_w_qkv, a_lam, a_subln, a_w_o, b_w_qkv, b_q_norm, b_k_norm, b_w_o, c_w_o, c_b_o, final_norm):
    batch, seq, d = x.shape
    n_ctx = ctx.shape[1]
    depth = mod_w.shape[0]
    t = seq + n_ctx
    assert seq % ROW_TILE == 0 and n_ctx == ROW_TILE and batch < MOD_ROWS
    n_tiles_all = t // ROW_TILE
    n_tiles_lat = seq // ROW_TILE
    ctx_tile = n_tiles_all - 1

    cs = jnp.concatenate([c, c_ctx[None, :], jnp.zeros((MOD_ROWS - batch - 1, d), F32)], axis=0)
    mod_all = _modulation(cs, mod_w, mod_b).reshape(depth, MOD_ROWS, 6, d)

    xs = jnp.concatenate([x, ctx], axis=1)

    perm_a = _half_split_perm(2, DIFF_HEAD_DIM)
    perm_b = _half_split_perm(1, GQA_HEAD_DIM)
    rope_a = _rope_lane_tables(seq, n_ctx, DIFF_HEAD_DIM)
    rope_b = _rope_lane_tables(seq, n_ctx, GQA_HEAD_DIM)
    ones_head = jnp.ones((1, HEAD_W), F32)
    no_bias = jnp.zeros((1, d), F32)

    for i in range(depth):
        last = i == depth - 1
        kind, j = i % N_MIXERS, i // N_MIXERS
        mod = mod_all[i]
        ln1 = ln_mix[i][None, :]
        n_out_tiles = n_tiles_lat if last else n_tiles_all

        if kind == 0:
            n_q = DIFF_HEADS * HEAD_W
            cols = (np.arange(2 * DIFF_HEADS)[:, None] * HEAD_W + perm_a[None, :]).reshape(-1)
            w = a_w_qkv[j]
            w_qk = w[:, cols].astype(BF16)
            w_vt = w[:, 2 * n_q:].T.astype(BF16)
            q, k, vt = _qkv_project(
                xs, mod, ln1, w_qk, w_vt, rope_a[0], rope_a[1], ones_head, ones_head,
                n_q=n_q, n_k=n_q, head_norm=False, q_scale=DIFF_HEAD_DIM ** -0.5 * LOG2E)
            mixed, mixed_ctx = _attention(
                q, k, vt, a_lam[j], a_subln[j][None, :], diff=True, lambda_init=_lambda_init(i),
                n_heads=DIFF_HEADS, heads_per_step=4, kv_heads_per_step=4, n_lat=seq, need_ctx=not last)
            w_o, b_o = a_w_o[j], no_bias
        elif kind == 1:
            n_q = GQA_HEADS * HEAD_W
            n_k = GQA_KV_HEADS * HEAD_W
            cols = (np.arange(GQA_HEADS + GQA_KV_HEADS)[:, None] * HEAD_W + perm_b[None, :]).reshape(-1)
            w = b_w_qkv[j]
            w_qk = w[:, cols].astype(BF16)
            w_vt = w[:, n_q + n_k:].T.astype(BF16)
            q, k, vt = _qkv_project(
                xs, mod, ln1, w_qk, w_vt, rope_b[0], rope_b[1],
                b_q_norm[j][perm_b][None, :], b_k_norm[j][perm_b][None, :],
                n_q=n_q, n_k=n_k, head_norm=True, q_scale=GQA_HEAD_DIM ** -0.5 * LOG2E)
            mixed, mixed_ctx = _attention(
                q, k, vt, jnp.zeros((4, DIFF_HEAD_DIM), F32), ones_head, diff=False, lambda_init=0.0,
                n_heads=GQA_HEADS, heads_per_step=GQA_HEADS, kv_heads_per_step=GQA_KV_HEADS,
                n_lat=seq, need_ctx=not last)
            w_o, b_o = b_w_o[j], no_bias
        else:
            x_cos, x_sin = _channel_dft(xs, mod, ln1, _channel_dft_weight(d // FNET_GROUPS))
            mixed, mixed_ctx = _position_dft(x_cos, x_sin, seq, need_ctx=not last)
            w_o, b_o = c_w_o[j], c_b_o[j][None, :]

        if last:
            mixed_ctx = mixed
        xs = _out_ffn(mixed, mixed_ctx, w_o.astype(BF16), b_o, xs, mod, ln_ffn[i][None, :],
                      ffn_w_gu[i].astype(BF16), ffn_w_down[i].astype(BF16), final_norm[None, :],
                      n_out_tiles, ctx_tile, final_norm=last)
    return xs
```

```python
import functools
import math

import numpy as np
import jax
import jax.numpy as jnp
from jax import lax
from jax.experimental import pallas as pl
from jax.experimental.pallas import tpu as pltpu

GRID_W = 64
ROPE_THETA = 10000.0
NORM_EPS = 1e-6
DIFF_HEADS = 8
DIFF_HEAD_DIM = 64
GQA_HEADS = 8
GQA_KV_HEADS = 2
GQA_HEAD_DIM = 128
FNET_GROUPS = 4
N_MIXERS = 3

LANES = 128
SUBLANES = 8
VMEM_LIMIT = 56 * 1024 * 1024

ROW_TILE = 256
HEAD_W = 128
KEY_CHUNK = 256
SCORE_LEAD = 2
FF_CHUNK = 256
ONES_ROWS = 16
MOD_ROWS = 16
LOG2E = math.log2(math.e)

BF16 = jnp.bfloat16
F32 = jnp.float32

SH1, SC1, G1, SH2, SC2, G2 = range(6)


def _lambda_init(layer_idx):
    return 0.8 - 0.6 * float(np.exp(-0.3 * layer_idx))


def _params(n_grid, flags=None):
    return pltpu.CompilerParams(
        dimension_semantics=("arbitrary",) * n_grid, vmem_limit_bytes=VMEM_LIMIT, flags=flags)


def _resident(shape):
    nd = len(shape)
    return pl.BlockSpec(shape, lambda *_: (0,) * nd, pipeline_mode=pl.Buffered(1))


def _dot(a, b):
    return jnp.dot(a, b, preferred_element_type=F32)


def _dot_nt(a, b):
    return lax.dot_general(a, b, (((1,), (1,)), ((), ())), preferred_element_type=F32)


def _rms(x, g):
    return x * lax.rsqrt(jnp.mean(x * x, axis=-1, keepdims=True) + NORM_EPS) * g


def _silu(x):
    return x / (1.0 + jnp.exp(-x))


def _mod_kernel(cs_ref, w_ref, b_ref, o_ref):
    s = _silu(cs_ref[...]).astype(BF16)
    o_ref[0] = _dot(s, w_ref[0].astype(BF16)) + b_ref[0]


def _modulation(cs, mod_w, mod_b, tn=1536):
    depth, d, n = mod_w.shape
    return pl.pallas_call(
        _mod_kernel,
        grid=(depth, n // tn),
        in_specs=[
            pl.BlockSpec((MOD_ROWS, d), lambda i, j: (0, 0)),
            pl.BlockSpec((1, d, tn), lambda i, j: (i, 0, j)),
            pl.BlockSpec((1, 1, tn), lambda i, j: (i, 0, j)),
        ],
        out_specs=pl.BlockSpec((1, MOD_ROWS, tn), lambda i, j: (i, 0, j)),
        out_shape=jax.ShapeDtypeStruct((depth, MOD_ROWS, n), F32),
        compiler_params=_params(2),
        name="modulation",
    )(cs, mod_w, mod_b.reshape(depth, 1, n))


def _mod_spec(ctx_tile, batch):
    def index(b, j):
        return (jnp.where(j == ctx_tile, batch, b), 0, 0)
    return index


def _norm_mod(x, ln, mod, shift, scale):
    y = _rms(x, ln)
    return (y * (1.0 + mod[scale:scale + 1]) + mod[shift:shift + 1]).astype(BF16)


def _stream_specs(stream, ctx_tile):
    d = stream[0].shape[2]
    if len(stream) == 1:
        return [pl.BlockSpec((1, ROW_TILE, d), lambda b, j: (b, j, 0))]
    return [pl.BlockSpec((1, ROW_TILE, d), lambda b, j: (b, jnp.minimum(j, ctx_tile - 1), 0)),
            pl.BlockSpec((1, ROW_TILE, d), lambda b, j: (b, 0, 0))]


def _stream_tile(refs, ctx_tile):
    if len(refs) == 1:
        return refs[0][0]
    return jnp.where(pl.program_id(1) == ctx_tile, refs[1][0], refs[0][0])


def _qkv_kernel(*refs, n_stream, n_q, n_k, head_norm, q_scale, ctx_tile):
    mod_ref, ln_ref, w_ref, wvt_ref, cos_ref, sin_ref, qn_ref, kn_ref, q_ref, k_ref, vt_ref = refs[n_stream:]
    x = _stream_tile(refs[:n_stream], ctx_tile)
    h = _norm_mod(x, ln_ref[...], mod_ref[0], SH1, SC1)
    cos, sin = cos_ref[...], sin_ref[...]
    for head in range((n_q + n_k) // HEAD_W):
        is_q = head < n_q // HEAD_W
        if head % 2 == 0:
            pair = _dot(h, w_ref[:, head * HEAD_W:(head + 2) * HEAD_W])
        t = pair[:, (head % 2) * HEAD_W:(head % 2 + 1) * HEAD_W]
        if head_norm:
            t = _rms(t, qn_ref[...] if is_q else kn_ref[...])
        t = t * cos + pltpu.roll(t, HEAD_W // 2, 1) * sin
        if is_q:
            q_ref[0, :, head * HEAD_W:(head + 1) * HEAD_W] = (t * q_scale).astype(BF16)
        else:
            c0 = head * HEAD_W - n_q
            k_ref[0, :, c0:c0 + HEAD_W] = t.astype(BF16)
    vt_ref[0] = _dot_nt(wvt_ref[...], h).astype(BF16)


def _qkv_project(stream, mod, ln, w_qk, w_vt, cosf, sinf, qn, kn, *, n_q, n_k, head_norm, q_scale):
    batch, _, d = stream[0].shape
    t = cosf.shape[0]
    n_v = w_vt.shape[0]
    n_tiles = t // ROW_TILE
    row = lambda b, j: (b, j, 0)
    kern = functools.partial(_qkv_kernel, n_stream=len(stream), n_q=n_q, n_k=n_k, head_norm=head_norm,
                             q_scale=q_scale, ctx_tile=n_tiles - 1)
    return pl.pallas_call(
        kern,
        grid=(batch, n_tiles),
        in_specs=_stream_specs(stream, n_tiles - 1) + [
            pl.BlockSpec((1, 6, d), _mod_spec(n_tiles - 1, batch)),
            _resident((1, d)),
            _resident((d, n_q + n_k)),
            _resident((n_v, d)),
            pl.BlockSpec((ROW_TILE, HEAD_W), lambda b, j: (j, 0)),
            pl.BlockSpec((ROW_TILE, HEAD_W), lambda b, j: (j, 0)),
            _resident((1, HEAD_W)),
            _resident((1, HEAD_W)),
        ],
        out_specs=[
            pl.BlockSpec((1, ROW_TILE, n_q), row),
            pl.BlockSpec((1, ROW_TILE, n_k), row),
            pl.BlockSpec((1, n_v, ROW_TILE), lambda b, j: (b, 0, j)),
        ],
        out_shape=[
            jax.ShapeDtypeStruct((batch, t, n_q), BF16),
            jax.ShapeDtypeStruct((batch, t, n_k), BF16),
            jax.ShapeDtypeStruct((batch, n_v, t), BF16),
        ],
        compiler_params=_params(2),
        name="qkv_project",
    )(*stream, mod, ln, w_qk, w_vt, cosf, sinf, qn, kn)


def _chan_dft_kernel(*refs, n_stream, ctx_tile):
    mod_ref, ln_ref, w_ref, oc_ref, os_ref = refs[n_stream:]
    x = _stream_tile(refs[:n_stream], ctx_tile)
    h = _norm_mod(x, ln_ref[...], mod_ref[0], SH1, SC1)
    d = h.shape[1]
    gd = w_ref.shape[0]
    for g in range(d // gd):
        y = _dot(h[:, g * gd:(g + 1) * gd], w_ref[...])
        oc_ref[0, :, g * gd:(g + 1) * gd] = y[:, :gd].astype(BF16)
        os_ref[0, :, g * gd:(g + 1) * gd] = y[:, gd:].astype(BF16)


def _channel_dft(stream, t, mod, ln, w):
    batch, _, d = stream[0].shape
    n_tiles = t // ROW_TILE
    row = lambda b, j: (b, j, 0)
    return pl.pallas_call(
        functools.partial(_chan_dft_kernel, n_stream=len(stream), ctx_tile=n_tiles - 1),
        grid=(batch, n_tiles),
        in_specs=_stream_specs(stream, n_tiles - 1) + [
            pl.BlockSpec((1, 6, d), _mod_spec(n_tiles - 1, batch)),
            _resident((1, d)),
            _resident(w.shape),
        ],
        out_specs=[pl.BlockSpec((1, ROW_TILE, d), row)] * 2,
        out_shape=[jax.ShapeDtypeStruct((batch, t, d), BF16)] * 2,
        compiler_params=_params(2),
        name="channel_dft",
    )(*stream, mod, ln, w)


def _score_chunk(k_ref, s_scr, q, kvh, buf, c, k_lo, n_chunks, m8):
    tq = q.shape[0]
    rows = slice(k_lo + c * KEY_CHUNK, k_lo + (c + 1) * KEY_CHUNK)
    s = _dot_nt(k_ref[0, rows, kvh * HEAD_W:(kvh + 1) * HEAD_W], q)
    lo = (buf * n_chunks + c) * KEY_CHUNK
    s_scr[lo:lo + KEY_CHUNK, :] = s
    return jnp.maximum(m8, jnp.max(s.reshape(KEY_CHUNK // SUBLANES, SUBLANES, tq), axis=0))


def _softmax_pv_streams(score_list, weigh_list, offset, k_ref, vt_ref, s_scr, m8_carried, k_lo, n_keys,
                        on_done):
    tq = score_list[0][0].shape[0]
    n_chunks = n_keys // KEY_CHUNK
    ones_rows = jnp.ones((ONES_ROWS, KEY_CHUNK), BF16)

    def weigh(kvh, buf, c, mx, acc):
        lo = (buf * n_chunks + c) * KEY_CHUNK
        e = jnp.exp2(s_scr[lo:lo + KEY_CHUNK, :] - mx)
        cols = slice(k_lo + c * KEY_CHUNK, k_lo + (c + 1) * KEY_CHUNK)
        v_ones = jnp.concatenate([vt_ref[0, kvh * HEAD_W:(kvh + 1) * HEAD_W, cols], ones_rows], axis=0)
        return acc + _dot(v_ones, e.astype(BF16))

    m8 = [None] * len(score_list)
    mx = acc = None
    n_steps = max(len(score_list) * n_chunks, offset + len(weigh_list) * n_chunks)
    for g in range(n_steps):
        if g < len(score_list) * n_chunks:
            j, c = divmod(g, n_chunks)
            q, kvh, buf = score_list[j]
            if c == 0:
                m8[j] = jnp.full((SUBLANES, tq), -jnp.inf, F32)
            m8[j] = _score_chunk(k_ref, s_scr, q, kvh, buf, c, k_lo, n_chunks, m8[j])
        ge = g - offset
        if 0 <= ge < len(weigh_list) * n_chunks:
            p, c = divmod(ge, n_chunks)
            kvh, buf, max_src = weigh_list[p]
            if c == 0:
                src = m8_carried if max_src is None else m8[max_src]
                mx = jnp.max(src, axis=0, keepdims=True)
                acc = jnp.zeros((HEAD_W + ONES_ROWS, tq), F32)
            acc = weigh(kvh, buf, c, mx, acc)
            if c == n_chunks - 1:
                on_done(p, acc[:HEAD_W] * (1.0 / acc[HEAD_W:HEAD_W + 1]))
    return m8


def _head_problems(q_ref, diff, heads_per_step, kv_heads_per_step, n_heads=None):
    problems = []
    for hh in range(heads_per_step if n_heads is None else n_heads):
        q = q_ref[0, :, hh * HEAD_W:(hh + 1) * HEAD_W]
        kvh = hh * kv_heads_per_step // heads_per_step
        if diff:
            lane = lax.broadcasted_iota(jnp.int32, q.shape, 1)
            first = (lane % (HEAD_W // 2)) < (HEAD_W // 4)
            zero = jnp.zeros_like(q)
            problems += [(jnp.where(first, q, zero), kvh), (jnp.where(first, zero, q), kvh)]
        else:
            problems.append((q, kvh))
    return problems


def _head_writer(o_ref, lam_ref, subln_ref, diff, lambda_init):
    first_comp = {}

    def on_done(p, o_t):
        if not diff:
            o_ref[0, :, p * HEAD_W:(p + 1) * HEAD_W] = o_t.T.astype(BF16)
            return
        hh = p // 2
        if p % 2 == 0:
            first_comp[hh] = o_t
            return
        lv = lam_ref[...]
        lam = (jnp.exp(jnp.sum(lv[0:1] * lv[1:2], axis=-1, keepdims=True))
               - jnp.exp(jnp.sum(lv[2:3] * lv[3:4], axis=-1, keepdims=True)) + lambda_init)
        o = (first_comp.pop(hh) - lam * o_t).T
        o = _rms(o, subln_ref[...]) * (1.0 - lambda_init)
        o_ref[0, :, hh * HEAD_W:(hh + 1) * HEAD_W] = o.astype(BF16)

    return on_done


def _attn_latent_kernel(lam_ref, subln_ref, q_ref, qn_ref, k_ref, vt_ref, o_ref, s_scr, m8_scr,
                        *, diff, lambda_init, heads_per_step, kv_heads_per_step, n_keys):
    n_chunks = n_keys // KEY_CHUNK
    cur = _head_problems(q_ref, diff, heads_per_step, kv_heads_per_step)
    nxt_q, nxt_kvh = _head_problems(qn_ref, diff, heads_per_step, kv_heads_per_step, n_heads=1)[0]
    n_prob = len(cur)
    assert n_prob % 2 == 0

    @pl.when(pl.program_id(2) == 0)
    def _():
        q, kvh = cur[0]
        m8 = jnp.full((SUBLANES, q.shape[0]), -jnp.inf, F32)
        for c in range(n_chunks):
            m8 = _score_chunk(k_ref, s_scr, q, kvh, 0, c, 0, n_chunks, m8)
        m8_scr[...] = m8

    score_list = [(q, kvh, p % 2) for p, (q, kvh) in enumerate(cur) if p > 0] + [(nxt_q, nxt_kvh, 0)]
    weigh_list = [(kvh, p % 2, None if p == 0 else p - 1) for p, (_, kvh) in enumerate(cur)]
    m8 = _softmax_pv_streams(
        score_list, weigh_list, min(SCORE_LEAD, n_chunks - 1), k_ref, vt_ref, s_scr, m8_scr[...], 0, n_keys,
        _head_writer(o_ref, lam_ref, subln_ref, diff, lambda_init))
    m8_scr[...] = m8[-1]


def _attn_ctx_kernel(lam_ref, subln_ref, q_ref, k_ref, vt_ref, o_ref, s_scr,
                     *, diff, lambda_init, heads_per_step, kv_heads_per_step, n_keys):
    n_chunks = n_keys // KEY_CHUNK
    cur = _head_problems(q_ref, diff, heads_per_step, kv_heads_per_step)
    score_list = [(q, kvh, p % 2) for p, (q, kvh) in enumerate(cur)]
    weigh_list = [(kvh, p % 2, p) for p, (_, kvh) in enumerate(cur)]
    _softmax_pv_streams(
        score_list, weigh_list, n_chunks + min(SCORE_LEAD, n_chunks - 1), k_ref, vt_ref, s_scr, None, 0, n_keys,
        _head_writer(o_ref, lam_ref, subln_ref, diff, lambda_init))


def _attention(q, k, vt, lam, subln, *, diff, lambda_init, n_heads, heads_per_step, kv_heads_per_step,
               n_lat, need_ctx):
    batch, t, _ = q.shape
    n_ctx = t - n_lat
    qw = heads_per_step * HEAD_W
    kw = kv_heads_per_step * HEAD_W
    n_groups = n_heads // heads_per_step
    n_lat_tiles = n_lat // ROW_TILE
    ctx_tile = n_lat // n_ctx
    static = dict(diff=diff, lambda_init=lambda_init, heads_per_step=heads_per_step,
                  kv_heads_per_step=kv_heads_per_step)
    name = "diff_attention" if diff else "gqa_attention"
    o_lat = pl.pallas_call(
        functools.partial(_attn_latent_kernel, n_keys=t, **static),
        grid=(batch, n_groups, n_lat_tiles),
        in_specs=[
            _resident(lam.shape),
            _resident(subln.shape),
            pl.BlockSpec((1, ROW_TILE, qw), lambda b, h, i: (b, i, h)),
            pl.BlockSpec((1, ROW_TILE, qw), lambda b, h, i: (b, jnp.minimum(i + 1, n_lat_tiles - 1), h)),
            pl.BlockSpec((1, t, kw), lambda b, h, i: (b, 0, h)),
            pl.BlockSpec((1, kw, t), lambda b, h, i: (b, h, 0)),
        ],
        out_specs=pl.BlockSpec((1, ROW_TILE, qw), lambda b, h, i: (b, i, h)),
        out_shape=jax.ShapeDtypeStruct((batch, n_lat, n_heads * HEAD_W), BF16),
        scratch_shapes=[pltpu.VMEM((2 * t, ROW_TILE), F32), pltpu.VMEM((SUBLANES, ROW_TILE), F32)],
        compiler_params=_params(3),
        name=name,
    )(lam, subln, q, q, k, vt)
    if not need_ctx:
        return o_lat, None
    o_ctx = pl.pallas_call(
        functools.partial(_attn_ctx_kernel, n_keys=n_ctx, **static),
        grid=(batch, n_groups),
        in_specs=[
            _resident(lam.shape),
            _resident(subln.shape),
            pl.BlockSpec((1, n_ctx, qw), lambda b, h: (b, ctx_tile, h)),
            pl.BlockSpec((1, n_ctx, kw), lambda b, h: (b, ctx_tile, h)),
            pl.BlockSpec((1, kw, n_ctx), lambda b, h: (b, h, ctx_tile)),
        ],
        out_specs=pl.BlockSpec((1, n_ctx, qw), lambda b, h: (b, 0, h)),
        out_shape=jax.ShapeDtypeStruct((batch, n_ctx, n_heads * HEAD_W), BF16),
        scratch_shapes=[pltpu.VMEM((2 * n_ctx, n_ctx), F32)],
        compiler_params=_params(2),
        name=name + "_ctx",
    )(lam, subln, q, k, vt)
    return o_lat, o_ctx


DFT_COL_TILE = 512
DFT_EXTRA_ROWS = 16


def _dft_mirror_kernel(cl_ref, sl_ref, ch_ref, flip_ref, xc_ref, xs_ref, o_ref, carry_scr, *, n_steps):
    s = pl.program_id(2)
    t = n_steps - 1 - s

    @pl.when(s == 0)
    def _():
        carry_scr[...] = _dot(ch_ref[...], xc_ref[0])

    a = _dot(cl_ref[...], xc_ref[0])
    b = _dot(sl_ref[...], xs_ref[0])
    total = a + b
    row = lax.broadcasted_iota(jnp.int32, total.shape, 0)
    mirror = _dot(flip_ref[...], total.astype(BF16))
    mirror = jnp.where(row == 0, carry_scr[0:1, :], mirror)
    carry_scr[0:1, :] = total[0:1]
    o_ref[0, pl.ds(pl.multiple_of(t * ROW_TILE, ROW_TILE), ROW_TILE), :] = (a - b).astype(BF16)
    lo = pl.multiple_of((2 * n_steps - 1 - t) * ROW_TILE, ROW_TILE)
    o_ref[0, pl.ds(lo, ROW_TILE), :] = mirror.astype(BF16)


def _dft_ctx_kernel(c_ref, s_ref, xc_ref, xs_ref, o_ref):
    o_ref[0] = (_dot(c_ref[...], xc_ref[0]) + _dot(s_ref[...], xs_ref[0])).astype(BF16)


def _position_dft(xc, xs, n_lat, need_ctx):
    batch, t, d = xc.shape
    n_ctx = t - n_lat
    half = n_lat // 2
    n_steps = half // ROW_TILE
    cl, sl = _dft_tables(n_lat, n_lat ** -0.5, n_rows=half + DFT_EXTRA_ROWS, negate_sin=False)
    c_half = cl[half:]
    cl, sl = cl[:half], sl[:half]
    flip = np.zeros((ROW_TILE, ROW_TILE), np.float32)
    flip[np.arange(1, ROW_TILE), ROW_TILE - np.arange(1, ROW_TILE)] = 1.0
    tile_rows = lambda b, c, s: (n_steps - 1 - s, 0)
    f_lat = pl.pallas_call(
        functools.partial(_dft_mirror_kernel, n_steps=n_steps),
        grid=(batch, d // DFT_COL_TILE, n_steps),
        in_specs=[
            pl.BlockSpec((ROW_TILE, n_lat), tile_rows),
            pl.BlockSpec((ROW_TILE, n_lat), tile_rows),
            _resident((DFT_EXTRA_ROWS, n_lat)),
            _resident((ROW_TILE, ROW_TILE)),
            pl.BlockSpec((1, n_lat, DFT_COL_TILE), lambda b, c, s: (b, 0, c)),
            pl.BlockSpec((1, n_lat, DFT_COL_TILE), lambda b, c, s: (b, 0, c)),
        ],
        out_specs=pl.BlockSpec((1, n_lat, DFT_COL_TILE), lambda b, c, s: (b, 0, c)),
        out_shape=jax.ShapeDtypeStruct((batch, n_lat, d), BF16),
        scratch_shapes=[pltpu.VMEM((DFT_EXTRA_ROWS, DFT_COL_TILE), F32)],
        compiler_params=_params(3),
        name="position_dft",
    )(cl, sl, c_half, jnp.asarray(flip, dtype=BF16), xc, xs)
    if not need_ctx:
        return f_lat, None
    cc, sc = _dft_tables(n_ctx, n_ctx ** -0.5)
    ctx_block = n_lat // n_ctx
    f_ctx = pl.pallas_call(
        _dft_ctx_kernel,
        grid=(batch,),
        in_specs=[
            _resident((n_ctx, n_ctx)),
            _resident((n_ctx, n_ctx)),
            pl.BlockSpec((1, n_ctx, d), lambda b: (b, ctx_block, 0)),
            pl.BlockSpec((1, n_ctx, d), lambda b: (b, ctx_block, 0)),
        ],
        out_specs=pl.BlockSpec((1, n_ctx, d), lambda b: (b, 0, 0)),
        out_shape=jax.ShapeDtypeStruct((batch, n_ctx, d), BF16),
        compiler_params=_params(1),
        name="position_dft_ctx",
    )(cc, sc, xc, xs)
    return f_lat, f_ctx


def _out_ffn_kernel(*refs, n_mixed, n_stream, d_ff, final_norm, ctx_tile):
    n_in = n_mixed + n_stream
    wo_ref, bo_ref, mod_ref, ln_ref, wgu_ref, wd_ref, fn_ref, o_ref, a_scr = refs[n_in:]
    mod = mod_ref[0]
    a = _stream_tile(refs[:n_mixed], ctx_tile)
    x = _stream_tile(refs[n_mixed:n_in], ctx_tile) + mod[G1:G1 + 1] * (_dot(a, wo_ref[...]) + bo_ref[...])
    h = _norm_mod(x, ln_ref[...], mod, SH2, SC2)
    for c in range(d_ff // FF_CHUNK):
        lo = c * FF_CHUNK
        g = _dot(h, wgu_ref[:, lo:lo + FF_CHUNK])
        u = _dot(h, wgu_ref[:, d_ff + lo:d_ff + lo + FF_CHUNK])
        a_scr[:, lo:lo + FF_CHUNK] = (_silu(g) * u).astype(BF16)
    y = x + mod[G2:G2 + 1] * _dot(a_scr[...], wd_ref[...])
    if final_norm:
        y = _rms(y, fn_ref[...])
    o_ref[0] = y


def _out_ffn(mixed, w_o, b_o, stream, mod, ln, w_gu, w_down, final_g, n_tiles, ctx_tile, final_norm):
    batch, _, d = stream[0].shape
    k = mixed[0].shape[2]
    d_ff = w_down.shape[0]
    row = lambda b, j: (b, j, 0)
    kern = functools.partial(_out_ffn_kernel, n_mixed=len(mixed), n_stream=len(stream), d_ff=d_ff,
                             final_norm=final_norm, ctx_tile=ctx_tile)
    return pl.pallas_call(
        kern,
        grid=(batch, n_tiles),
        in_specs=_stream_specs(mixed, ctx_tile) + _stream_specs(stream, ctx_tile) + [
            _resident((k, d)),
            _resident((1, d)),
            pl.BlockSpec((1, 6, d), _mod_spec(ctx_tile, batch)),
            _resident((1, d)),
            _resident((d, 2 * d_ff)),
            _resident((d_ff, d)),
            _resident((1, d)),
        ],
        out_specs=pl.BlockSpec((1, ROW_TILE, d), row),
        out_shape=jax.ShapeDtypeStruct((batch, n_tiles * ROW_TILE, d), F32),
        scratch_shapes=[pltpu.VMEM((ROW_TILE, d_ff), BF16)],
        compiler_params=_params(2),
        name="out_ffn",
    )(*mixed, *stream, w_o, b_o, mod, ln, w_gu, w_down, final_g)


def _half_split_perm(n_comp, comp_dim):
    src = np.zeros(HEAD_W, np.int32)
    for m in range(n_comp):
        for i in range(comp_dim // 2):
            for p in range(2):
                src[p * (HEAD_W // 2) + m * (comp_dim // 2) + i] = m * comp_dim + 2 * i + p
    return src


def _rope_lane_tables(seq, n_ctx, head_dim):
    axis_dim = head_dim // 2
    rows = jnp.arange(seq, dtype=jnp.int32) // GRID_W
    cols = jnp.arange(seq, dtype=jnp.int32) % GRID_W
    inv_freq = ROPE_THETA ** (-jnp.arange(0, axis_dim, 2, dtype=F32) / axis_dim)
    ang = jnp.concatenate([rows[:, None].astype(F32) * inv_freq,
                           cols[:, None].astype(F32) * inv_freq], axis=-1)
    reps = (HEAD_W // 2) // (head_dim // 2)
    cos = jnp.tile(jnp.cos(ang), (1, 2 * reps))
    sin = jnp.tile(jnp.sin(ang), (1, reps))
    sin = jnp.concatenate([-sin, sin], axis=-1)
    cos = jnp.concatenate([cos, jnp.ones((n_ctx, HEAD_W), F32)], axis=0)
    sin = jnp.concatenate([sin, jnp.zeros((n_ctx, HEAD_W), F32)], axis=0)
    return cos, sin


def _dft_tables(n, scale, n_rows=None, negate_sin=True):
    n_rows = n if n_rows is None else n_rows
    r = 1
    while r * r < n:
        r *= 2
    q = n // r
    k = jnp.arange(n_rows, dtype=jnp.int32)[:, None]
    a_idx = (k * jnp.arange(q, dtype=jnp.int32)[None, :] * r) % n
    b_idx = (k * jnp.arange(r, dtype=jnp.int32)[None, :]) % n
    w = 2.0 * math.pi / n
    ca, sa = jnp.cos(a_idx.astype(F32) * w), jnp.sin(a_idx.astype(F32) * w)
    cb, sb = jnp.cos(b_idx.astype(F32) * w) * scale, jnp.sin(b_idx.astype(F32) * w) * scale
    cos = ca[:, :, None] * cb[:, None, :] - sa[:, :, None] * sb[:, None, :]
    sin = sa[:, :, None] * cb[:, None, :] + ca[:, :, None] * sb[:, None, :]
    sin = -sin if negate_sin else sin
    return cos.reshape(n_rows, n).astype(BF16), sin.reshape(n_rows, n).astype(BF16)


def _channel_dft_weight(gd):
    idx = (np.arange(gd)[:, None] * np.arange(gd)[None, :]) % gd
    ang = 2.0 * np.pi * idx / gd
    w = np.concatenate([np.cos(ang), np.sin(ang)], axis=1) / math.sqrt(gd)
    return jnp.asarray(w, dtype=BF16)


def kernel(x, c, ctx, c_ctx, mod_w, mod_b, ln_mix, ln_ffn, ffn_w_gu, ffn_w_down, a_w_qkv, a_lam, a_subln, a_w_o, b_w_qkv, b_q_norm, b_k_norm, b_w_o, c_w_o, c_b_o, final_norm):
    batch, seq, d = x.shape
    n_ctx = ctx.shape[1]
    depth = mod_w.shape[0]
    t = seq + n_ctx
    assert seq % ROW_TILE == 0 and n_ctx == ROW_TILE and batch < MOD_ROWS
    n_tiles_all = t // ROW_TILE
    n_tiles_lat = seq // ROW_TILE
    ctx_tile = n_tiles_all - 1

    cs = jnp.concatenate([c, c_ctx[None, :], jnp.zeros((MOD_ROWS - batch - 1, d), F32)], axis=0)
    mod_all = _modulation(cs, mod_w, mod_b).reshape(depth, MOD_ROWS, 6, d)

    stream = (x, ctx)

    perm_a = _half_split_perm(2, DIFF_HEAD_DIM)
    perm_b = _half_split_perm(1, GQA_HEAD_DIM)
    rope_a = _rope_lane_tables(seq, n_ctx, DIFF_HEAD_DIM)
    rope_b = _rope_lane_tables(seq, n_ctx, GQA_HEAD_DIM)
    ones_head = jnp.ones((1, HEAD_W), F32)
    no_bias = jnp.zeros((1, d), F32)

    for i in range(depth):
        last = i == depth - 1
        kind, j = i % N_MIXERS, i // N_MIXERS
        mod = mod_all[i]
        ln1 = ln_mix[i][None, :]
        n_out_tiles = n_tiles_lat if last else n_tiles_all

        if kind == 0:
            n_q = DIFF_HEADS * HEAD_W
            cols = (np.arange(2 * DIFF_HEADS)[:, None] * HEAD_W + perm_a[None, :]).reshape(-1)
            w = a_w_qkv[j]
            w_qk = w[:, cols].astype(BF16)
            w_vt = w[:, 2 * n_q:].T.astype(BF16)
            q, k, vt = _qkv_project(
                stream, mod, ln1, w_qk, w_vt, rope_a[0], rope_a[1], ones_head, ones_head,
                n_q=n_q, n_k=n_q, head_norm=False, q_scale=DIFF_HEAD_DIM ** -0.5 * LOG2E)
            mixed, mixed_ctx = _attention(
                q, k, vt, a_lam[j], a_subln[j][None, :], diff=True, lambda_init=_lambda_init(i),
                n_heads=DIFF_HEADS, heads_per_step=4, kv_heads_per_step=4, n_lat=seq, need_ctx=not last)
            w_o, b_o = a_w_o[j], no_bias
        elif kind == 1:
            n_q = GQA_HEADS * HEAD_W
            n_k = GQA_KV_HEADS * HEAD_W
            cols = (np.arange(GQA_HEADS + GQA_KV_HEADS)[:, None] * HEAD_W + perm_b[None, :]).reshape(-1)
            w = b_w_qkv[j]
            w_qk = w[:, cols].astype(BF16)
            w_vt = w[:, n_q + n_k:].T.astype(BF16)
            q, k, vt = _qkv_project(
                stream, mod, ln1, w_qk, w_vt, rope_b[0], rope_b[1],
                b_q_norm[j][perm_b][None, :], b_k_norm[j][perm_b][None, :],
                n_q=n_q, n_k=n_k, head_norm=True, q_scale=GQA_HEAD_DIM ** -0.5 * LOG2E)
            mixed, mixed_ctx = _attention(
                q, k, vt, jnp.zeros((4, DIFF_HEAD_DIM), F32), ones_head, diff=False, lambda_init=0.0,
                n_heads=GQA_HEADS, heads_per_step=GQA_HEADS, kv_heads_per_step=GQA_KV_HEADS,
                n_lat=seq, need_ctx=not last)
            w_o, b_o = b_w_o[j], no_bias
        else:
            x_cos, x_sin = _channel_dft(stream, t, mod, ln1, _channel_dft_weight(d // FNET_GROUPS))
            mixed, mixed_ctx = _position_dft(x_cos, x_sin, seq, need_ctx=not last)
            w_o, b_o = c_w_o[j], c_b_o[j][None, :]

        xs = _out_ffn((mixed,) if last else (mixed, mixed_ctx), w_o.astype(BF16), b_o, stream, mod,
                      ln_ffn[i][None, :], ffn_w_gu[i].astype(BF16), ffn_w_down[i].astype(BF16),
                      final_norm[None, :], n_out_tiles, ctx_tile, final_norm=last)
        stream = (xs,)
    return xs
```

```python
import functools
import math

import numpy as np
import jax
import jax.numpy as jnp
from jax import lax
from jax.experimental import pallas as pl
from jax.experimental.pallas import tpu as pltpu

GRID_W = 64
ROPE_THETA = 10000.0
NORM_EPS = 1e-6
DIFF_HEADS = 8
DIFF_HEAD_DIM = 64
GQA_HEADS = 8
GQA_KV_HEADS = 2
GQA_HEAD_DIM = 128
FNET_GROUPS = 4
N_MIXERS = 3

LANES = 128
SUBLANES = 8
VMEM_LIMIT = 56 * 1024 * 1024

ROW_TILE = 256
BIG_ROW_TILE = 512
HEAD_W = 128
KEY_CHUNK = 256
SCORE_LEAD = 4
FF_CHUNK = 256
ONES_ROWS = 16
MOD_ROWS = 16
LOG2E = math.log2(math.e)

BF16 = jnp.bfloat16
F32 = jnp.float32

SH1, SC1, G1, SH2, SC2, G2 = range(6)


def _lambda_init(layer_idx):
    return 0.8 - 0.6 * float(np.exp(-0.3 * layer_idx))


def _params(n_grid, flags=None):
    return pltpu.CompilerParams(
        dimension_semantics=("arbitrary",) * n_grid, vmem_limit_bytes=VMEM_LIMIT, flags=flags)


def _resident(shape):
    nd = len(shape)
    return pl.BlockSpec(shape, lambda *_: (0,) * nd, pipeline_mode=pl.Buffered(1))


def _dot(a, b):
    return jnp.dot(a, b, preferred_element_type=F32)


def _dot_nt(a, b):
    return lax.dot_general(a, b, (((1,), (1,)), ((), ())), preferred_element_type=F32)


def _rms(x, g):
    return x * lax.rsqrt(jnp.mean(x * x, axis=-1, keepdims=True) + NORM_EPS) * g


def _silu(x):
    return x / (1.0 + jnp.exp(-x))


def _mod_kernel(cs_ref, w_ref, b_ref, o_ref):
    s = _silu(cs_ref[...]).astype(BF16)
    o_ref[0] = _dot(s, w_ref[0].astype(BF16)) + b_ref[0]


def _modulation(cs, mod_w, mod_b, tn=1536):
    depth, d, n = mod_w.shape
    return pl.pallas_call(
        _mod_kernel,
        grid=(depth, n // tn),
        in_specs=[
            pl.BlockSpec((MOD_ROWS, d), lambda i, j: (0, 0)),
            pl.BlockSpec((1, d, tn), lambda i, j: (i, 0, j)),
            pl.BlockSpec((1, 1, tn), lambda i, j: (i, 0, j)),
        ],
        out_specs=pl.BlockSpec((1, MOD_ROWS, tn), lambda i, j: (i, 0, j)),
        out_shape=jax.ShapeDtypeStruct((depth, MOD_ROWS, n), F32),
        compiler_params=_params(2),
        name="modulation",
    )(cs, mod_w, mod_b.reshape(depth, 1, n))


def _mod_spec(ctx_tile, batch):
    def index(b, j):
        return (jnp.where(j == ctx_tile, batch, b), 0, 0)
    return index


def _norm_mod(x, ln, mod, shift, scale):
    y = _rms(x, ln)
    return (y * (1.0 + mod[scale:scale + 1]) + mod[shift:shift + 1]).astype(BF16)


def _stream_specs(stream, ctx_tile, row_tile=ROW_TILE):
    d = stream[0].shape[2]
    if len(stream) == 1:
        return [pl.BlockSpec((1, row_tile, d), lambda b, j: (b, j, 0))]
    return [pl.BlockSpec((1, row_tile, d), lambda b, j: (b, jnp.minimum(j, ctx_tile - 1), 0)),
            pl.BlockSpec((1, row_tile, d), lambda b, j: (b, 0, 0))]


def _stream_tile(refs, ctx_tile):
    if len(refs) == 1:
        return refs[0][0]
    return jnp.where(pl.program_id(1) == ctx_tile, refs[1][0], refs[0][0])


def _qkv_kernel(*refs, n_stream, n_q, n_k, head_norm, q_scale, ctx_tile):
    mod_ref, ln_ref, w_ref, wvt_ref, cos_ref, sin_ref, qn_ref, kn_ref, q_ref, k_ref, vt_ref = refs[n_stream:]
    x = _stream_tile(refs[:n_stream], ctx_tile)
    h = _norm_mod(x, ln_ref[...], mod_ref[0], SH1, SC1)
    cos, sin = cos_ref[...], sin_ref[...]
    for head in range((n_q + n_k) // HEAD_W):
        is_q = head < n_q // HEAD_W
        if head % 2 == 0:
            pair = _dot(h, w_ref[:, head * HEAD_W:(head + 2) * HEAD_W])
        t = pair[:, (head % 2) * HEAD_W:(head % 2 + 1) * HEAD_W]
        if head_norm:
            t = _rms(t, qn_ref[...] if is_q else kn_ref[...])
        t = t * cos + pltpu.roll(t, HEAD_W // 2, 1) * sin
        if is_q:
            q_ref[0, :, head * HEAD_W:(head + 1) * HEAD_W] = (t * q_scale).astype(BF16)
        else:
            c0 = head * HEAD_W - n_q
            k_ref[0, :, c0:c0 + HEAD_W] = t.astype(BF16)
    vt_ref[0] = _dot_nt(wvt_ref[...], h).astype(BF16)


def _qkv_project(stream, mod, ln, w_qk, w_vt, cosf, sinf, qn, kn, *, n_q, n_k, head_norm, q_scale):
    batch, _, d = stream[0].shape
    t = cosf.shape[0]
    n_v = w_vt.shape[0]
    n_tiles = t // ROW_TILE
    row = lambda b, j: (b, j, 0)
    kern = functools.partial(_qkv_kernel, n_stream=len(stream), n_q=n_q, n_k=n_k, head_norm=head_norm,
                             q_scale=q_scale, ctx_tile=n_tiles - 1)
    return pl.pallas_call(
        kern,
        grid=(batch, n_tiles),
        in_specs=_stream_specs(stream, n_tiles - 1) + [
            pl.BlockSpec((1, 6, d), _mod_spec(n_tiles - 1, batch)),
            _resident((1, d)),
            _resident((d, n_q + n_k)),
            _resident((n_v, d)),
            pl.BlockSpec((ROW_TILE, HEAD_W), lambda b, j: (j, 0)),
            pl.BlockSpec((ROW_TILE, HEAD_W), lambda b, j: (j, 0)),
            _resident((1, HEAD_W)),
            _resident((1, HEAD_W)),
        ],
        out_specs=[
            pl.BlockSpec((1, ROW_TILE, n_q), row),
            pl.BlockSpec((1, ROW_TILE, n_k), row),
            pl.BlockSpec((1, n_v, ROW_TILE), lambda b, j: (b, 0, j)),
        ],
        out_shape=[
            jax.ShapeDtypeStruct((batch, t, n_q), BF16),
            jax.ShapeDtypeStruct((batch, t, n_k), BF16),
            jax.ShapeDtypeStruct((batch, n_v, t), BF16),
        ],
        compiler_params=_params(2),
        name="qkv_project",
    )(*stream, mod, ln, w_qk, w_vt, cosf, sinf, qn, kn)


def _chan_dft_kernel(*refs, n_stream, ctx_tile):
    mod_ref, ln_ref, w_ref, oc_ref, os_ref = refs[n_stream:]
    x = _stream_tile(refs[:n_stream], ctx_tile)
    h = _norm_mod(x, ln_ref[...], mod_ref[0], SH1, SC1)
    d = h.shape[1]
    gd = w_ref.shape[0]
    for g in range(d // gd):
        y = _dot(h[:, g * gd:(g + 1) * gd], w_ref[...])
        oc_ref[0, :, g * gd:(g + 1) * gd] = y[:, :gd].astype(BF16)
        os_ref[0, :, g * gd:(g + 1) * gd] = y[:, gd:].astype(BF16)


def _channel_dft(stream, t, mod, ln, w):
    batch, _, d = stream[0].shape
    n_tiles = t // ROW_TILE
    row = lambda b, j: (b, j, 0)
    return pl.pallas_call(
        functools.partial(_chan_dft_kernel, n_stream=len(stream), ctx_tile=n_tiles - 1),
        grid=(batch, n_tiles),
        in_specs=_stream_specs(stream, n_tiles - 1) + [
            pl.BlockSpec((1, 6, d), _mod_spec(n_tiles - 1, batch)),
            _resident((1, d)),
            _resident(w.shape),
        ],
        out_specs=[pl.BlockSpec((1, ROW_TILE, d), row)] * 2,
        out_shape=[jax.ShapeDtypeStruct((batch, t, d), BF16)] * 2,
        compiler_params=_params(2),
        name="channel_dft",
    )(*stream, mod, ln, w)


def _score_chunk(k_ref, s_scr, q, kvh, buf, c, k_lo, n_chunks, m8):
    tq = q.shape[0]
    rows = slice(k_lo + c * KEY_CHUNK, k_lo + (c + 1) * KEY_CHUNK)
    s = _dot_nt(k_ref[0, rows, kvh * HEAD_W:(kvh + 1) * HEAD_W], q)
    lo = (buf * n_chunks + c) * KEY_CHUNK
    s_scr[lo:lo + KEY_CHUNK, :] = s
    return jnp.maximum(m8, jnp.max(s.reshape(KEY_CHUNK // SUBLANES, SUBLANES, tq), axis=0))


def _softmax_pv_streams(score_list, weigh_list, offset, k_ref, vt_ref, s_scr, m8_carried, k_lo, n_keys,
                        on_done):
    tq = score_list[0][0].shape[0]
    n_chunks = n_keys // KEY_CHUNK
    ones_rows = jnp.ones((ONES_ROWS, KEY_CHUNK), BF16)

    def weigh(kvh, buf, c, mx, acc):
        lo = (buf * n_chunks + c) * KEY_CHUNK
        e = jnp.exp2(s_scr[lo:lo + KEY_CHUNK, :] - mx)
        cols = slice(k_lo + c * KEY_CHUNK, k_lo + (c + 1) * KEY_CHUNK)
        v_ones = jnp.concatenate([vt_ref[0, kvh * HEAD_W:(kvh + 1) * HEAD_W, cols], ones_rows], axis=0)
        return acc + _dot(v_ones, e.astype(BF16))

    m8 = [None] * len(score_list)
    mx = acc = None
    n_steps = max(len(score_list) * n_chunks, offset + len(weigh_list) * n_chunks)
    for g in range(n_steps):
        if g < len(score_list) * n_chunks:
            j, c = divmod(g, n_chunks)
            q, kvh, buf = score_list[j]
            if c == 0:
                m8[j] = jnp.full((SUBLANES, tq), -jnp.inf, F32)
            m8[j] = _score_chunk(k_ref, s_scr, q, kvh, buf, c, k_lo, n_chunks, m8[j])
        ge = g - offset
        if 0 <= ge < len(weigh_list) * n_chunks:
            p, c = divmod(ge, n_chunks)
            kvh, buf, max_src = weigh_list[p]
            if c == 0:
                src = m8_carried if max_src is None else m8[max_src]
                mx = jnp.max(src, axis=0, keepdims=True)
                acc = jnp.zeros((HEAD_W + ONES_ROWS, tq), F32)
            acc = weigh(kvh, buf, c, mx, acc)
            if c == n_chunks - 1:
                on_done(p, acc[:HEAD_W] * (1.0 / acc[HEAD_W:HEAD_W + 1]))
    return m8


def _head_problems(q_ref, diff, heads_per_step, kv_heads_per_step, n_heads=None):
    problems = []
    for hh in range(heads_per_step if n_heads is None else n_heads):
        q = q_ref[0, :, hh * HEAD_W:(hh + 1) * HEAD_W]
        kvh = hh * kv_heads_per_step // heads_per_step
        if diff:
            lane = lax.broadcasted_iota(jnp.int32, q.shape, 1)
            first = (lane % (HEAD_W // 2)) < (HEAD_W // 4)
            zero = jnp.zeros_like(q)
            problems += [(jnp.where(first, q, zero), kvh), (jnp.where(first, zero, q), kvh)]
        else:
            problems.append((q, kvh))
    return problems


def _head_writer(o_ref, lam_ref, subln_ref, diff, lambda_init):
    first_comp = {}

    def on_done(p, o_t):
        if not diff:
            o_ref[0, :, p * HEAD_W:(p + 1) * HEAD_W] = o_t.T.astype(BF16)
            return
        hh = p // 2
        if p % 2 == 0:
            first_comp[hh] = o_t
            return
        lv = lam_ref[...]
        lam = (jnp.exp(jnp.sum(lv[0:1] * lv[1:2], axis=-1, keepdims=True))
               - jnp.exp(jnp.sum(lv[2:3] * lv[3:4], axis=-1, keepdims=True)) + lambda_init)
        o = (first_comp.pop(hh) - lam * o_t).T
        o = _rms(o, subln_ref[...]) * (1.0 - lambda_init)
        o_ref[0, :, hh * HEAD_W:(hh + 1) * HEAD_W] = o.astype(BF16)

    return on_done


def _attn_latent_kernel(lam_ref, subln_ref, q_ref, qn_ref, k_ref, vt_ref, o_ref, s_scr, m8_scr,
                        *, diff, lambda_init, heads_per_step, kv_heads_per_step, n_keys):
    n_chunks = n_keys // KEY_CHUNK
    cur = _head_problems(q_ref, diff, heads_per_step, kv_heads_per_step)
    nxt_q, nxt_kvh = _head_problems(qn_ref, diff, heads_per_step, kv_heads_per_step, n_heads=1)[0]
    n_prob = len(cur)
    assert n_prob % 2 == 0

    @pl.when(pl.program_id(2) == 0)
    def _():
        q, kvh = cur[0]
        m8 = jnp.full((SUBLANES, q.shape[0]), -jnp.inf, F32)
        for c in range(n_chunks):
            m8 = _score_chunk(k_ref, s_scr, q, kvh, 0, c, 0, n_chunks, m8)
        m8_scr[...] = m8

    score_list = [(q, kvh, p % 2) for p, (q, kvh) in enumerate(cur) if p > 0] + [(nxt_q, nxt_kvh, 0)]
    weigh_list = [(kvh, p % 2, None if p == 0 else p - 1) for p, (_, kvh) in enumerate(cur)]
    m8 = _softmax_pv_streams(
        score_list, weigh_list, min(SCORE_LEAD, n_chunks - 1), k_ref, vt_ref, s_scr, m8_scr[...], 0, n_keys,
        _head_writer(o_ref, lam_ref, subln_ref, diff, lambda_init))
    m8_scr[...] = m8[-1]


def _attn_ctx_kernel(lam_ref, subln_ref, q_ref, k_ref, vt_ref, o_ref, s_scr,
                     *, diff, lambda_init, heads_per_step, kv_heads_per_step, n_keys):
    n_chunks = n_keys // KEY_CHUNK
    cur = _head_problems(q_ref, diff, heads_per_step, kv_heads_per_step)
    score_list = [(q, kvh, p % 2) for p, (q, kvh) in enumerate(cur)]
    weigh_list = [(kvh, p % 2, p) for p, (_, kvh) in enumerate(cur)]
    _softmax_pv_streams(
        score_list, weigh_list, n_chunks + min(SCORE_LEAD, n_chunks - 1), k_ref, vt_ref, s_scr, None, 0, n_keys,
        _head_writer(o_ref, lam_ref, subln_ref, diff, lambda_init))


def _attention(q, k, vt, lam, subln, *, diff, lambda_init, n_heads, heads_per_step, kv_heads_per_step,
               n_lat, need_ctx):
    batch, t, _ = q.shape
    n_ctx = t - n_lat
    qw = heads_per_step * HEAD_W
    kw = kv_heads_per_step * HEAD_W
    n_groups = n_heads // heads_per_step
    n_lat_tiles = n_lat // ROW_TILE
    ctx_tile = n_lat // n_ctx
    static = dict(diff=diff, lambda_init=lambda_init, heads_per_step=heads_per_step,
                  kv_heads_per_step=kv_heads_per_step)
    name = "diff_attention" if diff else "gqa_attention"
    o_lat = pl.pallas_call(
        functools.partial(_attn_latent_kernel, n_keys=t, **static),
        grid=(batch, n_groups, n_lat_tiles),
        in_specs=[
            _resident(lam.shape),
            _resident(subln.shape),
            pl.BlockSpec((1, ROW_TILE, qw), lambda b, h, i: (b, i, h)),
            pl.BlockSpec((1, ROW_TILE, qw), lambda b, h, i: (b, jnp.minimum(i + 1, n_lat_tiles - 1), h)),
            pl.BlockSpec((1, t, kw), lambda b, h, i: (b, 0, h)),
            pl.BlockSpec((1, kw, t), lambda b, h, i: (b, h, 0)),
        ],
        out_specs=pl.BlockSpec((1, ROW_TILE, qw), lambda b, h, i: (b, i, h)),
        out_shape=jax.ShapeDtypeStruct((batch, n_lat, n_heads * HEAD_W), BF16),
        scratch_shapes=[pltpu.VMEM((2 * t, ROW_TILE), F32), pltpu.VMEM((SUBLANES, ROW_TILE), F32)],
        compiler_params=_params(3),
        name=name,
    )(lam, subln, q, q, k, vt)
    if not need_ctx:
        return o_lat, None
    o_ctx = pl.pallas_call(
        functools.partial(_attn_ctx_kernel, n_keys=n_ctx, **static),
        grid=(batch, n_groups),
        in_specs=[
            _resident(lam.shape),
            _resident(subln.shape),
            pl.BlockSpec((1, n_ctx, qw), lambda b, h: (b, ctx_tile, h)),
            pl.BlockSpec((1, n_ctx, kw), lambda b, h: (b, ctx_tile, h)),
            pl.BlockSpec((1, kw, n_ctx), lambda b, h: (b, h, ctx_tile)),
        ],
        out_specs=pl.BlockSpec((1, n_ctx, qw), lambda b, h: (b, 0, h)),
        out_shape=jax.ShapeDtypeStruct((batch, n_ctx, n_heads * HEAD_W), BF16),
        scratch_shapes=[pltpu.VMEM((2 * n_ctx, n_ctx), F32)],
        compiler_params=_params(2),
        name=name + "_ctx",
    )(lam, subln, q, k, vt)
    return o_lat, o_ctx


DFT_COL_TILE = 512
DFT_EXTRA_ROWS = 16


def _dft_mirror_kernel(cl_ref, sl_ref, ch_ref, flip_ref, xc_ref, xs_ref, o_ref, carry_scr, *, n_steps):
    s = pl.program_id(2)
    t = n_steps - 1 - s

    @pl.when(s == 0)
    def _():
        carry_scr[...] = _dot(ch_ref[...], xc_ref[0])

    a = _dot(cl_ref[...], xc_ref[0])
    b = _dot(sl_ref[...], xs_ref[0])
    total = a + b
    row = lax.broadcasted_iota(jnp.int32, total.shape, 0)
    mirror = _dot(flip_ref[...], total.astype(BF16))
    mirror = jnp.where(row == 0, carry_scr[0:1, :], mirror)
    carry_scr[0:1, :] = total[0:1]
    o_ref[0, pl.ds(pl.multiple_of(t * ROW_TILE, ROW_TILE), ROW_TILE), :] = (a - b).astype(BF16)
    lo = pl.multiple_of((2 * n_steps - 1 - t) * ROW_TILE, ROW_TILE)
    o_ref[0, pl.ds(lo, ROW_TILE), :] = mirror.astype(BF16)


def _dft_ctx_kernel(c_ref, s_ref, xc_ref, xs_ref, o_ref):
    o_ref[0] = (_dot(c_ref[...], xc_ref[0]) + _dot(s_ref[...], xs_ref[0])).astype(BF16)


def _position_dft(xc, xs, n_lat, need_ctx):
    batch, t, d = xc.shape
    n_ctx = t - n_lat
    half = n_lat // 2
    n_steps = half // ROW_TILE
    cl, sl = _dft_tables(n_lat, n_lat ** -0.5, n_rows=half + DFT_EXTRA_ROWS, negate_sin=False)
    c_half = cl[half:]
    cl, sl = cl[:half], sl[:half]
    flip = np.zeros((ROW_TILE, ROW_TILE), np.float32)
    flip[np.arange(1, ROW_TILE), ROW_TILE - np.arange(1, ROW_TILE)] = 1.0
    tile_rows = lambda b, c, s: (n_steps - 1 - s, 0)
    f_lat = pl.pallas_call(
        functools.partial(_dft_mirror_kernel, n_steps=n_steps),
        grid=(batch, d // DFT_COL_TILE, n_steps),
        in_specs=[
            pl.BlockSpec((ROW_TILE, n_lat), tile_rows),
            pl.BlockSpec((ROW_TILE, n_lat), tile_rows),
            _resident((DFT_EXTRA_ROWS, n_lat)),
            _resident((ROW_TILE, ROW_TILE)),
            pl.BlockSpec((1, n_lat, DFT_COL_TILE), lambda b, c, s: (b, 0, c)),
            pl.BlockSpec((1, n_lat, DFT_COL_TILE), lambda b, c, s: (b, 0, c)),
        ],
        out_specs=pl.BlockSpec((1, n_lat, DFT_COL_TILE), lambda b, c, s: (b, 0, c)),
        out_shape=jax.ShapeDtypeStruct((batch, n_lat, d), BF16),
        scratch_shapes=[pltpu.VMEM((DFT_EXTRA_ROWS, DFT_COL_TILE), F32)],
        compiler_params=_params(3),
        name="position_dft",
    )(cl, sl, c_half, jnp.asarray(flip, dtype=BF16), xc, xs)
    if not need_ctx:
        return f_lat, None
    cc, sc = _dft_tables(n_ctx, n_ctx ** -0.5)
    ctx_block = n_lat // n_ctx
    f_ctx = pl.pallas_call(
        _dft_ctx_kernel,
        grid=(batch,),
        in_specs=[
            _resident((n_ctx, n_ctx)),
            _resident((n_ctx, n_ctx)),
            pl.BlockSpec((1, n_ctx, d), lambda b: (b, ctx_block, 0)),
            pl.BlockSpec((1, n_ctx, d), lambda b: (b, ctx_block, 0)),
        ],
        out_specs=pl.BlockSpec((1, n_ctx, d), lambda b: (b, 0, 0)),
        out_shape=jax.ShapeDtypeStruct((batch, n_ctx, d), BF16),
        compiler_params=_params(1),
        name="position_dft_ctx",
    )(cc, sc, xc, xs)
    return f_lat, f_ctx


def _out_ffn_kernel(*refs, n_mixed, n_stream, d_ff, final_norm, ctx_tile):
    n_in = n_mixed + n_stream
    wo_ref, bo_ref, mod_ref, ln_ref, wgu_ref, wd_ref, fn_ref, o_ref, a_scr = refs[n_in:]
    mod = mod_ref[0]
    a = _stream_tile(refs[:n_mixed], ctx_tile)
    x = _stream_tile(refs[n_mixed:n_in], ctx_tile) + mod[G1:G1 + 1] * (_dot(a, wo_ref[...]) + bo_ref[...])
    h = _norm_mod(x, ln_ref[...], mod, SH2, SC2)
    for lo in range(0, d_ff, FF_CHUNK):
        hi = min(lo + FF_CHUNK, d_ff)
        g = _dot(h, wgu_ref[:, lo:hi])
        u = _dot(h, wgu_ref[:, d_ff + lo:d_ff + hi])
        a_scr[:, lo:hi] = (_silu(g) * u).astype(BF16)
    y = x + mod[G2:G2 + 1] * _dot(a_scr[...], wd_ref[...])
    if final_norm:
        y = _rms(y, fn_ref[...])
    o_ref[0] = y


def _out_ffn(mixed, w_o, b_o, stream, mod, ln, w_gu, w_down, final_g, n_tiles, ctx_tile, final_norm,
             row_tile=ROW_TILE):
    batch, _, d = stream[0].shape
    k = mixed[0].shape[2]
    d_ff = w_down.shape[0]
    row = lambda b, j: (b, j, 0)
    kern = functools.partial(_out_ffn_kernel, n_mixed=len(mixed), n_stream=len(stream), d_ff=d_ff,
                             final_norm=final_norm, ctx_tile=ctx_tile)
    return pl.pallas_call(
        kern,
        grid=(batch, n_tiles),
        in_specs=_stream_specs(mixed, ctx_tile, row_tile) + _stream_specs(stream, ctx_tile, row_tile) + [
            _resident((k, d)),
            _resident((1, d)),
            pl.BlockSpec((1, 6, d), _mod_spec(ctx_tile, batch)),
            _resident((1, d)),
            _resident((d, 2 * d_ff)),
            _resident((d_ff, d)),
            _resident((1, d)),
        ],
        out_specs=pl.BlockSpec((1, row_tile, d), row),
        out_shape=jax.ShapeDtypeStruct((batch, n_tiles * row_tile, d), F32),
        scratch_shapes=[pltpu.VMEM((row_tile, d_ff), BF16)],
        compiler_params=_params(2),
        name="out_ffn",
    )(*mixed, *stream, w_o, b_o, mod, ln, w_gu, w_down, final_g)


def _half_split_perm(n_comp, comp_dim):
    src = np.zeros(HEAD_W, np.int32)
    for m in range(n_comp):
        for i in range(comp_dim // 2):
            for p in range(2):
                src[p * (HEAD_W // 2) + m * (comp_dim // 2) + i] = m * comp_dim + 2 * i + p
    return src


def _rope_lane_tables(seq, n_ctx, head_dim):
    axis_dim = head_dim // 2
    rows = jnp.arange(seq, dtype=jnp.int32) // GRID_W
    cols = jnp.arange(seq, dtype=jnp.int32) % GRID_W
    inv_freq = ROPE_THETA ** (-jnp.arange(0, axis_dim, 2, dtype=F32) / axis_dim)
    ang = jnp.concatenate([rows[:, None].astype(F32) * inv_freq,
                           cols[:, None].astype(F32) * inv_freq], axis=-1)
    reps = (HEAD_W // 2) // (head_dim // 2)
    cos = jnp.tile(jnp.cos(ang), (1, 2 * reps))
    sin = jnp.tile(jnp.sin(ang), (1, reps))
    sin = jnp.concatenate([-sin, sin], axis=-1)
    cos = jnp.concatenate([cos, jnp.ones((n_ctx, HEAD_W), F32)], axis=0)
    sin = jnp.concatenate([sin, jnp.zeros((n_ctx, HEAD_W), F32)], axis=0)
    return cos, sin


def _dft_tables(n, scale, n_rows=None, negate_sin=True):
    n_rows = n if n_rows is None else n_rows
    r = 1
    while r * r < n:
        r *= 2
    q = n // r
    k = jnp.arange(n_rows, dtype=jnp.int32)[:, None]
    a_idx = (k * jnp.arange(q, dtype=jnp.int32)[None, :] * r) % n
    b_idx = (k * jnp.arange(r, dtype=jnp.int32)[None, :]) % n
    w = 2.0 * math.pi / n
    ca, sa = jnp.cos(a_idx.astype(F32) * w), jnp.sin(a_idx.astype(F32) * w)
    cb, sb = jnp.cos(b_idx.astype(F32) * w) * scale, jnp.sin(b_idx.astype(F32) * w) * scale
    cos = ca[:, :, None] * cb[:, None, :] - sa[:, :, None] * sb[:, None, :]
    sin = sa[:, :, None] * cb[:, None, :] + ca[:, :, None] * sb[:, None, :]
    sin = -sin if negate_sin else sin
    return cos.reshape(n_rows, n).astype(BF16), sin.reshape(n_rows, n).astype(BF16)


def _channel_dft_weight(gd):
    idx = (np.arange(gd)[:, None] * np.arange(gd)[None, :]) % gd
    ang = 2.0 * np.pi * idx / gd
    w = np.concatenate([np.cos(ang), np.sin(ang)], axis=1) / math.sqrt(gd)
    return jnp.asarray(w, dtype=BF16)


def kernel(x, c, ctx, c_ctx, mod_w, mod_b, ln_mix, ln_ffn, ffn_w_gu, ffn_w_down, a_w_qkv, a_lam, a_subln, a_w_o, b_w_qkv, b_q_norm, b_k_norm, b_w_o, c_w_o, c_b_o, final_norm):
    batch, seq, d = x.shape
    n_ctx = ctx.shape[1]
    depth = mod_w.shape[0]
    t = seq + n_ctx
    assert seq % ROW_TILE == 0 and n_ctx == ROW_TILE and batch < MOD_ROWS
    n_tiles_all = t // ROW_TILE
    n_tiles_lat = seq // ROW_TILE
    ctx_tile = n_tiles_all - 1

    cs = jnp.concatenate([c, c_ctx[None, :], jnp.zeros((MOD_ROWS - batch - 1, d), F32)], axis=0)
    mod_all = _modulation(cs, mod_w, mod_b).reshape(depth, MOD_ROWS, 6, d)

    stream = (x, ctx)

    perm_a = _half_split_perm(2, DIFF_HEAD_DIM)
    perm_b = _half_split_perm(1, GQA_HEAD_DIM)
    rope_a = _rope_lane_tables(seq, n_ctx, DIFF_HEAD_DIM)
    rope_b = _rope_lane_tables(seq, n_ctx, GQA_HEAD_DIM)
    ones_head = jnp.ones((1, HEAD_W), F32)
    no_bias = jnp.zeros((1, d), F32)

    for i in range(depth):
        last = i == depth - 1
        kind, j = i % N_MIXERS, i // N_MIXERS
        mod = mod_all[i]
        ln1 = ln_mix[i][None, :]
        n_out_tiles = n_tiles_lat if last else n_tiles_all

        if kind == 0:
            n_q = DIFF_HEADS * HEAD_W
            cols = (np.arange(2 * DIFF_HEADS)[:, None] * HEAD_W + perm_a[None, :]).reshape(-1)
            w = a_w_qkv[j]
            w_qk = w[:, cols].astype(BF16)
            w_vt = w[:, 2 * n_q:].T.astype(BF16)
            q, k, vt = _qkv_project(
                stream, mod, ln1, w_qk, w_vt, rope_a[0], rope_a[1], ones_head, ones_head,
                n_q=n_q, n_k=n_q, head_norm=False, q_scale=DIFF_HEAD_DIM ** -0.5 * LOG2E)
            mixed, mixed_ctx = _attention(
                q, k, vt, a_lam[j], a_subln[j][None, :], diff=True, lambda_init=_lambda_init(i),
                n_heads=DIFF_HEADS, heads_per_step=4, kv_heads_per_step=4, n_lat=seq, need_ctx=not last)
            w_o, b_o = a_w_o[j], no_bias
        elif kind == 1:
            n_q = GQA_HEADS * HEAD_W
            n_k = GQA_KV_HEADS * HEAD_W
            cols = (np.arange(GQA_HEADS + GQA_KV_HEADS)[:, None] * HEAD_W + perm_b[None, :]).reshape(-1)
            w = b_w_qkv[j]
            w_qk = w[:, cols].astype(BF16)
            w_vt = w[:, n_q + n_k:].T.astype(BF16)
            q, k, vt = _qkv_project(
                stream, mod, ln1, w_qk, w_vt, rope_b[0], rope_b[1],
                b_q_norm[j][perm_b][None, :], b_k_norm[j][perm_b][None, :],
                n_q=n_q, n_k=n_k, head_norm=True, q_scale=GQA_HEAD_DIM ** -0.5 * LOG2E)
            mixed, mixed_ctx = _attention(
                q, k, vt, jnp.zeros((4, DIFF_HEAD_DIM), F32), ones_head, diff=False, lambda_init=0.0,
                n_heads=GQA_HEADS, heads_per_step=GQA_HEADS, kv_heads_per_step=GQA_KV_HEADS,
                n_lat=seq, need_ctx=not last)
            w_o, b_o = b_w_o[j], no_bias
        else:
            x_cos, x_sin = _channel_dft(stream, t, mod, ln1, _channel_dft_weight(d // FNET_GROUPS))
            mixed, mixed_ctx = _position_dft(x_cos, x_sin, seq, need_ctx=not last)
            w_o, b_o = c_w_o[j], c_b_o[j][None, :]

        ffn_weights = (ln_ffn[i][None, :], ffn_w_gu[i].astype(BF16), ffn_w_down[i].astype(BF16),
                       final_norm[None, :])
        if last:
            n_big = seq // BIG_ROW_TILE
            xs = _out_ffn((mixed,), w_o.astype(BF16), b_o, stream, mod, *ffn_weights, n_big, n_big,
                          final_norm=True, row_tile=BIG_ROW_TILE)
        else:
            xs = _out_ffn((mixed, mixed_ctx), w_o.astype(BF16), b_o, stream, mod, *ffn_weights,
                          n_tiles_all, ctx_tile, final_norm=False)
        stream = (xs,)
    return xs
```

```python
import functools
import math

import numpy as np
import jax
import jax.numpy as jnp
from jax import lax
from jax.experimental import pallas as pl
from jax.experimental.pallas import tpu as pltpu

GRID_W = 64
ROPE_THETA = 10000.0
NORM_EPS = 1e-6
DIFF_HEADS = 8
DIFF_HEAD_DIM = 64
GQA_HEADS = 8
GQA_KV_HEADS = 2
GQA_HEAD_DIM = 128
FNET_GROUPS = 4
N_MIXERS = 3

LANES = 128
SUBLANES = 8
VMEM_LIMIT = 56 * 1024 * 1024

ROW_TILE = 256
BIG_ROW_TILE = 512
HEAD_W = 128
KEY_CHUNK = 256
SCORE_LEAD = 4
FF_CHUNK = 256
ONES_ROWS = 16
MOD_ROWS = 16
LOG2E = math.log2(math.e)

BF16 = jnp.bfloat16
F32 = jnp.float32

SH1, SC1, G1, SH2, SC2, G2 = range(6)


def _lambda_init(layer_idx):
    return 0.8 - 0.6 * float(np.exp(-0.3 * layer_idx))


def _params(n_grid):
    return pltpu.CompilerParams(
        dimension_semantics=("arbitrary",) * n_grid, vmem_limit_bytes=VMEM_LIMIT)


def _resident(shape):
    nd = len(shape)
    return pl.BlockSpec(shape, lambda *_: (0,) * nd, pipeline_mode=pl.Buffered(1))


def _dot(a, b):
    return jnp.dot(a, b, preferred_element_type=F32)


def _dot_nt(a, b):
    return lax.dot_general(a, b, (((1,), (1,)), ((), ())), preferred_element_type=F32)


def _rms(x, g):
    return x * lax.rsqrt(jnp.mean(x * x, axis=-1, keepdims=True) + NORM_EPS) * g


def _silu(x):
    return x / (1.0 + jnp.exp(-x))


def _mod_kernel(cs_ref, w_ref, b_ref, o_ref):
    s = _silu(cs_ref[...]).astype(BF16)
    o_ref[0] = _dot(s, w_ref[0].astype(BF16)) + b_ref[0]


def _modulation(cs, mod_w, mod_b, tn=1536):
    depth, d, n = mod_w.shape
    return pl.pallas_call(
        _mod_kernel,
        grid=(depth, n // tn),
        in_specs=[
            pl.BlockSpec((MOD_ROWS, d), lambda i, j: (0, 0)),
            pl.BlockSpec((1, d, tn), lambda i, j: (i, 0, j)),
            pl.BlockSpec((1, 1, tn), lambda i, j: (i, 0, j)),
        ],
        out_specs=pl.BlockSpec((1, MOD_ROWS, tn), lambda i, j: (i, 0, j)),
        out_shape=jax.ShapeDtypeStruct((depth, MOD_ROWS, n), F32),
        compiler_params=_params(2),
        name="modulation",
    )(cs, mod_w, mod_b.reshape(depth, 1, n))


def _mod_spec(d, mod_row):
    if mod_row is None:
        return pl.BlockSpec((1, 6, d), lambda b, j: (b, 0, 0))
    return pl.BlockSpec((1, 6, d), lambda b, j: (mod_row, 0, 0))


def _norm_mod(x, ln, mod, shift, scale):
    y = _rms(x, ln)
    return (y * (1.0 + mod[scale:scale + 1]) + mod[shift:shift + 1]).astype(BF16)


def _qkv_kernel(x_ref, mod_ref, ln_ref, w_ref, wvt_ref, cos_ref, sin_ref, qn_ref, kn_ref,
                q_ref, k_ref, vt_ref, *, n_q, n_k, head_norm, q_scale):
    h = _norm_mod(x_ref[0], ln_ref[...], mod_ref[0], SH1, SC1)
    cos, sin = cos_ref[...], sin_ref[...]
    for head in range((n_q + n_k) // HEAD_W):
        is_q = head < n_q // HEAD_W
        if head % 2 == 0:
            pair = _dot(h, w_ref[:, head * HEAD_W:(head + 2) * HEAD_W])
        t = pair[:, (head % 2) * HEAD_W:(head % 2 + 1) * HEAD_W]
        if head_norm:
            t = _rms(t, qn_ref[...] if is_q else kn_ref[...])
        t = t * cos + pltpu.roll(t, HEAD_W // 2, 1) * sin
        if is_q:
            q_ref[0, :, head * HEAD_W:(head + 1) * HEAD_W] = (t * q_scale).astype(BF16)
        else:
            c0 = head * HEAD_W - n_q
            k_ref[0, :, c0:c0 + HEAD_W] = t.astype(BF16)
    vt_ref[0] = _dot_nt(wvt_ref[...], h).astype(BF16)


def _qkv_project(x, mod, mod_row, ln, w_qk, w_vt, cosf, sinf, qn, kn, *, row_tile, n_q, n_k, head_norm,
                 q_scale):
    batch, rows, d = x.shape
    n_v = w_vt.shape[0]
    row = lambda b, j: (b, j, 0)
    kern = functools.partial(_qkv_kernel, n_q=n_q, n_k=n_k, head_norm=head_norm, q_scale=q_scale)
    return pl.pallas_call(
        kern,
        grid=(batch, rows // row_tile),
        in_specs=[
            pl.BlockSpec((1, row_tile, d), row),
            _mod_spec(d, mod_row),
            _resident((1, d)),
            _resident((d, n_q + n_k)),
            _resident((n_v, d)),
            pl.BlockSpec((row_tile, HEAD_W), lambda b, j: (j, 0)),
            pl.BlockSpec((row_tile, HEAD_W), lambda b, j: (j, 0)),
            _resident((1, HEAD_W)),
            _resident((1, HEAD_W)),
        ],
        out_specs=[
            pl.BlockSpec((1, row_tile, n_q), row),
            pl.BlockSpec((1, row_tile, n_k), row),
            pl.BlockSpec((1, n_v, row_tile), lambda b, j: (b, 0, j)),
        ],
        out_shape=[
            jax.ShapeDtypeStruct((batch, rows, n_q), BF16),
            jax.ShapeDtypeStruct((batch, rows, n_k), BF16),
            jax.ShapeDtypeStruct((batch, n_v, rows), BF16),
        ],
        compiler_params=_params(2),
        name="qkv_project",
    )(x, mod, ln, w_qk, w_vt, cosf, sinf, qn, kn)


def _chan_dft_kernel(x_ref, mod_ref, ln_ref, w_ref, oc_ref, os_ref):
    h = _norm_mod(x_ref[0], ln_ref[...], mod_ref[0], SH1, SC1)
    d = h.shape[1]
    gd = w_ref.shape[0]
    for g in range(d // gd):
        y = _dot(h[:, g * gd:(g + 1) * gd], w_ref[...])
        oc_ref[0, :, g * gd:(g + 1) * gd] = y[:, :gd].astype(BF16)
        os_ref[0, :, g * gd:(g + 1) * gd] = y[:, gd:].astype(BF16)


def _channel_dft(x, mod, mod_row, ln, w, row_tile):
    batch, rows, d = x.shape
    row = lambda b, j: (b, j, 0)
    return pl.pallas_call(
        _chan_dft_kernel,
        grid=(batch, rows // row_tile),
        in_specs=[
            pl.BlockSpec((1, row_tile, d), row),
            _mod_spec(d, mod_row),
            _resident((1, d)),
            _resident(w.shape),
        ],
        out_specs=[pl.BlockSpec((1, row_tile, d), row)] * 2,
        out_shape=[jax.ShapeDtypeStruct((batch, rows, d), BF16)] * 2,
        compiler_params=_params(2),
        name="channel_dft",
    )(x, mod, ln, w)


def _kv_chunks(klat_ref, kctx_ref, vtlat_ref, vtctx_ref, n_lat_chunks):
    def pick(lat_ref, ctx_ref, c):
        return (lat_ref, c) if c < n_lat_chunks else (ctx_ref, c - n_lat_chunks)

    def key_chunk(c, kvh):
        ref, cc = pick(klat_ref, kctx_ref, c)
        return ref[0, cc * KEY_CHUNK:(cc + 1) * KEY_CHUNK, kvh * HEAD_W:(kvh + 1) * HEAD_W]

    def val_chunk(c, kvh):
        ref, cc = pick(vtlat_ref, vtctx_ref, c)
        return ref[0, kvh * HEAD_W:(kvh + 1) * HEAD_W, cc * KEY_CHUNK:(cc + 1) * KEY_CHUNK]

    return key_chunk, val_chunk


def _score_chunk(key_chunk, s_scr, q, kvh, buf, c, n_chunks, m8):
    tq = q.shape[0]
    s = _dot_nt(key_chunk(c, kvh), q)
    lo = (buf * n_chunks + c) * KEY_CHUNK
    s_scr[lo:lo + KEY_CHUNK, :] = s
    return jnp.maximum(m8, jnp.max(s.reshape(KEY_CHUNK // SUBLANES, SUBLANES, tq), axis=0))


def _softmax_pv_streams(score_list, weigh_list, offset, key_chunk, val_chunk, s_scr, m8_carried, n_chunks,
                        on_done):
    tq = score_list[0][0].shape[0]
    ones_rows = jnp.ones((ONES_ROWS, KEY_CHUNK), BF16)

    def weigh(kvh, buf, c, mx, acc):
        lo = (buf * n_chunks + c) * KEY_CHUNK
        e = jnp.exp2(s_scr[lo:lo + KEY_CHUNK, :] - mx)
        v_ones = jnp.concatenate([val_chunk(c, kvh), ones_rows], axis=0)
        return acc + _dot(v_ones, e.astype(BF16))

    m8 = [None] * len(score_list)
    mx = acc = None
    n_steps = max(len(score_list) * n_chunks, offset + len(weigh_list) * n_chunks)
    for g in range(n_steps):
        if g < len(score_list) * n_chunks:
            j, c = divmod(g, n_chunks)
            q, kvh, buf = score_list[j]
            if c == 0:
                m8[j] = jnp.full((SUBLANES, tq), -jnp.inf, F32)
            m8[j] = _score_chunk(key_chunk, s_scr, q, kvh, buf, c, n_chunks, m8[j])
        ge = g - offset
        if 0 <= ge < len(weigh_list) * n_chunks:
            p, c = divmod(ge, n_chunks)
            kvh, buf, max_src = weigh_list[p]
            if c == 0:
                src = m8_carried if max_src is None else m8[max_src]
                mx = jnp.max(src, axis=0, keepdims=True)
                acc = jnp.zeros((HEAD_W + ONES_ROWS, tq), F32)
            acc = weigh(kvh, buf, c, mx, acc)
            if c == n_chunks - 1:
                on_done(p, acc[:HEAD_W] * (1.0 / acc[HEAD_W:HEAD_W + 1]))
    return m8


def _head_problems(q_ref, diff, heads_per_step, kv_heads_per_step, n_heads=None):
    problems = []
    for hh in range(heads_per_step if n_heads is None else n_heads):
        q = q_ref[0, :, hh * HEAD_W:(hh + 1) * HEAD_W]
        kvh = hh * kv_heads_per_step // heads_per_step
        if diff:
            lane = lax.broadcasted_iota(jnp.int32, q.shape, 1)
            first = (lane % (HEAD_W // 2)) < (HEAD_W // 4)
            zero = jnp.zeros_like(q)
            problems += [(jnp.where(first, q, zero), kvh), (jnp.where(first, zero, q), kvh)]
        else:
            problems.append((q, kvh))
    return problems


def _head_writer(o_ref, lam_ref, subln_ref, diff, lambda_init):
    first_comp = {}

    def on_done(p, o_t):
        if not diff:
            o_ref[0, :, p * HEAD_W:(p + 1) * HEAD_W] = o_t.T.astype(BF16)
            return
        hh = p // 2
        if p % 2 == 0:
            first_comp[hh] = o_t
            return
        lv = lam_ref[...]
        lam = (jnp.exp(jnp.sum(lv[0:1] * lv[1:2], axis=-1, keepdims=True))
               - jnp.exp(jnp.sum(lv[2:3] * lv[3:4], axis=-1, keepdims=True)) + lambda_init)
        o = (first_comp.pop(hh) - lam * o_t).T
        o = _rms(o, subln_ref[...]) * (1.0 - lambda_init)
        o_ref[0, :, hh * HEAD_W:(hh + 1) * HEAD_W] = o.astype(BF16)

    return on_done


def _attn_latent_kernel(lam_ref, subln_ref, q_ref, qn_ref, klat_ref, kctx_ref, vtlat_ref, vtctx_ref, o_ref,
                        s_scr, m8_scr, *, diff, lambda_init, heads_per_step, kv_heads_per_step):
    n_lat_chunks = klat_ref.shape[1] // KEY_CHUNK
    n_chunks = n_lat_chunks + kctx_ref.shape[1] // KEY_CHUNK
    key_chunk, val_chunk = _kv_chunks(klat_ref, kctx_ref, vtlat_ref, vtctx_ref, n_lat_chunks)
    cur = _head_problems(q_ref, diff, heads_per_step, kv_heads_per_step)
    nxt_q, nxt_kvh = _head_problems(qn_ref, diff, heads_per_step, kv_heads_per_step, n_heads=1)[0]
    n_prob = len(cur)
    assert n_prob % 2 == 0

    @pl.when(pl.program_id(2) == 0)
    def _():
        q, kvh = cur[0]
        m8 = jnp.full((SUBLANES, q.shape[0]), -jnp.inf, F32)
        for c in range(n_chunks):
            m8 = _score_chunk(key_chunk, s_scr, q, kvh, 0, c, n_chunks, m8)
        m8_scr[...] = m8

    score_list = [(q, kvh, p % 2) for p, (q, kvh) in enumerate(cur) if p > 0] + [(nxt_q, nxt_kvh, 0)]
    weigh_list = [(kvh, p % 2, None if p == 0 else p - 1) for p, (_, kvh) in enumerate(cur)]
    m8 = _softmax_pv_streams(
        score_list, weigh_list, min(SCORE_LEAD, n_chunks - 1), key_chunk, val_chunk, s_scr, m8_scr[...],
        n_chunks, _head_writer(o_ref, lam_ref, subln_ref, diff, lambda_init))
    m8_scr[...] = m8[-1]


def _attn_ctx_kernel(lam_ref, subln_ref, q_ref, k_ref, vt_ref, o_ref, s_scr,
                     *, diff, lambda_init, heads_per_step, kv_heads_per_step):
    n_chunks = k_ref.shape[1] // KEY_CHUNK
    key_chunk, val_chunk = _kv_chunks(None, k_ref, None, vt_ref, 0)
    cur = _head_problems(q_ref, diff, heads_per_step, kv_heads_per_step)
    score_list = [(q, kvh, p % 2) for p, (q, kvh) in enumerate(cur)]
    weigh_list = [(kvh, p % 2, p) for p, (_, kvh) in enumerate(cur)]
    _softmax_pv_streams(
        score_list, weigh_list, n_chunks + min(SCORE_LEAD, n_chunks - 1), key_chunk, val_chunk, s_scr, None,
        n_chunks, _head_writer(o_ref, lam_ref, subln_ref, diff, lambda_init))


def _attention(q, k, vt, q_ctx, k_ctx, vt_ctx, lam, subln, *, diff, lambda_init, n_heads, heads_per_step,
               kv_heads_per_step, need_ctx):
    batch, n_lat, _ = q.shape
    n_ctx = k_ctx.shape[1]
    qw = heads_per_step * HEAD_W
    kw = kv_heads_per_step * HEAD_W
    n_groups = n_heads // heads_per_step
    n_lat_tiles = n_lat // ROW_TILE
    static = dict(diff=diff, lambda_init=lambda_init, heads_per_step=heads_per_step,
                  kv_heads_per_step=kv_heads_per_step)
    name = "diff_attention" if diff else "gqa_attention"
    o_lat = pl.pallas_call(
        functools.partial(_attn_latent_kernel, **static),
        grid=(batch, n_groups, n_lat_tiles),
        in_specs=[
            _resident(lam.shape),
            _resident(subln.shape),
            pl.BlockSpec((1, ROW_TILE, qw), lambda b, h, i: (b, i, h)),
            pl.BlockSpec((1, ROW_TILE, qw), lambda b, h, i: (b, jnp.minimum(i + 1, n_lat_tiles - 1), h)),
            pl.BlockSpec((1, n_lat, kw), lambda b, h, i: (b, 0, h)),
            pl.BlockSpec((1, n_ctx, kw), lambda b, h, i: (b, 0, h)),
            pl.BlockSpec((1, kw, n_lat), lambda b, h, i: (b, h, 0)),
            pl.BlockSpec((1, kw, n_ctx), lambda b, h, i: (b, h, 0)),
        ],
        out_specs=pl.BlockSpec((1, ROW_TILE, qw), lambda b, h, i: (b, i, h)),
        out_shape=jax.ShapeDtypeStruct((batch, n_lat, n_heads * HEAD_W), BF16),
        scratch_shapes=[pltpu.VMEM((2 * (n_lat + n_ctx), ROW_TILE), F32), pltpu.VMEM((SUBLANES, ROW_TILE), F32)],
        compiler_params=_params(3),
        name=name,
    )(lam, subln, q, q, k, k_ctx, vt, vt_ctx)
    if not need_ctx:
        return o_lat, None
    o_ctx = pl.pallas_call(
        functools.partial(_attn_ctx_kernel, **static),
        grid=(batch, n_groups),
        in_specs=[
            _resident(lam.shape),
            _resident(subln.shape),
            pl.BlockSpec((1, n_ctx, qw), lambda b, h: (b, 0, h)),
            pl.BlockSpec((1, n_ctx, kw), lambda b, h: (b, 0, h)),
            pl.BlockSpec((1, kw, n_ctx), lambda b, h: (b, h, 0)),
        ],
        out_specs=pl.BlockSpec((1, n_ctx, qw), lambda b, h: (b, 0, h)),
        out_shape=jax.ShapeDtypeStruct((batch, n_ctx, n_heads * HEAD_W), BF16),
        scratch_shapes=[pltpu.VMEM((2 * n_ctx, n_ctx), F32)],
        compiler_params=_params(2),
        name=name + "_ctx",
    )(lam, subln, q_ctx, k_ctx, vt_ctx)
    return o_lat, o_ctx


DFT_COL_TILE = 512
DFT_EXTRA_ROWS = 16


def _dft_mirror_kernel(cl_ref, sl_ref, ch_ref, flip_ref, xc_ref, xs_ref, o_ref, carry_scr, *, n_steps):
    s = pl.program_id(2)
    t = n_steps - 1 - s

    @pl.when(s == 0)
    def _():
        carry_scr[...] = _dot(ch_ref[...], xc_ref[0])

    a = _dot(cl_ref[...], xc_ref[0])
    b = _dot(sl_ref[...], xs_ref[0])
    total = a + b
    row = lax.broadcasted_iota(jnp.int32, total.shape, 0)
    mirror = _dot(flip_ref[...], total.astype(BF16))
    mirror = jnp.where(row == 0, carry_scr[0:1, :], mirror)
    carry_scr[0:1, :] = total[0:1]
    o_ref[0, pl.ds(pl.multiple_of(t * ROW_TILE, ROW_TILE), ROW_TILE), :] = (a - b).astype(BF16)
    lo = pl.multiple_of((2 * n_steps - 1 - t) * ROW_TILE, ROW_TILE)
    o_ref[0, pl.ds(lo, ROW_TILE), :] = mirror.astype(BF16)


def _position_dft(xc, xs):
    batch, n, d = xc.shape
    half = n // 2
    n_steps = half // ROW_TILE
    cl, sl = _dft_tables(n, n ** -0.5, n_rows=half + DFT_EXTRA_ROWS, negate_sin=False)
    c_half = cl[half:]
    cl, sl = cl[:half], sl[:half]
    flip = np.zeros((ROW_TILE, ROW_TILE), np.float32)
    flip[np.arange(1, ROW_TILE), ROW_TILE - np.arange(1, ROW_TILE)] = 1.0
    tile_rows = lambda b, c, s: (n_steps - 1 - s, 0)
    return pl.pallas_call(
        functools.partial(_dft_mirror_kernel, n_steps=n_steps),
        grid=(batch, d // DFT_COL_TILE, n_steps),
        in_specs=[
            pl.BlockSpec((ROW_TILE, n), tile_rows),
            pl.BlockSpec((ROW_TILE, n), tile_rows),
            _resident((DFT_EXTRA_ROWS, n)),
            _resident((ROW_TILE, ROW_TILE)),
            pl.BlockSpec((1, n, DFT_COL_TILE), lambda b, c, s: (b, 0, c)),
            pl.BlockSpec((1, n, DFT_COL_TILE), lambda b, c, s: (b, 0, c)),
        ],
        out_specs=pl.BlockSpec((1, n, DFT_COL_TILE), lambda b, c, s: (b, 0, c)),
        out_shape=jax.ShapeDtypeStruct((batch, n, d), BF16),
        scratch_shapes=[pltpu.VMEM((DFT_EXTRA_ROWS, DFT_COL_TILE), F32)],
        compiler_params=_params(3),
        name="position_dft",
    )(cl, sl, c_half, jnp.asarray(flip, dtype=BF16), xc, xs)


def _dft_small_kernel(c_ref, s_ref, xc_ref, xs_ref, o_ref):
    o_ref[0] = (_dot(c_ref[...], xc_ref[0]) + _dot(s_ref[...], xs_ref[0])).astype(BF16)


def _position_dft_small(xc, xs):
    batch, n, d = xc.shape
    cc, sc = _dft_tables(n, n ** -0.5)
    whole = lambda b: (b, 0, 0)
    return pl.pallas_call(
        _dft_small_kernel,
        grid=(batch,),
        in_specs=[
            _resident((n, n)),
            _resident((n, n)),
            pl.BlockSpec((1, n, d), whole),
            pl.BlockSpec((1, n, d), whole),
        ],
        out_specs=pl.BlockSpec((1, n, d), whole),
        out_shape=jax.ShapeDtypeStruct((batch, n, d), BF16),
        compiler_params=_params(1),
        name="position_dft_ctx",
    )(cc, sc, xc, xs)


def _out_ffn_kernel(a_ref, x_ref, wo_ref, bo_ref, mod_ref, ln_ref, wgu_ref, wd_ref, fn_ref, o_ref, a_scr,
                    *, d_ff, final_norm):
    mod = mod_ref[0]
    x = x_ref[0] + mod[G1:G1 + 1] * (_dot(a_ref[0], wo_ref[...]) + bo_ref[...])
    h = _norm_mod(x, ln_ref[...], mod, SH2, SC2)
    for lo in range(0, d_ff, FF_CHUNK):
        hi = min(lo + FF_CHUNK, d_ff)
        g = _dot(h, wgu_ref[:, lo:hi])
        u = _dot(h, wgu_ref[:, d_ff + lo:d_ff + hi])
        a_scr[:, lo:hi] = (_silu(g) * u).astype(BF16)
    y = x + mod[G2:G2 + 1] * _dot(a_scr[...], wd_ref[...])
    if final_norm:
        y = _rms(y, fn_ref[...])
    o_ref[0] = y


def _out_ffn(a, x, mod, mod_row, w_o, b_o, ln, w_gu, w_down, final_g, *, row_tile, final_norm):
    batch, rows, d = x.shape
    k = a.shape[2]
    d_ff = w_down.shape[0]
    row = lambda b, j: (b, j, 0)
    return pl.pallas_call(
        functools.partial(_out_ffn_kernel, d_ff=d_ff, final_norm=final_norm),
        grid=(batch, rows // row_tile),
        in_specs=[
            pl.BlockSpec((1, row_tile, k), row),
            pl.BlockSpec((1, row_tile, d), row),
            _resident((k, d)),
            _resident((1, d)),
            _mod_spec(d, mod_row),
            _resident((1, d)),
            _resident((d, 2 * d_ff)),
            _resident((d_ff, d)),
            _resident((1, d)),
        ],
        out_specs=pl.BlockSpec((1, row_tile, d), row),
        out_shape=jax.ShapeDtypeStruct((batch, rows, d), F32),
        scratch_shapes=[pltpu.VMEM((row_tile, d_ff), BF16)],
        compiler_params=_params(2),
        name="out_ffn",
    )(a, x, w_o, b_o, mod, ln, w_gu, w_down, final_g)


def _half_split_perm(n_comp, comp_dim):
    src = np.zeros(HEAD_W, np.int32)
    for m in range(n_comp):
        for i in range(comp_dim // 2):
            for p in range(2):
                src[p * (HEAD_W // 2) + m * (comp_dim // 2) + i] = m * comp_dim + 2 * i + p
    return src


def _rope_lane_tables(seq, head_dim):
    axis_dim = head_dim // 2
    rows = jnp.arange(seq, dtype=jnp.int32) // GRID_W
    cols = jnp.arange(seq, dtype=jnp.int32) % GRID_W
    inv_freq = ROPE_THETA ** (-jnp.arange(0, axis_dim, 2, dtype=F32) / axis_dim)
    ang = jnp.concatenate([rows[:, None].astype(F32) * inv_freq,
                           cols[:, None].astype(F32) * inv_freq], axis=-1)
    reps = (HEAD_W // 2) // (head_dim // 2)
    cos = jnp.tile(jnp.cos(ang), (1, 2 * reps))
    sin = jnp.tile(jnp.sin(ang), (1, reps))
    return cos, jnp.concatenate([-sin, sin], axis=-1)


def _dft_tables(n, scale, n_rows=None, negate_sin=True):
    n_rows = n if n_rows is None else n_rows
    r = 1
    while r * r < n:
        r *= 2
    q = n // r
    k = jnp.arange(n_rows, dtype=jnp.int32)[:, None]
    a_idx = (k * jnp.arange(q, dtype=jnp.int32)[None, :] * r) % n
    b_idx = (k * jnp.arange(r, dtype=jnp.int32)[None, :]) % n
    w = 2.0 * math.pi / n
    ca, sa = jnp.cos(a_idx.astype(F32) * w), jnp.sin(a_idx.astype(F32) * w)
    cb, sb = jnp.cos(b_idx.astype(F32) * w) * scale, jnp.sin(b_idx.astype(F32) * w) * scale
    cos = ca[:, :, None] * cb[:, None, :] - sa[:, :, None] * sb[:, None, :]
    sin = sa[:, :, None] * cb[:, None, :] + ca[:, :, None] * sb[:, None, :]
    sin = -sin if negate_sin else sin
    return cos.reshape(n_rows, n).astype(BF16), sin.reshape(n_rows, n).astype(BF16)


def _channel_dft_weight(gd):
    idx = (np.arange(gd)[:, None] * np.arange(gd)[None, :]) % gd
    ang = 2.0 * np.pi * idx / gd
    w = np.concatenate([np.cos(ang), np.sin(ang)], axis=1) / math.sqrt(gd)
    return jnp.asarray(w, dtype=BF16)


def kernel(x, c, ctx, c_ctx, mod_w, mod_b, ln_mix, ln_ffn, ffn_w_gu, ffn_w_down, a_w_qkv, a_lam, a_subln, a_w_o, b_w_qkv, b_q_norm, b_k_norm, b_w_o, c_w_o, c_b_o, final_norm):
    batch, seq, d = x.shape
    n_ctx = ctx.shape[1]
    depth = mod_w.shape[0]
    assert seq % BIG_ROW_TILE == 0 and n_ctx == ROW_TILE and batch < MOD_ROWS

    cs = jnp.concatenate([c, c_ctx[None, :], jnp.zeros((MOD_ROWS - batch - 1, d), F32)], axis=0)
    mod_all = _modulation(cs, mod_w, mod_b).reshape(depth, MOD_ROWS, 6, d)
    lat = dict(mod_row=None, row_tile=BIG_ROW_TILE)
    con = dict(mod_row=batch, row_tile=n_ctx)

    perm_a = _half_split_perm(2, DIFF_HEAD_DIM)
    perm_b = _half_split_perm(1, GQA_HEAD_DIM)
    rope_a = _rope_lane_tables(seq, DIFF_HEAD_DIM)
    rope_b = _rope_lane_tables(seq, GQA_HEAD_DIM)
    rope_ctx = (jnp.ones((n_ctx, HEAD_W), F32), jnp.zeros((n_ctx, HEAD_W), F32))
    ones_head = jnp.ones((1, HEAD_W), F32)
    no_bias = jnp.zeros((1, d), F32)

    x_lat, x_ctx = x, ctx
    for i in range(depth):
        last = i == depth - 1
        kind, j = i % N_MIXERS, i // N_MIXERS
        mod = mod_all[i]
        ln1 = ln_mix[i][None, :]

        if kind == 2:
            w_chan = _channel_dft_weight(d // FNET_GROUPS)
            mixed = _position_dft(*_channel_dft(x_lat, mod, None, ln1, w_chan, BIG_ROW_TILE))
            if not last:
                mixed_ctx = _position_dft_small(*_channel_dft(x_ctx, mod, batch, ln1, w_chan, n_ctx))
            w_o, b_o = c_w_o[j], c_b_o[j][None, :]
        else:
            if kind == 0:
                n_q = n_k = DIFF_HEADS * HEAD_W
                perm, rope, w = perm_a, rope_a, a_w_qkv[j]
                proj = dict(qn=ones_head, kn=ones_head, n_q=n_q, n_k=n_k, head_norm=False,
                            q_scale=DIFF_HEAD_DIM ** -0.5 * LOG2E)
                attn = dict(lam=a_lam[j], subln=a_subln[j][None, :], diff=True, lambda_init=_lambda_init(i),
                            n_heads=DIFF_HEADS, heads_per_step=4, kv_heads_per_step=4)
                w_o = a_w_o[j]
            else:
                n_q, n_k = GQA_HEADS * HEAD_W, GQA_KV_HEADS * HEAD_W
                perm, rope, w = perm_b, rope_b, b_w_qkv[j]
                proj = dict(qn=b_q_norm[j][perm_b][None, :], kn=b_k_norm[j][perm_b][None, :], n_q=n_q, n_k=n_k,
                            head_norm=True, q_scale=GQA_HEAD_DIM ** -0.5 * LOG2E)
                attn = dict(lam=jnp.zeros((4, DIFF_HEAD_DIM), F32), subln=ones_head, diff=False, lambda_init=0.0,
                            n_heads=GQA_HEADS, heads_per_step=GQA_HEADS, kv_heads_per_step=GQA_KV_HEADS)
                w_o = b_w_o[j]
            cols = (np.arange((n_q + n_k) // HEAD_W)[:, None] * HEAD_W + perm[None, :]).reshape(-1)
            w_qk = w[:, cols].astype(BF16)
            w_vt = w[:, n_q + n_k:].T.astype(BF16)
            lat_proj = dict(lat, row_tile=ROW_TILE) if proj["head_norm"] else lat
            q, k, vt = _qkv_project(x_lat, mod, ln=ln1, w_qk=w_qk, w_vt=w_vt, cosf=rope[0], sinf=rope[1],
                                    **lat_proj, **proj)
            q_ctx, k_ctx, vt_ctx = _qkv_project(x_ctx, mod, ln=ln1, w_qk=w_qk, w_vt=w_vt, cosf=rope_ctx[0],
                                                sinf=rope_ctx[1], **con, **proj)
            mixed, mixed_ctx = _attention(q, k, vt, q_ctx, k_ctx, vt_ctx, need_ctx=not last, **attn)
            b_o = no_bias

        ffn = dict(w_o=w_o.astype(BF16), b_o=b_o, ln=ln_ffn[i][None, :], w_gu=ffn_w_gu[i].astype(BF16),
                   w_down=ffn_w_down[i].astype(BF16), final_g=final_norm[None, :], final_norm=last)
        if not last:
            x_ctx = _out_ffn(mixed_ctx, x_ctx, mod, **con, **ffn)
        x_lat = _out_ffn(mixed, x_lat, mod, **lat, **ffn)
    return x_lat
```

```python
import functools
import math

import numpy as np
import jax
import jax.numpy as jnp
from jax import lax
from jax.experimental import pallas as pl
from jax.experimental.pallas import tpu as pltpu

GRID_W = 64
ROPE_THETA = 10000.0
NORM_EPS = 1e-6
DIFF_HEADS = 8
DIFF_HEAD_DIM = 64
GQA_HEADS = 8
GQA_KV_HEADS = 2
GQA_HEAD_DIM = 128
FNET_GROUPS = 4
N_MIXERS = 3

LANES = 128
SUBLANES = 8
VMEM_LIMIT = 56 * 1024 * 1024

ROW_TILE = 256
BIG_ROW_TILE = 512
FFN_ROW_TILE = 1024
HEAD_W = 128
KEY_CHUNK = 256
SCORE_LEAD = 4
FF_CHUNK = 256
ONES_ROWS = 16
MOD_ROWS = 16
LOG2E = math.log2(math.e)

BF16 = jnp.bfloat16
F32 = jnp.float32

SH1, SC1, G1, SH2, SC2, G2 = range(6)


def _lambda_init(layer_idx):
    return 0.8 - 0.6 * float(np.exp(-0.3 * layer_idx))


def _params(n_grid):
    return pltpu.CompilerParams(
        dimension_semantics=("arbitrary",) * n_grid, vmem_limit_bytes=VMEM_LIMIT)


def _resident(shape):
    nd = len(shape)
    return pl.BlockSpec(shape, lambda *_: (0,) * nd, pipeline_mode=pl.Buffered(1))


def _dot(a, b):
    return jnp.dot(a, b, preferred_element_type=F32)


def _dot_nt(a, b):
    return lax.dot_general(a, b, (((1,), (1,)), ((), ())), preferred_element_type=F32)


def _rms(x, g):
    return x * lax.rsqrt(jnp.mean(x * x, axis=-1, keepdims=True) + NORM_EPS) * g


def _silu(x):
    return x / (1.0 + jnp.exp(-x))


def _mod_kernel(cs_ref, w_ref, b_ref, o_ref):
    s = _silu(cs_ref[...]).astype(BF16)
    o_ref[0] = _dot(s, w_ref[0].astype(BF16)) + b_ref[0]


def _modulation(cs, mod_w, mod_b, tn=1536):
    depth, d, n = mod_w.shape
    return pl.pallas_call(
        _mod_kernel,
        grid=(depth, n // tn),
        in_specs=[
            pl.BlockSpec((MOD_ROWS, d), lambda i, j: (0, 0)),
            pl.BlockSpec((1, d, tn), lambda i, j: (i, 0, j)),
            pl.BlockSpec((1, 1, tn), lambda i, j: (i, 0, j)),
        ],
        out_specs=pl.BlockSpec((1, MOD_ROWS, tn), lambda i, j: (i, 0, j)),
        out_shape=jax.ShapeDtypeStruct((depth, MOD_ROWS, n), F32),
        compiler_params=_params(2),
        name="modulation",
    )(cs, mod_w, mod_b.reshape(depth, 1, n))


def _mod_spec(d, mod_row):
    if mod_row is None:
        return pl.BlockSpec((1, 6, d), lambda b, j: (b, 0, 0))
    return pl.BlockSpec((1, 6, d), lambda b, j: (mod_row, 0, 0))


def _norm_mod(x, ln, mod, shift, scale):
    y = _rms(x, ln)
    return (y * (1.0 + mod[scale:scale + 1]) + mod[shift:shift + 1]).astype(BF16)


def _qkv_kernel(x_ref, mod_ref, ln_ref, w_ref, wvt_ref, cos_ref, sin_ref, qn_ref, kn_ref,
                q_ref, k_ref, vt_ref, *, n_q, n_k, head_norm, q_scale):
    h = _norm_mod(x_ref[0], ln_ref[...], mod_ref[0], SH1, SC1)
    cos, sin = cos_ref[...], sin_ref[...]
    for head in range((n_q + n_k) // HEAD_W):
        is_q = head < n_q // HEAD_W
        if head % 2 == 0:
            pair = _dot(h, w_ref[:, head * HEAD_W:(head + 2) * HEAD_W])
        t = pair[:, (head % 2) * HEAD_W:(head % 2 + 1) * HEAD_W]
        if head_norm:
            t = _rms(t, qn_ref[...] if is_q else kn_ref[...])
        t = t * cos + pltpu.roll(t, HEAD_W // 2, 1) * sin
        if is_q:
            q_ref[0, :, head * HEAD_W:(head + 1) * HEAD_W] = (t * q_scale).astype(BF16)
        else:
            c0 = head * HEAD_W - n_q
            k_ref[0, :, c0:c0 + HEAD_W] = t.astype(BF16)
    vt_ref[0] = _dot_nt(wvt_ref[...], h).astype(BF16)


def _qkv_project(x, mod, mod_row, ln, w_qk, w_vt, cosf, sinf, qn, kn, *, row_tile, n_q, n_k, head_norm,
                 q_scale):
    batch, rows, d = x.shape
    n_v = w_vt.shape[0]
    row = lambda b, j: (b, j, 0)
    kern = functools.partial(_qkv_kernel, n_q=n_q, n_k=n_k, head_norm=head_norm, q_scale=q_scale)
    return pl.pallas_call(
        kern,
        grid=(batch, rows // row_tile),
        in_specs=[
            pl.BlockSpec((1, row_tile, d), row),
            _mod_spec(d, mod_row),
            _resident((1, d)),
            _resident((d, n_q + n_k)),
            _resident((n_v, d)),
            pl.BlockSpec((row_tile, HEAD_W), lambda b, j: (j, 0)),
            pl.BlockSpec((row_tile, HEAD_W), lambda b, j: (j, 0)),
            _resident((1, HEAD_W)),
            _resident((1, HEAD_W)),
        ],
        out_specs=[
            pl.BlockSpec((1, row_tile, n_q), row),
            pl.BlockSpec((1, row_tile, n_k), row),
            pl.BlockSpec((1, n_v, row_tile), lambda b, j: (b, 0, j)),
        ],
        out_shape=[
            jax.ShapeDtypeStruct((batch, rows, n_q), BF16),
            jax.ShapeDtypeStruct((batch, rows, n_k), BF16),
            jax.ShapeDtypeStruct((batch, n_v, rows), BF16),
        ],
        compiler_params=_params(2),
        name="qkv_project",
    )(x, mod, ln, w_qk, w_vt, cosf, sinf, qn, kn)


def _chan_dft_kernel(x_ref, mod_ref, ln_ref, w_ref, oc_ref, os_ref):
    h = _norm_mod(x_ref[0], ln_ref[...], mod_ref[0], SH1, SC1)
    d = h.shape[1]
    gd = w_ref.shape[0]
    for g in range(d // gd):
        y = _dot(h[:, g * gd:(g + 1) * gd], w_ref[...])
        oc_ref[0, :, g * gd:(g + 1) * gd] = y[:, :gd].astype(BF16)
        os_ref[0, :, g * gd:(g + 1) * gd] = y[:, gd:].astype(BF16)


def _channel_dft(x, mod, mod_row, ln, w, row_tile):
    batch, rows, d = x.shape
    row = lambda b, j: (b, j, 0)
    return pl.pallas_call(
        _chan_dft_kernel,
        grid=(batch, rows // row_tile),
        in_specs=[
            pl.BlockSpec((1, row_tile, d), row),
            _mod_spec(d, mod_row),
            _resident((1, d)),
            _resident(w.shape),
        ],
        out_specs=[pl.BlockSpec((1, row_tile, d), row)] * 2,
        out_shape=[jax.ShapeDtypeStruct((batch, rows, d), BF16)] * 2,
        compiler_params=_params(2),
        name="channel_dft",
    )(x, mod, ln, w)


def _kv_chunks(klat_ref, kctx_ref, vtlat_ref, vtctx_ref, n_lat_chunks):
    def pick(lat_ref, ctx_ref, c):
        return (lat_ref, c) if c < n_lat_chunks else (ctx_ref, c - n_lat_chunks)

    def key_chunk(c, kvh):
        ref, cc = pick(klat_ref, kctx_ref, c)
        return ref[0, cc * KEY_CHUNK:(cc + 1) * KEY_CHUNK, kvh * HEAD_W:(kvh + 1) * HEAD_W]

    def val_chunk(c, kvh):
        ref, cc = pick(vtlat_ref, vtctx_ref, c)
        return ref[0, kvh * HEAD_W:(kvh + 1) * HEAD_W, cc * KEY_CHUNK:(cc + 1) * KEY_CHUNK]

    return key_chunk, val_chunk


def _score_chunk(key_chunk, s_scr, q, kvh, buf, c, n_chunks, m8):
    tq = q.shape[0]
    s = _dot_nt(key_chunk(c, kvh), q)
    lo = (buf * n_chunks + c) * KEY_CHUNK
    s_scr[lo:lo + KEY_CHUNK, :] = s
    return jnp.maximum(m8, jnp.max(s.reshape(KEY_CHUNK // SUBLANES, SUBLANES, tq), axis=0))


def _softmax_pv_streams(score_list, weigh_list, offset, key_chunk, val_chunk, s_scr, m8_carried, n_chunks,
                        on_done):
    tq = score_list[0][0].shape[0]
    ones_rows = jnp.ones((ONES_ROWS, KEY_CHUNK), BF16)

    def weigh(kvh, buf, c, mx, acc):
        lo = (buf * n_chunks + c) * KEY_CHUNK
        e = jnp.exp2(s_scr[lo:lo + KEY_CHUNK, :] - mx)
        v_ones = jnp.concatenate([val_chunk(c, kvh), ones_rows], axis=0)
        return acc + _dot(v_ones, e.astype(BF16))

    m8 = [None] * len(score_list)
    mx = acc = None
    n_steps = max(len(score_list) * n_chunks, offset + len(weigh_list) * n_chunks)
    for g in range(n_steps):
        if g < len(score_list) * n_chunks:
            j, c = divmod(g, n_chunks)
            q, kvh, buf = score_list[j]
            if c == 0:
                m8[j] = jnp.full((SUBLANES, tq), -jnp.inf, F32)
            m8[j] = _score_chunk(key_chunk, s_scr, q, kvh, buf, c, n_chunks, m8[j])
        ge = g - offset
        if 0 <= ge < len(weigh_list) * n_chunks:
            p, c = divmod(ge, n_chunks)
            kvh, buf, max_src = weigh_list[p]
            if c == 0:
                src = m8_carried if max_src is None else m8[max_src]
                mx = jnp.max(src, axis=0, keepdims=True)
                acc = jnp.zeros((HEAD_W + ONES_ROWS, tq), F32)
            acc = weigh(kvh, buf, c, mx, acc)
            if c == n_chunks - 1:
                on_done(p, acc[:HEAD_W] * (1.0 / acc[HEAD_W:HEAD_W + 1]))
    return m8


def _head_problems(q_ref, diff, heads_per_step, kv_heads_per_step, n_heads=None):
    problems = []
    for hh in range(heads_per_step if n_heads is None else n_heads):
        q = q_ref[0, :, hh * HEAD_W:(hh + 1) * HEAD_W]
        kvh = hh * kv_heads_per_step // heads_per_step
        if diff:
            lane = lax.broadcasted_iota(jnp.int32, q.shape, 1)
            first = (lane % (HEAD_W // 2)) < (HEAD_W // 4)
            zero = jnp.zeros_like(q)
            problems += [(jnp.where(first, q, zero), kvh), (jnp.where(first, zero, q), kvh)]
        else:
            problems.append((q, kvh))
    return problems


def _head_writer(o_ref, lam_ref, subln_ref, diff, lambda_init):
    first_comp = {}

    def on_done(p, o_t):
        if not diff:
            o_ref[0, :, p * HEAD_W:(p + 1) * HEAD_W] = o_t.T.astype(BF16)
            return
        hh = p // 2
        if p % 2 == 0:
            first_comp[hh] = o_t
            return
        lv = lam_ref[...]
        lam = (jnp.exp(jnp.sum(lv[0:1] * lv[1:2], axis=-1, keepdims=True))
               - jnp.exp(jnp.sum(lv[2:3] * lv[3:4], axis=-1, keepdims=True)) + lambda_init)
        o = (first_comp.pop(hh) - lam * o_t).T
        o = _rms(o, subln_ref[...]) * (1.0 - lambda_init)
        o_ref[0, :, hh * HEAD_W:(hh + 1) * HEAD_W] = o.astype(BF16)

    return on_done


def _attn_latent_kernel(lam_ref, subln_ref, q_ref, qn_ref, klat_ref, kctx_ref, vtlat_ref, vtctx_ref, o_ref,
                        s_scr, m8_scr, *, diff, lambda_init, heads_per_step, kv_heads_per_step):
    n_lat_chunks = klat_ref.shape[1] // KEY_CHUNK
    n_chunks = n_lat_chunks + kctx_ref.shape[1] // KEY_CHUNK
    key_chunk, val_chunk = _kv_chunks(klat_ref, kctx_ref, vtlat_ref, vtctx_ref, n_lat_chunks)
    cur = _head_problems(q_ref, diff, heads_per_step, kv_heads_per_step)
    nxt_q, nxt_kvh = _head_problems(qn_ref, diff, heads_per_step, kv_heads_per_step, n_heads=1)[0]
    n_prob = len(cur)
    assert n_prob % 2 == 0

    @pl.when(pl.program_id(2) == 0)
    def _():
        q, kvh = cur[0]
        m8 = jnp.full((SUBLANES, q.shape[0]), -jnp.inf, F32)
        for c in range(n_chunks):
            m8 = _score_chunk(key_chunk, s_scr, q, kvh, 0, c, n_chunks, m8)
        m8_scr[...] = m8

    score_list = [(q, kvh, p % 2) for p, (q, kvh) in enumerate(cur) if p > 0] + [(nxt_q, nxt_kvh, 0)]
    weigh_list = [(kvh, p % 2, None if p == 0 else p - 1) for p, (_, kvh) in enumerate(cur)]
    m8 = _softmax_pv_streams(
        score_list, weigh_list, min(SCORE_LEAD, n_chunks - 1), key_chunk, val_chunk, s_scr, m8_scr[...],
        n_chunks, _head_writer(o_ref, lam_ref, subln_ref, diff, lambda_init))
    m8_scr[...] = m8[-1]


def _attn_ctx_kernel(lam_ref, subln_ref, q_ref, k_ref, vt_ref, o_ref, s_scr,
                     *, diff, lambda_init, heads_per_step, kv_heads_per_step):
    n_chunks = k_ref.shape[1] // KEY_CHUNK
    key_chunk, val_chunk = _kv_chunks(None, k_ref, None, vt_ref, 0)
    cur = _head_problems(q_ref, diff, heads_per_step, kv_heads_per_step)
    score_list = [(q, kvh, p % 2) for p, (q, kvh) in enumerate(cur)]
    weigh_list = [(kvh, p % 2, p) for p, (_, kvh) in enumerate(cur)]
    _softmax_pv_streams(
        score_list, weigh_list, n_chunks + min(SCORE_LEAD, n_chunks - 1), key_chunk, val_chunk, s_scr, None,
        n_chunks, _head_writer(o_ref, lam_ref, subln_ref, diff, lambda_init))


def _attention(q, k, vt, q_ctx, k_ctx, vt_ctx, lam, subln, *, diff, lambda_init, n_heads, heads_per_step,
               kv_heads_per_step, need_ctx):
    batch, n_lat, _ = q.shape
    n_ctx = k_ctx.shape[1]
    qw = heads_per_step * HEAD_W
    kw = kv_heads_per_step * HEAD_W
    n_groups = n_heads // heads_per_step
    n_lat_tiles = n_lat // ROW_TILE
    static = dict(diff=diff, lambda_init=lambda_init, heads_per_step=heads_per_step,
                  kv_heads_per_step=kv_heads_per_step)
    name = "diff_attention" if diff else "gqa_attention"
    o_lat = pl.pallas_call(
        functools.partial(_attn_latent_kernel, **static),
        grid=(batch, n_groups, n_lat_tiles),
        in_specs=[
            _resident(lam.shape),
            _resident(subln.shape),
            pl.BlockSpec((1, ROW_TILE, qw), lambda b, h, i: (b, i, h)),
            pl.BlockSpec((1, ROW_TILE, qw), lambda b, h, i: (b, jnp.minimum(i + 1, n_lat_tiles - 1), h)),
            pl.BlockSpec((1, n_lat, kw), lambda b, h, i: (b, 0, h)),
            pl.BlockSpec((1, n_ctx, kw), lambda b, h, i: (b, 0, h)),
            pl.BlockSpec((1, kw, n_lat), lambda b, h, i: (b, h, 0)),
            pl.BlockSpec((1, kw, n_ctx), lambda b, h, i: (b, h, 0)),
        ],
        out_specs=pl.BlockSpec((1, ROW_TILE, qw), lambda b, h, i: (b, i, h)),
        out_shape=jax.ShapeDtypeStruct((batch, n_lat, n_heads * HEAD_W), BF16),
        scratch_shapes=[pltpu.VMEM((2 * (n_lat + n_ctx), ROW_TILE), F32), pltpu.VMEM((SUBLANES, ROW_TILE), F32)],
        compiler_params=_params(3),
        name=name,
    )(lam, subln, q, q, k, k_ctx, vt, vt_ctx)
    if not need_ctx:
        return o_lat, None
    o_ctx = pl.pallas_call(
        functools.partial(_attn_ctx_kernel, **static),
        grid=(batch, n_groups),
        in_specs=[
            _resident(lam.shape),
            _resident(subln.shape),
            pl.BlockSpec((1, n_ctx, qw), lambda b, h: (b, 0, h)),
            pl.BlockSpec((1, n_ctx, kw), lambda b, h: (b, 0, h)),
            pl.BlockSpec((1, kw, n_ctx), lambda b, h: (b, h, 0)),
        ],
        out_specs=pl.BlockSpec((1, n_ctx, qw), lambda b, h: (b, 0, h)),
        out_shape=jax.ShapeDtypeStruct((batch, n_ctx, n_heads * HEAD_W), BF16),
        scratch_shapes=[pltpu.VMEM((2 * n_ctx, n_ctx), F32)],
        compiler_params=_params(2),
        name=name + "_ctx",
    )(lam, subln, q_ctx, k_ctx, vt_ctx)
    return o_lat, o_ctx


DFT_COL_TILE = 1024
DFT_EXTRA_ROWS = 16


def _dft_mirror_kernel(cl_ref, sl_ref, ch_ref, flip_ref, xc_ref, xs_ref, o_ref, carry_scr, *, n_steps):
    s = pl.program_id(2)
    t = n_steps - 1 - s

    @pl.when(s == 0)
    def _():
        carry_scr[...] = _dot(ch_ref[...], xc_ref[0])

    a = _dot(cl_ref[...], xc_ref[0])
    b = _dot(sl_ref[...], xs_ref[0])
    total = a + b
    row = lax.broadcasted_iota(jnp.int32, total.shape, 0)
    mirror = _dot(flip_ref[...], total.astype(BF16))
    mirror = jnp.where(row == 0, carry_scr[0:1, :], mirror)
    carry_scr[0:1, :] = total[0:1]
    o_ref[0, pl.ds(pl.multiple_of(t * ROW_TILE, ROW_TILE), ROW_TILE), :] = (a - b).astype(BF16)
    lo = pl.multiple_of((2 * n_steps - 1 - t) * ROW_TILE, ROW_TILE)
    o_ref[0, pl.ds(lo, ROW_TILE), :] = mirror.astype(BF16)


def _position_dft(xc, xs):
    batch, n, d = xc.shape
    half = n // 2
    n_steps = half // ROW_TILE
    cl, sl = _dft_tables(n, n ** -0.5, n_rows=half, negate_sin=False)
    alt = jnp.where(jnp.arange(n, dtype=jnp.int32) % 2 == 0, n ** -0.5, -(n ** -0.5)).astype(BF16)
    c_half = jnp.zeros((DFT_EXTRA_ROWS, n), BF16).at[0].set(alt)
    flip = np.zeros((ROW_TILE, ROW_TILE), np.float32)
    flip[np.arange(1, ROW_TILE), ROW_TILE - np.arange(1, ROW_TILE)] = 1.0
    tile_rows = lambda b, c, s: (n_steps - 1 - s, 0)
    return pl.pallas_call(
        functools.partial(_dft_mirror_kernel, n_steps=n_steps),
        grid=(batch, d // DFT_COL_TILE, n_steps),
        in_specs=[
            pl.BlockSpec((ROW_TILE, n), tile_rows),
            pl.BlockSpec((ROW_TILE, n), tile_rows),
            _resident((DFT_EXTRA_ROWS, n)),
            _resident((ROW_TILE, ROW_TILE)),
            pl.BlockSpec((1, n, DFT_COL_TILE), lambda b, c, s: (b, 0, c), pipeline_mode=pl.Buffered(1)),
            pl.BlockSpec((1, n, DFT_COL_TILE), lambda b, c, s: (b, 0, c), pipeline_mode=pl.Buffered(1)),
        ],
        out_specs=pl.BlockSpec((1, n, DFT_COL_TILE), lambda b, c, s: (b, 0, c)),
        out_shape=jax.ShapeDtypeStruct((batch, n, d), BF16),
        scratch_shapes=[pltpu.VMEM((DFT_EXTRA_ROWS, DFT_COL_TILE), F32)],
        compiler_params=_params(3),
        name="position_dft",
    )(cl, sl, c_half, jnp.asarray(flip, dtype=BF16), xc, xs)


def _dft_small_kernel(c_ref, s_ref, xc_ref, xs_ref, o_ref):
    o_ref[0] = (_dot(c_ref[...], xc_ref[0]) + _dot(s_ref[...], xs_ref[0])).astype(BF16)


def _position_dft_small(xc, xs):
    batch, n, d = xc.shape
    cc, sc = _dft_tables(n, n ** -0.5)
    whole = lambda b: (b, 0, 0)
    return pl.pallas_call(
        _dft_small_kernel,
        grid=(batch,),
        in_specs=[
            _resident((n, n)),
            _resident((n, n)),
            pl.BlockSpec((1, n, d), whole),
            pl.BlockSpec((1, n, d), whole),
        ],
        out_specs=pl.BlockSpec((1, n, d), whole),
        out_shape=jax.ShapeDtypeStruct((batch, n, d), BF16),
        compiler_params=_params(1),
        name="position_dft_ctx",
    )(cc, sc, xc, xs)


def _out_ffn_kernel(a_ref, x_ref, wo_ref, bo_ref, mod_ref, ln_ref, wgu_ref, wd_ref, fn_ref, o_ref, a_scr,
                    *, d_ff, final_norm):
    mod = mod_ref[0]
    x = x_ref[0] + mod[G1:G1 + 1] * (_dot(a_ref[0], wo_ref[...]) + bo_ref[...])
    h = _norm_mod(x, ln_ref[...], mod, SH2, SC2)
    for lo in range(0, d_ff, FF_CHUNK):
        hi = min(lo + FF_CHUNK, d_ff)
        g = _dot(h, wgu_ref[:, lo:hi])
        u = _dot(h, wgu_ref[:, d_ff + lo:d_ff + hi])
        a_scr[:, lo:hi] = (_silu(g) * u).astype(BF16)
    y = x + mod[G2:G2 + 1] * _dot(a_scr[...], wd_ref[...])
    if final_norm:
        y = _rms(y, fn_ref[...])
    o_ref[0] = y


def _out_ffn(a, x, mod, mod_row, w_o, b_o, ln, w_gu, w_down, final_g, *, row_tile, final_norm):
    batch, rows, d = x.shape
    k = a.shape[2]
    d_ff = w_down.shape[0]
    row = lambda b, j: (b, j, 0)
    return pl.pallas_call(
        functools.partial(_out_ffn_kernel, d_ff=d_ff, final_norm=final_norm),
        grid=(batch, rows // row_tile),
        in_specs=[
            pl.BlockSpec((1, row_tile, k), row),
            pl.BlockSpec((1, row_tile, d), row),
            _resident((k, d)),
            _resident((1, d)),
            _mod_spec(d, mod_row),
            _resident((1, d)),
            _resident((d, 2 * d_ff)),
            _resident((d_ff, d)),
            _resident((1, d)),
        ],
        out_specs=pl.BlockSpec((1, row_tile, d), row),
        out_shape=jax.ShapeDtypeStruct((batch, rows, d), F32),
        scratch_shapes=[pltpu.VMEM((row_tile, d_ff), BF16)],
        compiler_params=_params(2),
        name="out_ffn",
    )(a, x, w_o, b_o, mod, ln, w_gu, w_down, final_g)


def _half_split_perm(n_comp, comp_dim):
    src = np.zeros(HEAD_W, np.int32)
    for m in range(n_comp):
        for i in range(comp_dim // 2):
            for p in range(2):
                src[p * (HEAD_W // 2) + m * (comp_dim // 2) + i] = m * comp_dim + 2 * i + p
    return src


def _rope_lane_tables(seq, head_dim):
    axis_dim = head_dim // 2
    rows = jnp.arange(seq, dtype=jnp.int32) // GRID_W
    cols = jnp.arange(seq, dtype=jnp.int32) % GRID_W
    inv_freq = ROPE_THETA ** (-jnp.arange(0, axis_dim, 2, dtype=F32) / axis_dim)
    ang = jnp.concatenate([rows[:, None].astype(F32) * inv_freq,
                           cols[:, None].astype(F32) * inv_freq], axis=-1)
    reps = (HEAD_W // 2) // (head_dim // 2)
    cos = jnp.tile(jnp.cos(ang), (1, 2 * reps))
    sin = jnp.tile(jnp.sin(ang), (1, reps))
    return cos, jnp.concatenate([-sin, sin], axis=-1)


def _dft_tables(n, scale, n_rows=None, negate_sin=True):
    n_rows = n if n_rows is None else n_rows
    r = 1
    while r * r < n:
        r *= 2
    q = n // r
    k = jnp.arange(n_rows, dtype=jnp.int32)[:, None]
    a_idx = (k * jnp.arange(q, dtype=jnp.int32)[None, :] * r) % n
    b_idx = (k * jnp.arange(r, dtype=jnp.int32)[None, :]) % n
    w = 2.0 * math.pi / n
    ca, sa = jnp.cos(a_idx.astype(F32) * w), jnp.sin(a_idx.astype(F32) * w)
    cb, sb = jnp.cos(b_idx.astype(F32) * w) * scale, jnp.sin(b_idx.astype(F32) * w) * scale
    cos = ca[:, :, None] * cb[:, None, :] - sa[:, :, None] * sb[:, None, :]
    sin = sa[:, :, None] * cb[:, None, :] + ca[:, :, None] * sb[:, None, :]
    sin = -sin if negate_sin else sin
    return cos.reshape(n_rows, n).astype(BF16), sin.reshape(n_rows, n).astype(BF16)


def _channel_dft_weight(gd):
    idx = (np.arange(gd)[:, None] * np.arange(gd)[None, :]) % gd
    ang = 2.0 * np.pi * idx / gd
    w = np.concatenate([np.cos(ang), np.sin(ang)], axis=1) / math.sqrt(gd)
    return jnp.asarray(w, dtype=BF16)


def kernel(x, c, ctx, c_ctx, mod_w, mod_b, ln_mix, ln_ffn, ffn_w_gu, ffn_w_down, a_w_qkv, a_lam, a_subln, a_w_o, b_w_qkv, b_q_norm, b_k_norm, b_w_o, c_w_o, c_b_o, final_norm):
    batch, seq, d = x.shape
    n_ctx = ctx.shape[1]
    depth = mod_w.shape[0]
    assert seq % FFN_ROW_TILE == 0 and n_ctx == ROW_TILE and batch < MOD_ROWS

    cs = jnp.concatenate([c, c_ctx[None, :], jnp.zeros((MOD_ROWS - batch - 1, d), F32)], axis=0)
    mod_all = _modulation(cs, mod_w, mod_b).reshape(depth, MOD_ROWS, 6, d)
    lat = dict(mod_row=None, row_tile=BIG_ROW_TILE)
    con = dict(mod_row=batch, row_tile=n_ctx)

    perm_a = _half_split_perm(2, DIFF_HEAD_DIM)
    perm_b = _half_split_perm(1, GQA_HEAD_DIM)
    rope_a = _rope_lane_tables(seq, DIFF_HEAD_DIM)
    rope_b = _rope_lane_tables(seq, GQA_HEAD_DIM)
    rope_ctx = (jnp.ones((n_ctx, HEAD_W), F32), jnp.zeros((n_ctx, HEAD_W), F32))
    ones_head = jnp.ones((1, HEAD_W), F32)
    no_bias = jnp.zeros((1, d), F32)

    x_lat, x_ctx = x, ctx
    for i in range(depth):
        last = i == depth - 1
        kind, j = i % N_MIXERS, i // N_MIXERS
        mod = mod_all[i]
        ln1 = ln_mix[i][None, :]

        if kind == 2:
            w_chan = _channel_dft_weight(d // FNET_GROUPS)
            mixed = _position_dft(*_channel_dft(x_lat, mod, None, ln1, w_chan, BIG_ROW_TILE))
            if not last:
                mixed_ctx = _position_dft_small(*_channel_dft(x_ctx, mod, batch, ln1, w_chan, n_ctx))
            w_o, b_o = c_w_o[j], c_b_o[j][None, :]
        else:
            if kind == 0:
                n_q = n_k = DIFF_HEADS * HEAD_W
                perm, rope, w = perm_a, rope_a, a_w_qkv[j]
                proj = dict(qn=ones_head, kn=ones_head, n_q=n_q, n_k=n_k, head_norm=False,
                            q_scale=DIFF_HEAD_DIM ** -0.5 * LOG2E)
                attn = dict(lam=a_lam[j], subln=a_subln[j][None, :], diff=True, lambda_init=_lambda_init(i),
                            n_heads=DIFF_HEADS, heads_per_step=4, kv_heads_per_step=4)
                w_o = a_w_o[j]
            else:
                n_q, n_k = GQA_HEADS * HEAD_W, GQA_KV_HEADS * HEAD_W
                perm, rope, w = perm_b, rope_b, b_w_qkv[j]
                proj = dict(qn=b_q_norm[j][perm_b][None, :], kn=b_k_norm[j][perm_b][None, :], n_q=n_q, n_k=n_k,
                            head_norm=True, q_scale=GQA_HEAD_DIM ** -0.5 * LOG2E)
                attn = dict(lam=jnp.zeros((4, DIFF_HEAD_DIM), F32), subln=ones_head, diff=False, lambda_init=0.0,
                            n_heads=GQA_HEADS, heads_per_step=GQA_HEADS, kv_heads_per_step=GQA_KV_HEADS)
                w_o = b_w_o[j]
            cols = (np.arange((n_q + n_k) // HEAD_W)[:, None] * HEAD_W + perm[None, :]).reshape(-1)
            w_qk = w[:, cols].astype(BF16)
            w_vt = w[:, n_q + n_k:].T.astype(BF16)
            lat_proj = dict(lat, row_tile=ROW_TILE) if proj["head_norm"] else lat
            q, k, vt = _qkv_project(x_lat, mod, ln=ln1, w_qk=w_qk, w_vt=w_vt, cosf=rope[0], sinf=rope[1],
                                    **lat_proj, **proj)
            q_ctx, k_ctx, vt_ctx = _qkv_project(x_ctx, mod, ln=ln1, w_qk=w_qk, w_vt=w_vt, cosf=rope_ctx[0],
                                                sinf=rope_ctx[1], **con, **proj)
            mixed, mixed_ctx = _attention(q, k, vt, q_ctx, k_ctx, vt_ctx, need_ctx=not last, **attn)
            b_o = no_bias

        ffn = dict(w_o=w_o.astype(BF16), b_o=b_o, ln=ln_ffn[i][None, :], w_gu=ffn_w_gu[i].astype(BF16),
                   w_down=ffn_w_down[i].astype(BF16), final_g=final_norm[None, :], final_norm=last)
        if not last:
            x_ctx = _out_ffn(mixed_ctx, x_ctx, mod, **con, **ffn)
        x_lat = _out_ffn(mixed, x_lat, mod, **dict(lat, row_tile=FFN_ROW_TILE), **ffn)
    return x_lat
```

```python
import functools
import math

import numpy as np
import jax
import jax.numpy as jnp
from jax import lax
from jax.experimental import pallas as pl
from jax.experimental.pallas import tpu as pltpu

GRID_W = 64
ROPE_THETA = 10000.0
NORM_EPS = 1e-6
DIFF_HEADS = 8
DIFF_HEAD_DIM = 64
GQA_HEADS = 8
GQA_KV_HEADS = 2
GQA_HEAD_DIM = 128
FNET_GROUPS = 4
N_MIXERS = 3

LANES = 128
SUBLANES = 8
VMEM_LIMIT = 56 * 1024 * 1024

ROW_TILE = 256
BIG_ROW_TILE = 512
FFN_ROW_TILE = 1024
HEAD_W = 128
KEY_CHUNK = 256
SCORE_LEAD = 4
FF_CHUNK = 256
ONES_ROWS = 16
MOD_ROWS = 16
LOG2E = math.log2(math.e)

BF16 = jnp.bfloat16
F32 = jnp.float32

SH1, SC1, G1, SH2, SC2, G2 = range(6)


def _lambda_init(layer_idx):
    return 0.8 - 0.6 * float(np.exp(-0.3 * layer_idx))


def _params(n_grid):
    return pltpu.CompilerParams(
        dimension_semantics=("arbitrary",) * n_grid, vmem_limit_bytes=VMEM_LIMIT)


def _resident(shape):
    nd = len(shape)
    return pl.BlockSpec(shape, lambda *_: (0,) * nd, pipeline_mode=pl.Buffered(1))


def _dot(a, b):
    return jnp.dot(a, b, preferred_element_type=F32)


def _dot_nt(a, b):
    return lax.dot_general(a, b, (((1,), (1,)), ((), ())), preferred_element_type=F32)


def _rms(x, g):
    return x * lax.rsqrt(jnp.mean(x * x, axis=-1, keepdims=True) + NORM_EPS) * g


def _silu(x):
    return x / (1.0 + jnp.exp(-x))


def _mod_kernel(cs_ref, w_ref, b_ref, o_ref):
    s = _silu(cs_ref[...]).astype(BF16)
    o_ref[0] = _dot(s, w_ref[0].astype(BF16)) + b_ref[0]


def _modulation(cs, mod_w, mod_b, tn=1536):
    depth, d, n = mod_w.shape
    return pl.pallas_call(
        _mod_kernel,
        grid=(depth, n // tn),
        in_specs=[
            pl.BlockSpec((MOD_ROWS, d), lambda i, j: (0, 0)),
            pl.BlockSpec((1, d, tn), lambda i, j: (i, 0, j)),
            pl.BlockSpec((1, 1, tn), lambda i, j: (i, 0, j)),
        ],
        out_specs=pl.BlockSpec((1, MOD_ROWS, tn), lambda i, j: (i, 0, j)),
        out_shape=jax.ShapeDtypeStruct((depth, MOD_ROWS, n), F32),
        compiler_params=_params(2),
        name="modulation",
    )(cs, mod_w, mod_b.reshape(depth, 1, n))


def _mod_spec(d, mod_row):
    if mod_row is None:
        return pl.BlockSpec((1, 6, d), lambda b, j: (b, 0, 0))
    return pl.BlockSpec((1, 6, d), lambda b, j: (mod_row, 0, 0))


def _norm_mod(x, ln, mod, shift, scale):
    y = _rms(x, ln)
    return (y * (1.0 + mod[scale:scale + 1]) + mod[shift:shift + 1]).astype(BF16)


def _qkv_kernel(x_ref, mod_ref, ln_ref, w_ref, wvt_ref, cos_ref, sin_ref, qn_ref, kn_ref,
                q_ref, k_ref, vt_ref, *, n_q, n_k, head_norm, q_scale):
    h = _norm_mod(x_ref[0], ln_ref[...], mod_ref[0], SH1, SC1)
    cos, sin = cos_ref[...], sin_ref[...]
    for head in range((n_q + n_k) // HEAD_W):
        is_q = head < n_q // HEAD_W
        if head % 2 == 0:
            pair = _dot(h, w_ref[:, head * HEAD_W:(head + 2) * HEAD_W])
        t = pair[:, (head % 2) * HEAD_W:(head % 2 + 1) * HEAD_W]
        if head_norm:
            t = _rms(t, qn_ref[...] if is_q else kn_ref[...])
        t = t * cos + pltpu.roll(t, HEAD_W // 2, 1) * sin
        if is_q:
            q_ref[0, :, head * HEAD_W:(head + 1) * HEAD_W] = (t * q_scale).astype(BF16)
        else:
            c0 = head * HEAD_W - n_q
            k_ref[0, :, c0:c0 + HEAD_W] = t.astype(BF16)
    vt_ref[0] = _dot_nt(wvt_ref[...], h).astype(BF16)


def _qkv_project(x, mod, mod_row, ln, w_qk, w_vt, cosf, sinf, qn, kn, *, row_tile, n_q, n_k, head_norm,
                 q_scale):
    batch, rows, d = x.shape
    n_v = w_vt.shape[0]
    row = lambda b, j: (b, j, 0)
    kern = functools.partial(_qkv_kernel, n_q=n_q, n_k=n_k, head_norm=head_norm, q_scale=q_scale)
    return pl.pallas_call(
        kern,
        grid=(batch, rows // row_tile),
        in_specs=[
            pl.BlockSpec((1, row_tile, d), row),
            _mod_spec(d, mod_row),
            _resident((1, d)),
            _resident((d, n_q + n_k)),
            _resident((n_v, d)),
            pl.BlockSpec((row_tile, HEAD_W), lambda b, j: (j, 0)),
            pl.BlockSpec((row_tile, HEAD_W), lambda b, j: (j, 0)),
            _resident((1, HEAD_W)),
            _resident((1, HEAD_W)),
        ],
        out_specs=[
            pl.BlockSpec((1, row_tile, n_q), row),
            pl.BlockSpec((1, row_tile, n_k), row),
            pl.BlockSpec((1, n_v, row_tile), lambda b, j: (b, 0, j)),
        ],
        out_shape=[
            jax.ShapeDtypeStruct((batch, rows, n_q), BF16),
            jax.ShapeDtypeStruct((batch, rows, n_k), BF16),
            jax.ShapeDtypeStruct((batch, n_v, rows), BF16),
        ],
        compiler_params=_params(2),
        name="qkv_project",
    )(x, mod, ln, w_qk, w_vt, cosf, sinf, qn, kn)


def _chan_dft_kernel(x_ref, mod_ref, ln_ref, w_ref, oc_ref, os_ref):
    h = _norm_mod(x_ref[0], ln_ref[...], mod_ref[0], SH1, SC1)
    d = h.shape[1]
    gd = w_ref.shape[0]
    for g in range(d // gd):
        y = _dot(h[:, g * gd:(g + 1) * gd], w_ref[...])
        oc_ref[0, :, g * gd:(g + 1) * gd] = y[:, :gd].astype(BF16)
        os_ref[0, :, g * gd:(g + 1) * gd] = y[:, gd:].astype(BF16)


def _channel_dft(x, mod, mod_row, ln, w, row_tile):
    batch, rows, d = x.shape
    row = lambda b, j: (b, j, 0)
    return pl.pallas_call(
        _chan_dft_kernel,
        grid=(batch, rows // row_tile),
        in_specs=[
            pl.BlockSpec((1, row_tile, d), row),
            _mod_spec(d, mod_row),
            _resident((1, d)),
            _resident(w.shape),
        ],
        out_specs=[pl.BlockSpec((1, row_tile, d), row)] * 2,
        out_shape=[jax.ShapeDtypeStruct((batch, rows, d), BF16)] * 2,
        compiler_params=_params(2),
        name="channel_dft",
    )(x, mod, ln, w)


def _kv_chunks(klat_ref, kctx_ref, vtlat_ref, vtctx_ref, n_lat_chunks):
    def pick(lat_ref, ctx_ref, c):
        return (lat_ref, c) if c < n_lat_chunks else (ctx_ref, c - n_lat_chunks)

    def key_chunk(c, kvh):
        ref, cc = pick(klat_ref, kctx_ref, c)
        return ref[0, cc * KEY_CHUNK:(cc + 1) * KEY_CHUNK, kvh * HEAD_W:(kvh + 1) * HEAD_W]

    def val_chunk(c, kvh):
        ref, cc = pick(vtlat_ref, vtctx_ref, c)
        return ref[0, kvh * HEAD_W:(kvh + 1) * HEAD_W, cc * KEY_CHUNK:(cc + 1) * KEY_CHUNK]

    return key_chunk, val_chunk


def _score_chunk(key_chunk, s_scr, q, kvh, buf, c, n_chunks, m8):
    tq = q.shape[0]
    s = _dot_nt(key_chunk(c, kvh), q)
    lo = (buf * n_chunks + c) * KEY_CHUNK
    s_scr[lo:lo + KEY_CHUNK, :] = s
    return jnp.maximum(m8, jnp.max(s.reshape(KEY_CHUNK // SUBLANES, SUBLANES, tq), axis=0))


def _softmax_pv_streams(score_list, weigh_list, offset, key_chunk, val_chunk, s_scr, m8_carried, n_chunks,
                        on_done):
    tq = score_list[0][0].shape[0]
    ones_rows = jnp.ones((ONES_ROWS, KEY_CHUNK), BF16)

    def weigh(kvh, buf, c, mx, acc):
        lo = (buf * n_chunks + c) * KEY_CHUNK
        e = jnp.exp2(s_scr[lo:lo + KEY_CHUNK, :] - mx)
        v_ones = jnp.concatenate([val_chunk(c, kvh), ones_rows], axis=0)
        return acc + _dot(v_ones, e.astype(BF16))

    m8 = [None] * len(score_list)
    mx = acc = None
    n_steps = max(len(score_list) * n_chunks, offset + len(weigh_list) * n_chunks)
    for g in range(n_steps):
        if g < len(score_list) * n_chunks:
            j, c = divmod(g, n_chunks)
            q, kvh, buf = score_list[j]
            if c == 0:
                m8[j] = jnp.full((SUBLANES, tq), -jnp.inf, F32)
            m8[j] = _score_chunk(key_chunk, s_scr, q, kvh, buf, c, n_chunks, m8[j])
        ge = g - offset
        if 0 <= ge < len(weigh_list) * n_chunks:
            p, c = divmod(ge, n_chunks)
            kvh, buf, max_src = weigh_list[p]
            if c == 0:
                src = m8_carried if max_src is None else m8[max_src]
                mx = jnp.max(src, axis=0, keepdims=True)
                acc = jnp.zeros((HEAD_W + ONES_ROWS, tq), F32)
            acc = weigh(kvh, buf, c, mx, acc)
            if c == n_chunks - 1:
                on_done(p, acc[:HEAD_W] * (1.0 / acc[HEAD_W:HEAD_W + 1]))
    return m8


def _head_problems(q_ref, diff, heads_per_step, kv_heads_per_step, n_heads=None):
    problems = []
    for hh in range(heads_per_step if n_heads is None else n_heads):
        q = q_ref[0, :, hh * HEAD_W:(hh + 1) * HEAD_W]
        kvh = hh * kv_heads_per_step // heads_per_step
        if diff:
            lane = lax.broadcasted_iota(jnp.int32, q.shape, 1)
            first = (lane % (HEAD_W // 2)) < (HEAD_W // 4)
            zero = jnp.zeros_like(q)
            problems += [(jnp.where(first, q, zero), kvh), (jnp.where(first, zero, q), kvh)]
        else:
            problems.append((q, kvh))
    return problems


def _head_writer(o_ref, lam_ref, subln_ref, diff, lambda_init):
    first_comp = {}

    def on_done(p, o_t):
        if not diff:
            o_ref[0, :, p * HEAD_W:(p + 1) * HEAD_W] = o_t.T.astype(BF16)
            return
        hh = p // 2
        if p % 2 == 0:
            first_comp[hh] = o_t
            return
        lv = lam_ref[...]
        lam = (jnp.exp(jnp.sum(lv[0:1] * lv[1:2], axis=-1, keepdims=True))
               - jnp.exp(jnp.sum(lv[2:3] * lv[3:4], axis=-1, keepdims=True)) + lambda_init)
        o = (first_comp.pop(hh) - lam * o_t).T
        o = _rms(o, subln_ref[...]) * (1.0 - lambda_init)
        o_ref[0, :, hh * HEAD_W:(hh + 1) * HEAD_W] = o.astype(BF16)

    return on_done


def _attn_latent_kernel(lam_ref, subln_ref, q_ref, qn_ref, klat_ref, kctx_ref, vtlat_ref, vtctx_ref, o_ref,
                        s_scr, m8_scr, *, diff, lambda_init, heads_per_step, kv_heads_per_step):
    n_lat_chunks = klat_ref.shape[1] // KEY_CHUNK
    n_chunks = n_lat_chunks + kctx_ref.shape[1] // KEY_CHUNK
    key_chunk, val_chunk = _kv_chunks(klat_ref, kctx_ref, vtlat_ref, vtctx_ref, n_lat_chunks)
    cur = _head_problems(q_ref, diff, heads_per_step, kv_heads_per_step)
    nxt_q, nxt_kvh = _head_problems(qn_ref, diff, heads_per_step, kv_heads_per_step, n_heads=1)[0]
    n_prob = len(cur)
    assert n_prob % 2 == 0

    @pl.when(pl.program_id(2) == 0)
    def _():
        q, kvh = cur[0]
        m8 = jnp.full((SUBLANES, q.shape[0]), -jnp.inf, F32)
        for c in range(n_chunks):
            m8 = _score_chunk(key_chunk, s_scr, q, kvh, 0, c, n_chunks, m8)
        m8_scr[...] = m8

    score_list = [(q, kvh, p % 2) for p, (q, kvh) in enumerate(cur) if p > 0] + [(nxt_q, nxt_kvh, 0)]
    weigh_list = [(kvh, p % 2, None if p == 0 else p - 1) for p, (_, kvh) in enumerate(cur)]
    m8 = _softmax_pv_streams(
        score_list, weigh_list, min(SCORE_LEAD, n_chunks - 1), key_chunk, val_chunk, s_scr, m8_scr[...],
        n_chunks, _head_writer(o_ref, lam_ref, subln_ref, diff, lambda_init))
    m8_scr[...] = m8[-1]


def _attn_ctx_kernel(lam_ref, subln_ref, q_ref, k_ref, vt_ref, o_ref, s_scr,
                     *, diff, lambda_init, heads_per_step, kv_heads_per_step):
    n_chunks = k_ref.shape[1] // KEY_CHUNK
    key_chunk, val_chunk = _kv_chunks(None, k_ref, None, vt_ref, 0)
    cur = _head_problems(q_ref, diff, heads_per_step, kv_heads_per_step)
    score_list = [(q, kvh, p % 2) for p, (q, kvh) in enumerate(cur)]
    weigh_list = [(kvh, p % 2, p) for p, (_, kvh) in enumerate(cur)]
    _softmax_pv_streams(
        score_list, weigh_list, n_chunks + min(SCORE_LEAD, n_chunks - 1), key_chunk, val_chunk, s_scr, None,
        n_chunks, _head_writer(o_ref, lam_ref, subln_ref, diff, lambda_init))


def _attention(q, k, vt, q_ctx, k_ctx, vt_ctx, lam, subln, *, diff, lambda_init, n_heads, heads_per_step,
               kv_heads_per_step, need_ctx):
    batch, n_lat, _ = q.shape
    n_ctx = k_ctx.shape[1]
    qw = heads_per_step * HEAD_W
    kw = kv_heads_per_step * HEAD_W
    n_groups = n_heads // heads_per_step
    n_lat_tiles = n_lat // ROW_TILE
    static = dict(diff=diff, lambda_init=lambda_init, heads_per_step=heads_per_step,
                  kv_heads_per_step=kv_heads_per_step)
    name = "diff_attention" if diff else "gqa_attention"
    o_lat = pl.pallas_call(
        functools.partial(_attn_latent_kernel, **static),
        grid=(batch, n_groups, n_lat_tiles),
        in_specs=[
            _resident(lam.shape),
            _resident(subln.shape),
            pl.BlockSpec((1, ROW_TILE, qw), lambda b, h, i: (b, i, h)),
            pl.BlockSpec((1, ROW_TILE, qw), lambda b, h, i: (b, jnp.minimum(i + 1, n_lat_tiles - 1), h)),
            pl.BlockSpec((1, n_lat, kw), lambda b, h, i: (b, 0, h)),
            pl.BlockSpec((1, n_ctx, kw), lambda b, h, i: (b, 0, h)),
            pl.BlockSpec((1, kw, n_lat), lambda b, h, i: (b, h, 0)),
            pl.BlockSpec((1, kw, n_ctx), lambda b, h, i: (b, h, 0)),
        ],
        out_specs=pl.BlockSpec((1, ROW_TILE, qw), lambda b, h, i: (b, i, h)),
        out_shape=jax.ShapeDtypeStruct((batch, n_lat, n_heads * HEAD_W), BF16),
        scratch_shapes=[pltpu.VMEM((2 * (n_lat + n_ctx), ROW_TILE), F32), pltpu.VMEM((SUBLANES, ROW_TILE), F32)],
        compiler_params=_params(3),
        name=name,
    )(lam, subln, q, q, k, k_ctx, vt, vt_ctx)
    if not need_ctx:
        return o_lat, None
    o_ctx = pl.pallas_call(
        functools.partial(_attn_ctx_kernel, **static),
        grid=(batch, n_groups),
        in_specs=[
            _resident(lam.shape),
            _resident(subln.shape),
            pl.BlockSpec((1, n_ctx, qw), lambda b, h: (b, 0, h)),
            pl.BlockSpec((1, n_ctx, kw), lambda b, h: (b, 0, h)),
            pl.BlockSpec((1, kw, n_ctx), lambda b, h: (b, h, 0)),
        ],
        out_specs=pl.BlockSpec((1, n_ctx, qw), lambda b, h: (b, 0, h)),
        out_shape=jax.ShapeDtypeStruct((batch, n_ctx, n_heads * HEAD_W), BF16),
        scratch_shapes=[pltpu.VMEM((2 * n_ctx, n_ctx), F32)],
        compiler_params=_params(2),
        name=name + "_ctx",
    )(lam, subln, q_ctx, k_ctx, vt_ctx)
    return o_lat, o_ctx


DFT_COL_TILE = 1024
DFT_EXTRA_ROWS = 16


def _dft_mirror_kernel(cl_ref, sl_ref, ch_ref, flip_ref, xc_ref, xs_ref, o_ref, carry_scr, *, n_steps):
    s = pl.program_id(2)
    t = n_steps - 1 - s

    @pl.when(s == 0)
    def _():
        carry_scr[...] = _dot(ch_ref[...], xc_ref[0])

    a = _dot(cl_ref[...], xc_ref[0])
    b = _dot(sl_ref[...], xs_ref[0])
    total = a + b
    row = lax.broadcasted_iota(jnp.int32, total.shape, 0)
    mirror = _dot(flip_ref[...], total.astype(BF16))
    mirror = jnp.where(row == 0, carry_scr[0:1, :], mirror)
    carry_scr[0:1, :] = total[0:1]
    o_ref[0, pl.ds(pl.multiple_of(t * ROW_TILE, ROW_TILE), ROW_TILE), :] = (a - b).astype(BF16)
    lo = pl.multiple_of((2 * n_steps - 1 - t) * ROW_TILE, ROW_TILE)
    o_ref[0, pl.ds(lo, ROW_TILE), :] = mirror.astype(BF16)


def _position_dft(xc, xs):
    batch, n, d = xc.shape
    half = n // 2
    n_steps = half // ROW_TILE
    cl, sl = _dft_tables(n, n ** -0.5, n_rows=half, negate_sin=False)
    alt = jnp.where(jnp.arange(n, dtype=jnp.int32) % 2 == 0, n ** -0.5, -(n ** -0.5)).astype(BF16)
    c_half = jnp.zeros((DFT_EXTRA_ROWS, n), BF16).at[0].set(alt)
    flip = np.zeros((ROW_TILE, ROW_TILE), np.float32)
    flip[np.arange(1, ROW_TILE), ROW_TILE - np.arange(1, ROW_TILE)] = 1.0
    tile_rows = lambda b, c, s: (n_steps - 1 - s, 0)
    return pl.pallas_call(
        functools.partial(_dft_mirror_kernel, n_steps=n_steps),
        grid=(batch, d // DFT_COL_TILE, n_steps),
        in_specs=[
            pl.BlockSpec((ROW_TILE, n), tile_rows),
            pl.BlockSpec((ROW_TILE, n), tile_rows),
            _resident((DFT_EXTRA_ROWS, n)),
            _resident((ROW_TILE, ROW_TILE)),
            pl.BlockSpec((1, n, DFT_COL_TILE), lambda b, c, s: (b, 0, c), pipeline_mode=pl.Buffered(1)),
            pl.BlockSpec((1, n, DFT_COL_TILE), lambda b, c, s: (b, 0, c), pipeline_mode=pl.Buffered(1)),
        ],
        out_specs=pl.BlockSpec((1, n, DFT_COL_TILE), lambda b, c, s: (b, 0, c)),
        out_shape=jax.ShapeDtypeStruct((batch, n, d), BF16),
        scratch_shapes=[pltpu.VMEM((DFT_EXTRA_ROWS, DFT_COL_TILE), F32)],
        compiler_params=_params(3),
        name="position_dft",
    )(cl, sl, c_half, jnp.asarray(flip, dtype=BF16), xc, xs)


def _dft_small_kernel(c_ref, s_ref, xc_ref, xs_ref, o_ref):
    o_ref[0] = (_dot(c_ref[...], xc_ref[0]) + _dot(s_ref[...], xs_ref[0])).astype(BF16)


def _position_dft_small(xc, xs):
    batch, n, d = xc.shape
    cc, sc = _dft_tables(n, n ** -0.5)
    whole = lambda b: (b, 0, 0)
    return pl.pallas_call(
        _dft_small_kernel,
        grid=(batch,),
        in_specs=[
            _resident((n, n)),
            _resident((n, n)),
            pl.BlockSpec((1, n, d), whole),
            pl.BlockSpec((1, n, d), whole),
        ],
        out_specs=pl.BlockSpec((1, n, d), whole),
        out_shape=jax.ShapeDtypeStruct((batch, n, d), BF16),
        compiler_params=_params(1),
        name="position_dft_ctx",
    )(cc, sc, xc, xs)


def _out_ffn_kernel(a_ref, x_ref, wo_ref, bo_ref, mod_ref, ln_ref, wgu_ref, wd_ref, fn_ref, o_ref, a_scr,
                    *, d_ff, final_norm):
    mod = mod_ref[0]
    x = x_ref[0] + mod[G1:G1 + 1] * (_dot(a_ref[0], wo_ref[...]) + bo_ref[...])
    h = _norm_mod(x, ln_ref[...], mod, SH2, SC2)
    for lo in range(0, d_ff, FF_CHUNK):
        hi = min(lo + FF_CHUNK, d_ff)
        g = _dot(h, wgu_ref[0, :, lo:hi])
        u = _dot(h, wgu_ref[0, :, d_ff + lo:d_ff + hi])
        a_scr[:, lo:hi] = (_silu(g) * u).astype(BF16)
    y = x + mod[G2:G2 + 1] * _dot(a_scr[...], wd_ref[0])
    if final_norm:
        y = _rms(y, fn_ref[...])
    o_ref[0] = y


def _out_ffn(a, x, mod, mod_row, w_o, b_o, ln, w_gu, w_down, layer, final_g, *, row_tile, final_norm):
    batch, rows, d = x.shape
    k = a.shape[2]
    d_ff = w_down.shape[1]
    row = lambda b, j: (b, j, 0)
    layer_block = lambda shape: pl.BlockSpec((1,) + shape, lambda b, j: (layer, 0, 0), pipeline_mode=pl.Buffered(1))
    return pl.pallas_call(
        functools.partial(_out_ffn_kernel, d_ff=d_ff, final_norm=final_norm),
        grid=(batch, rows // row_tile),
        in_specs=[
            pl.BlockSpec((1, row_tile, k), row),
            pl.BlockSpec((1, row_tile, d), row),
            _resident((k, d)),
            _resident((1, d)),
            _mod_spec(d, mod_row),
            _resident((1, d)),
            layer_block((d, 2 * d_ff)),
            layer_block((d_ff, d)),
            _resident((1, d)),
        ],
        out_specs=pl.BlockSpec((1, row_tile, d), row),
        out_shape=jax.ShapeDtypeStruct((batch, rows, d), F32),
        scratch_shapes=[pltpu.VMEM((row_tile, d_ff), BF16)],
        compiler_params=_params(2),
        name="out_ffn",
    )(a, x, w_o, b_o, mod, ln, w_gu, w_down, final_g)


def _half_split(w, n_comp, comp_dim):
    lead = w.shape[:-1]
    heads = w.shape[-1] // HEAD_W
    w = w.reshape(lead + (heads, n_comp, comp_dim // 2, 2))
    return jnp.moveaxis(w, -1, -3).reshape(lead + (heads * HEAD_W,))


def _rope_lane_tables(seq, head_dim):
    axis_dim = head_dim // 2
    rows = jnp.arange(seq, dtype=jnp.int32) // GRID_W
    cols = jnp.arange(seq, dtype=jnp.int32) % GRID_W
    inv_freq = ROPE_THETA ** (-jnp.arange(0, axis_dim, 2, dtype=F32) / axis_dim)
    ang = jnp.concatenate([rows[:, None].astype(F32) * inv_freq,
                           cols[:, None].astype(F32) * inv_freq], axis=-1)
    reps = (HEAD_W // 2) // (head_dim // 2)
    cos = jnp.tile(jnp.cos(ang), (1, 2 * reps))
    sin = jnp.tile(jnp.sin(ang), (1, reps))
    return cos, jnp.concatenate([-sin, sin], axis=-1)


def _dft_tables(n, scale, n_rows=None, negate_sin=True):
    n_rows = n if n_rows is None else n_rows
    r = 1
    while r * r < n:
        r *= 2
    q = n // r
    k = jnp.arange(n_rows, dtype=jnp.int32)[:, None]
    a_idx = (k * jnp.arange(q, dtype=jnp.int32)[None, :] * r) % n
    b_idx = (k * jnp.arange(r, dtype=jnp.int32)[None, :]) % n
    w = 2.0 * math.pi / n
    ca, sa = jnp.cos(a_idx.astype(F32) * w), jnp.sin(a_idx.astype(F32) * w)
    cb, sb = jnp.cos(b_idx.astype(F32) * w) * scale, jnp.sin(b_idx.astype(F32) * w) * scale
    cos = ca[:, :, None] * cb[:, None, :] - sa[:, :, None] * sb[:, None, :]
    sin = sa[:, :, None] * cb[:, None, :] + ca[:, :, None] * sb[:, None, :]
    sin = -sin if negate_sin else sin
    return cos.reshape(n_rows, n).astype(BF16), sin.reshape(n_rows, n).astype(BF16)


def _channel_dft_weight(gd):
    idx = (np.arange(gd)[:, None] * np.arange(gd)[None, :]) % gd
    ang = 2.0 * np.pi * idx / gd
    w = np.concatenate([np.cos(ang), np.sin(ang)], axis=1) / math.sqrt(gd)
    return jnp.asarray(w, dtype=BF16)


def kernel(x, c, ctx, c_ctx, mod_w, mod_b, ln_mix, ln_ffn, ffn_w_gu, ffn_w_down, a_w_qkv, a_lam, a_subln, a_w_o, b_w_qkv, b_q_norm, b_k_norm, b_w_o, c_w_o, c_b_o, final_norm):
    batch, seq, d = x.shape
    n_ctx = ctx.shape[1]
    depth = mod_w.shape[0]
    assert seq % FFN_ROW_TILE == 0 and n_ctx == ROW_TILE and batch < MOD_ROWS

    cs = jnp.concatenate([c, c_ctx[None, :], jnp.zeros((MOD_ROWS - batch - 1, d), F32)], axis=0)
    mod_all = _modulation(cs, mod_w, mod_b).reshape(depth, MOD_ROWS, 6, d)
    lat = dict(mod_row=None, row_tile=BIG_ROW_TILE)
    con = dict(mod_row=batch, row_tile=n_ctx)

    rope_a = _rope_lane_tables(seq, DIFF_HEAD_DIM)
    rope_b = _rope_lane_tables(seq, GQA_HEAD_DIM)
    rope_ctx = (jnp.ones((n_ctx, HEAD_W), F32), jnp.zeros((n_ctx, HEAD_W), F32))
    ones_head = jnp.ones((1, HEAD_W), F32)
    no_bias = jnp.zeros((1, d), F32)

    w_gu_all, w_down_all = ffn_w_gu.astype(BF16), ffn_w_down.astype(BF16)
    x_lat, x_ctx = x, ctx
    for i in range(depth):
        last = i == depth - 1
        kind, j = i % N_MIXERS, i // N_MIXERS
        mod = mod_all[i]
        ln1 = ln_mix[i][None, :]

        if kind == 2:
            w_chan = _channel_dft_weight(d // FNET_GROUPS)
            mixed = _position_dft(*_channel_dft(x_lat, mod, None, ln1, w_chan, BIG_ROW_TILE))
            if not last:
                mixed_ctx = _position_dft_small(*_channel_dft(x_ctx, mod, batch, ln1, w_chan, n_ctx))
            w_o, b_o = c_w_o[j], c_b_o[j][None, :]
        else:
            if kind == 0:
                n_q = n_k = DIFF_HEADS * HEAD_W
                split, rope, w = (2, DIFF_HEAD_DIM), rope_a, a_w_qkv[j]
                proj = dict(qn=ones_head, kn=ones_head, n_q=n_q, n_k=n_k, head_norm=False,
                            q_scale=DIFF_HEAD_DIM ** -0.5 * LOG2E)
                attn = dict(lam=a_lam[j], subln=a_subln[j][None, :], diff=True, lambda_init=_lambda_init(i),
                            n_heads=DIFF_HEADS, heads_per_step=4, kv_heads_per_step=4)
                w_o = a_w_o[j]
            else:
                n_q, n_k = GQA_HEADS * HEAD_W, GQA_KV_HEADS * HEAD_W
                split, rope, w = (1, GQA_HEAD_DIM), rope_b, b_w_qkv[j]
                proj = dict(qn=_half_split(b_q_norm[j], *split)[None, :], kn=_half_split(b_k_norm[j], *split)[None, :],
                            n_q=n_q, n_k=n_k, head_norm=True, q_scale=GQA_HEAD_DIM ** -0.5 * LOG2E)
                attn = dict(lam=jnp.zeros((4, DIFF_HEAD_DIM), F32), subln=ones_head, diff=False, lambda_init=0.0,
                            n_heads=GQA_HEADS, heads_per_step=GQA_HEADS, kv_heads_per_step=GQA_KV_HEADS)
                w_o = b_w_o[j]
            w_qk = _half_split(w[:, :n_q + n_k].astype(BF16), *split)
            w_vt = w[:, n_q + n_k:].T.astype(BF16)
            lat_proj = dict(lat, row_tile=ROW_TILE) if proj["head_norm"] else lat
            q, k, vt = _qkv_project(x_lat, mod, ln=ln1, w_qk=w_qk, w_vt=w_vt, cosf=rope[0], sinf=rope[1],
                                    **lat_proj, **proj)
            q_ctx, k_ctx, vt_ctx = _qkv_project(x_ctx, mod, ln=ln1, w_qk=w_qk, w_vt=w_vt, cosf=rope_ctx[0],
                                                sinf=rope_ctx[1], **con, **proj)
            mixed, mixed_ctx = _attention(q, k, vt, q_ctx, k_ctx, vt_ctx, need_ctx=not last, **attn)
            b_o = no_bias

        ffn = dict(w_o=w_o.astype(BF16), b_o=b_o, ln=ln_ffn[i][None, :], w_gu=w_gu_all, w_down=w_down_all,
                   layer=i, final_g=final_norm[None, :], final_norm=last)
        if not last:
            x_ctx = _out_ffn(mixed_ctx, x_ctx, mod, **con, **ffn)
        x_lat = _out_ffn(mixed, x_lat, mod, **dict(lat, row_tile=FFN_ROW_TILE), **ffn)
    return x_lat
```

```python
import functools
import math

import numpy as np
import jax
import jax.numpy as jnp
from jax import lax
from jax.experimental import pallas as pl
from jax.experimental.pallas import tpu as pltpu

GRID_W = 64
ROPE_THETA = 10000.0
NORM_EPS = 1e-6
DIFF_HEADS = 8
DIFF_HEAD_DIM = 64
GQA_HEADS = 8
GQA_KV_HEADS = 2
GQA_HEAD_DIM = 128
FNET_GROUPS = 4
N_MIXERS = 3

LANES = 128
SUBLANES = 8
VMEM_LIMIT = 56 * 1024 * 1024

ROW_TILE = 256
BIG_ROW_TILE = 1024
FFN_ROW_TILE = 1024
HEAD_W = 128
KEY_CHUNK = 256
SCORE_LEAD = 4
FF_CHUNK = 256
ONES_ROWS = 16
MOD_ROWS = 16
MOD_COL_TILE = 1536
DIFF_HEADS_PER_STEP = 4
LOG2E = math.log2(math.e)

BF16 = jnp.bfloat16
F32 = jnp.float32

SH1, SC1, G1, SH2, SC2, G2 = range(6)


def _lambda_init(layer_idx):
    return 0.8 - 0.6 * float(np.exp(-0.3 * layer_idx))


def _params(n_grid):
    return pltpu.CompilerParams(
        dimension_semantics=("arbitrary",) * n_grid, vmem_limit_bytes=VMEM_LIMIT)


def _resident(shape):
    nd = len(shape)
    return pl.BlockSpec(shape, lambda *_: (0,) * nd, pipeline_mode=pl.Buffered(1))


def _dot(a, b):
    return jnp.dot(a, b, preferred_element_type=F32)


def _dot_nt(a, b):
    return lax.dot_general(a, b, (((1,), (1,)), ((), ())), preferred_element_type=F32)


def _rms(x, g):
    return x * lax.rsqrt(jnp.mean(x * x, axis=-1, keepdims=True) + NORM_EPS) * g


def _silu(x):
    return x / (1.0 + jnp.exp(-x))


def _mod_kernel(cs_ref, w_ref, b_ref, o_ref):
    s = _silu(cs_ref[...]).astype(BF16)
    o_ref[0] = _dot(s, w_ref[0].astype(BF16)) + b_ref[0]


def _modulation(cs, mod_w, mod_b):
    depth, d, n = mod_w.shape
    tn = MOD_COL_TILE
    return pl.pallas_call(
        _mod_kernel,
        grid=(depth, n // tn),
        in_specs=[
            pl.BlockSpec((MOD_ROWS, d), lambda i, j: (0, 0)),
            pl.BlockSpec((1, d, tn), lambda i, j: (i, 0, j)),
            pl.BlockSpec((1, 1, tn), lambda i, j: (i, 0, j)),
        ],
        out_specs=pl.BlockSpec((1, MOD_ROWS, tn), lambda i, j: (i, 0, j)),
        out_shape=jax.ShapeDtypeStruct((depth, MOD_ROWS, n), F32),
        compiler_params=_params(2),
        name="modulation",
    )(cs, mod_w, mod_b.reshape(depth, 1, n))


def _mod_spec(d, mod_row):
    if mod_row is None:
        return pl.BlockSpec((1, 6, d), lambda b, j: (b, 0, 0))
    return pl.BlockSpec((1, 6, d), lambda b, j: (mod_row, 0, 0))


def _norm_mod(x, ln, mod, shift, scale):
    y = _rms(x, ln)
    return (y * (1.0 + mod[scale:scale + 1]) + mod[shift:shift + 1]).astype(BF16)


def _qkv_kernel(x_ref, mod_ref, ln_ref, w_ref, wvt_ref, cos_ref, sin_ref, qn_ref, kn_ref,
                q_ref, k_ref, vt_ref, *, n_q, n_k, head_norm, q_scale):
    h = _norm_mod(x_ref[0], ln_ref[...], mod_ref[0], SH1, SC1)
    cos, sin = cos_ref[...], sin_ref[...]
    for head in range((n_q + n_k) // HEAD_W):
        is_q = head < n_q // HEAD_W
        if head % 2 == 0:
            pair = _dot(h, w_ref[:, head * HEAD_W:(head + 2) * HEAD_W])
        t = pair[:, (head % 2) * HEAD_W:(head % 2 + 1) * HEAD_W]
        if head_norm:
            t = _rms(t, qn_ref[...] if is_q else kn_ref[...])
        t = t * cos + pltpu.roll(t, HEAD_W // 2, 1) * sin
        if is_q:
            q_ref[0, :, head * HEAD_W:(head + 1) * HEAD_W] = (t * q_scale).astype(BF16)
        else:
            c0 = head * HEAD_W - n_q
            k_ref[0, :, c0:c0 + HEAD_W] = t.astype(BF16)
    vt_ref[0] = _dot_nt(wvt_ref[...], h).astype(BF16)


def _qkv_project(x, mod, mod_row, ln, w_qk, w_vt, cosf, sinf, qn, kn, *, row_tile, n_q, n_k, head_norm,
                 q_scale):
    batch, rows, d = x.shape
    n_v = w_vt.shape[0]
    row = lambda b, j: (b, j, 0)
    kern = functools.partial(_qkv_kernel, n_q=n_q, n_k=n_k, head_norm=head_norm, q_scale=q_scale)
    return pl.pallas_call(
        kern,
        grid=(batch, rows // row_tile),
        in_specs=[
            pl.BlockSpec((1, row_tile, d), row),
            _mod_spec(d, mod_row),
            _resident((1, d)),
            _resident((d, n_q + n_k)),
            _resident((n_v, d)),
            pl.BlockSpec((row_tile, HEAD_W), lambda b, j: (j, 0)),
            pl.BlockSpec((row_tile, HEAD_W), lambda b, j: (j, 0)),
            _resident((1, HEAD_W)),
            _resident((1, HEAD_W)),
        ],
        out_specs=[
            pl.BlockSpec((1, row_tile, n_q), row),
            pl.BlockSpec((1, row_tile, n_k), row),
            pl.BlockSpec((1, n_v, row_tile), lambda b, j: (b, 0, j)),
        ],
        out_shape=[
            jax.ShapeDtypeStruct((batch, rows, n_q), BF16),
            jax.ShapeDtypeStruct((batch, rows, n_k), BF16),
            jax.ShapeDtypeStruct((batch, n_v, rows), BF16),
        ],
        compiler_params=_params(2),
        name="qkv_project",
    )(x, mod, ln, w_qk, w_vt, cosf, sinf, qn, kn)


def _chan_dft_kernel(x_ref, mod_ref, ln_ref, w_ref, oc_ref, os_ref):
    h = _norm_mod(x_ref[0], ln_ref[...], mod_ref[0], SH1, SC1)
    d = h.shape[1]
    gd = w_ref.shape[0]
    for g in range(d // gd):
        y = _dot(h[:, g * gd:(g + 1) * gd], w_ref[...])
        oc_ref[0, :, g * gd:(g + 1) * gd] = y[:, :gd].astype(BF16)
        os_ref[0, :, g * gd:(g + 1) * gd] = y[:, gd:].astype(BF16)


def _channel_dft(x, mod, mod_row, ln, w, row_tile):
    batch, rows, d = x.shape
    row = lambda b, j: (b, j, 0)
    return pl.pallas_call(
        _chan_dft_kernel,
        grid=(batch, rows // row_tile),
        in_specs=[
            pl.BlockSpec((1, row_tile, d), row),
            _mod_spec(d, mod_row),
            _resident((1, d)),
            _resident(w.shape),
        ],
        out_specs=[pl.BlockSpec((1, row_tile, d), row)] * 2,
        out_shape=[jax.ShapeDtypeStruct((batch, rows, d), BF16)] * 2,
        compiler_params=_params(2),
        name="channel_dft",
    )(x, mod, ln, w)


def _kv_chunks(klat_ref, kctx_ref, vtlat_ref, vtctx_ref, n_lat_chunks):
    def pick(lat_ref, ctx_ref, c):
        return (lat_ref, c) if c < n_lat_chunks else (ctx_ref, c - n_lat_chunks)

    def key_chunk(c, kvh):
        ref, cc = pick(klat_ref, kctx_ref, c)
        return ref[0, cc * KEY_CHUNK:(cc + 1) * KEY_CHUNK, kvh * HEAD_W:(kvh + 1) * HEAD_W]

    def val_chunk(c, kvh):
        ref, cc = pick(vtlat_ref, vtctx_ref, c)
        return ref[0, kvh * HEAD_W:(kvh + 1) * HEAD_W, cc * KEY_CHUNK:(cc + 1) * KEY_CHUNK]

    return key_chunk, val_chunk


def _score_chunk(key_chunk, s_scr, q, kvh, buf, c, n_chunks, m8):
    tq = q.shape[0]
    s = _dot_nt(key_chunk(c, kvh), q)
    lo = (buf * n_chunks + c) * KEY_CHUNK
    s_scr[lo:lo + KEY_CHUNK, :] = s
    return jnp.maximum(m8, jnp.max(s.reshape(KEY_CHUNK // SUBLANES, SUBLANES, tq), axis=0))


def _softmax_pv_streams(score_list, weigh_list, offset, key_chunk, val_chunk, s_scr, m8_carried, n_chunks,
                        on_done):
    tq = score_list[0][0].shape[0]
    ones_rows = jnp.ones((ONES_ROWS, KEY_CHUNK), BF16)

    def weigh(kvh, buf, c, mx, acc):
        lo = (buf * n_chunks + c) * KEY_CHUNK
        e = jnp.exp2(s_scr[lo:lo + KEY_CHUNK, :] - mx)
        v_ones = jnp.concatenate([val_chunk(c, kvh), ones_rows], axis=0)
        return acc + _dot(v_ones, e.astype(BF16))

    m8 = [None] * len(score_list)
    mx = acc = None
    n_steps = max(len(score_list) * n_chunks, offset + len(weigh_list) * n_chunks)
    for g in range(n_steps):
        if g < len(score_list) * n_chunks:
            j, c = divmod(g, n_chunks)
            q, kvh, buf = score_list[j]
            if c == 0:
                m8[j] = jnp.full((SUBLANES, tq), -jnp.inf, F32)
            m8[j] = _score_chunk(key_chunk, s_scr, q, kvh, buf, c, n_chunks, m8[j])
        ge = g - offset
        if 0 <= ge < len(weigh_list) * n_chunks:
            p, c = divmod(ge, n_chunks)
            kvh, buf, max_src = weigh_list[p]
            if c == 0:
                src = m8_carried if max_src is None else m8[max_src]
                mx = jnp.max(src, axis=0, keepdims=True)
                acc = jnp.zeros((HEAD_W + ONES_ROWS, tq), F32)
            acc = weigh(kvh, buf, c, mx, acc)
            if c == n_chunks - 1:
                on_done(p, acc[:HEAD_W] * (1.0 / acc[HEAD_W:HEAD_W + 1]))
    return m8


def _head_problems(q_ref, diff, heads_per_step, kv_heads_per_step, n_heads=None):
    problems = []
    for hh in range(heads_per_step if n_heads is None else n_heads):
        q = q_ref[0, :, hh * HEAD_W:(hh + 1) * HEAD_W]
        kvh = hh * kv_heads_per_step // heads_per_step
        if diff:
            lane = lax.broadcasted_iota(jnp.int32, q.shape, 1)
            first = (lane % (HEAD_W // 2)) < (HEAD_W // 4)
            zero = jnp.zeros_like(q)
            problems += [(jnp.where(first, q, zero), kvh), (jnp.where(first, zero, q), kvh)]
        else:
            problems.append((q, kvh))
    return problems


def _head_writer(o_ref, lam_ref, subln_ref, diff, lambda_init):
    first_comp = {}

    def on_done(p, o_t):
        if not diff:
            o_ref[0, :, p * HEAD_W:(p + 1) * HEAD_W] = o_t.T.astype(BF16)
            return
        hh = p // 2
        if p % 2 == 0:
            first_comp[hh] = o_t
            return
        lv = lam_ref[...]
        lam = (jnp.exp(jnp.sum(lv[0:1] * lv[1:2], axis=-1, keepdims=True))
               - jnp.exp(jnp.sum(lv[2:3] * lv[3:4], axis=-1, keepdims=True)) + lambda_init)
        o = (first_comp.pop(hh) - lam * o_t).T
        o = _rms(o, subln_ref[...]) * (1.0 - lambda_init)
        o_ref[0, :, hh * HEAD_W:(hh + 1) * HEAD_W] = o.astype(BF16)

    return on_done


def _attn_latent_kernel(lam_ref, subln_ref, q_ref, qn_ref, klat_ref, kctx_ref, vtlat_ref, vtctx_ref, o_ref,
                        s_scr, m8_scr, *, diff, lambda_init, heads_per_step, kv_heads_per_step):
    n_lat_chunks = klat_ref.shape[1] // KEY_CHUNK
    n_chunks = n_lat_chunks + kctx_ref.shape[1] // KEY_CHUNK
    key_chunk, val_chunk = _kv_chunks(klat_ref, kctx_ref, vtlat_ref, vtctx_ref, n_lat_chunks)
    cur = _head_problems(q_ref, diff, heads_per_step, kv_heads_per_step)
    nxt_q, nxt_kvh = _head_problems(qn_ref, diff, heads_per_step, kv_heads_per_step, n_heads=1)[0]
    n_prob = len(cur)
    assert n_prob % 2 == 0

    @pl.when(pl.program_id(2) == 0)
    def _():
        q, kvh = cur[0]
        m8 = jnp.full((SUBLANES, q.shape[0]), -jnp.inf, F32)
        for c in range(n_chunks):
            m8 = _score_chunk(key_chunk, s_scr, q, kvh, 0, c, n_chunks, m8)
        m8_scr[...] = m8

    score_list = [(q, kvh, p % 2) for p, (q, kvh) in enumerate(cur) if p > 0] + [(nxt_q, nxt_kvh, 0)]
    weigh_list = [(kvh, p % 2, None if p == 0 else p - 1) for p, (_, kvh) in enumerate(cur)]
    m8 = _softmax_pv_streams(
        score_list, weigh_list, min(SCORE_LEAD, n_chunks - 1), key_chunk, val_chunk, s_scr, m8_scr[...],
        n_chunks, _head_writer(o_ref, lam_ref, subln_ref, diff, lambda_init))
    m8_scr[...] = m8[-1]


def _attn_ctx_kernel(lam_ref, subln_ref, q_ref, k_ref, vt_ref, o_ref, s_scr,
                     *, diff, lambda_init, heads_per_step, kv_heads_per_step):
    n_chunks = k_ref.shape[1] // KEY_CHUNK
    key_chunk, val_chunk = _kv_chunks(None, k_ref, None, vt_ref, 0)
    cur = _head_problems(q_ref, diff, heads_per_step, kv_heads_per_step)
    score_list = [(q, kvh, p % 2) for p, (q, kvh) in enumerate(cur)]
    weigh_list = [(kvh, p % 2, p) for p, (_, kvh) in enumerate(cur)]
    _softmax_pv_streams(
        score_list, weigh_list, n_chunks + min(SCORE_LEAD, n_chunks - 1), key_chunk, val_chunk, s_scr, None,
        n_chunks, _head_writer(o_ref, lam_ref, subln_ref, diff, lambda_init))


def _attention(q, k, vt, q_ctx, k_ctx, vt_ctx, lam, subln, *, diff, lambda_init, n_heads, heads_per_step,
               kv_heads_per_step, need_ctx):
    batch, n_lat, _ = q.shape
    n_ctx = k_ctx.shape[1]
    qw = heads_per_step * HEAD_W
    kw = kv_heads_per_step * HEAD_W
    n_groups = n_heads // heads_per_step
    n_lat_tiles = n_lat // ROW_TILE
    static = dict(diff=diff, lambda_init=lambda_init, heads_per_step=heads_per_step,
                  kv_heads_per_step=kv_heads_per_step)
    name = "diff_attention" if diff else "gqa_attention"
    o_lat = pl.pallas_call(
        functools.partial(_attn_latent_kernel, **static),
        grid=(batch, n_groups, n_lat_tiles),
        in_specs=[
            _resident(lam.shape),
            _resident(subln.shape),
            pl.BlockSpec((1, ROW_TILE, qw), lambda b, h, i: (b, i, h)),
            pl.BlockSpec((1, ROW_TILE, qw), lambda b, h, i: (b, jnp.minimum(i + 1, n_lat_tiles - 1), h)),
            pl.BlockSpec((1, n_lat, kw), lambda b, h, i: (b, 0, h)),
            pl.BlockSpec((1, n_ctx, kw), lambda b, h, i: (b, 0, h)),
            pl.BlockSpec((1, kw, n_lat), lambda b, h, i: (b, h, 0)),
            pl.BlockSpec((1, kw, n_ctx), lambda b, h, i: (b, h, 0)),
        ],
        out_specs=pl.BlockSpec((1, ROW_TILE, qw), lambda b, h, i: (b, i, h)),
        out_shape=jax.ShapeDtypeStruct((batch, n_lat, n_heads * HEAD_W), BF16),
        scratch_shapes=[pltpu.VMEM((2 * (n_lat + n_ctx), ROW_TILE), F32), pltpu.VMEM((SUBLANES, ROW_TILE), F32)],
        compiler_params=_params(3),
        name=name,
    )(lam, subln, q, q, k, k_ctx, vt, vt_ctx)
    if not need_ctx:
        return o_lat, None
    o_ctx = pl.pallas_call(
        functools.partial(_attn_ctx_kernel, **static),
        grid=(batch, n_groups),
        in_specs=[
            _resident(lam.shape),
            _resident(subln.shape),
            pl.BlockSpec((1, n_ctx, qw), lambda b, h: (b, 0, h)),
            pl.BlockSpec((1, n_ctx, kw), lambda b, h: (b, 0, h)),
            pl.BlockSpec((1, kw, n_ctx), lambda b, h: (b, h, 0)),
        ],
        out_specs=pl.BlockSpec((1, n_ctx, qw), lambda b, h: (b, 0, h)),
        out_shape=jax.ShapeDtypeStruct((batch, n_ctx, n_heads * HEAD_W), BF16),
        scratch_shapes=[pltpu.VMEM((2 * n_ctx, n_ctx), F32)],
        compiler_params=_params(2),
        name=name + "_ctx",
    )(lam, subln, q_ctx, k_ctx, vt_ctx)
    return o_lat, o_ctx


DFT_COL_TILE = 1024
DFT_EXTRA_ROWS = 16


def _dft_mirror_kernel(cl_ref, sl_ref, ch_ref, flip_ref, xc_ref, xs_ref, o_ref, carry_scr, *, n_steps):
    s = pl.program_id(2)
    t = n_steps - 1 - s

    @pl.when(s == 0)
    def _():
        carry_scr[...] = _dot(ch_ref[...], xc_ref[0])

    a = _dot(cl_ref[...], xc_ref[0])
    b = _dot(sl_ref[...], xs_ref[0])
    total = a + b
    row = lax.broadcasted_iota(jnp.int32, total.shape, 0)
    mirror = _dot(flip_ref[...], total.astype(BF16))
    mirror = jnp.where(row == 0, carry_scr[0:1, :], mirror)
    carry_scr[0:1, :] = total[0:1]
    o_ref[0, pl.ds(pl.multiple_of(t * ROW_TILE, ROW_TILE), ROW_TILE), :] = (a - b).astype(BF16)
    lo = pl.multiple_of((2 * n_steps - 1 - t) * ROW_TILE, ROW_TILE)
    o_ref[0, pl.ds(lo, ROW_TILE), :] = mirror.astype(BF16)


def _position_dft(xc, xs):
    batch, n, d = xc.shape
    half = n // 2
    n_steps = half // ROW_TILE
    cl, sl = _dft_tables(n, n ** -0.5, n_rows=half, negate_sin=False)
    alt = jnp.where(jnp.arange(n, dtype=jnp.int32) % 2 == 0, n ** -0.5, -(n ** -0.5)).astype(BF16)
    c_half = jnp.zeros((DFT_EXTRA_ROWS, n), BF16).at[0].set(alt)
    flip = np.zeros((ROW_TILE, ROW_TILE), np.float32)
    flip[np.arange(1, ROW_TILE), ROW_TILE - np.arange(1, ROW_TILE)] = 1.0
    tile_rows = lambda b, c, s: (n_steps - 1 - s, 0)
    return pl.pallas_call(
        functools.partial(_dft_mirror_kernel, n_steps=n_steps),
        grid=(batch, d // DFT_COL_TILE, n_steps),
        in_specs=[
            pl.BlockSpec((ROW_TILE, n), tile_rows),
            pl.BlockSpec((ROW_TILE, n), tile_rows),
            _resident((DFT_EXTRA_ROWS, n)),
            _resident((ROW_TILE, ROW_TILE)),
            pl.BlockSpec((1, n, DFT_COL_TILE), lambda b, c, s: (b, 0, c), pipeline_mode=pl.Buffered(1)),
            pl.BlockSpec((1, n, DFT_COL_TILE), lambda b, c, s: (b, 0, c), pipeline_mode=pl.Buffered(1)),
        ],
        out_specs=pl.BlockSpec((1, n, DFT_COL_TILE), lambda b, c, s: (b, 0, c)),
        out_shape=jax.ShapeDtypeStruct((batch, n, d), BF16),
        scratch_shapes=[pltpu.VMEM((DFT_EXTRA_ROWS, DFT_COL_TILE), F32)],
        compiler_params=_params(3),
        name="position_dft",
    )(cl, sl, c_half, jnp.asarray(flip, dtype=BF16), xc, xs)


def _dft_small_kernel(c_ref, s_ref, xc_ref, xs_ref, o_ref):
    o_ref[0] = (_dot(c_ref[...], xc_ref[0]) + _dot(s_ref[...], xs_ref[0])).astype(BF16)


def _position_dft_small(xc, xs):
    batch, n, d = xc.shape
    cc, sc = _dft_tables(n, n ** -0.5)
    whole = lambda b: (b, 0, 0)
    return pl.pallas_call(
        _dft_small_kernel,
        grid=(batch,),
        in_specs=[
            _resident((n, n)),
            _resident((n, n)),
            pl.BlockSpec((1, n, d), whole),
            pl.BlockSpec((1, n, d), whole),
        ],
        out_specs=pl.BlockSpec((1, n, d), whole),
        out_shape=jax.ShapeDtypeStruct((batch, n, d), BF16),
        compiler_params=_params(1),
        name="position_dft_ctx",
    )(cc, sc, xc, xs)


def _out_ffn_kernel(a_ref, x_ref, wo_ref, bo_ref, mod_ref, ln_ref, wgu_ref, wd_ref, fn_ref, o_ref, a_scr,
                    *, d_ff, final_norm):
    mod = mod_ref[0]
    x = x_ref[0] + mod[G1:G1 + 1] * (_dot(a_ref[0], wo_ref[...]) + bo_ref[...])
    h = _norm_mod(x, ln_ref[...], mod, SH2, SC2)
    for lo in range(0, d_ff, FF_CHUNK):
        hi = min(lo + FF_CHUNK, d_ff)
        g = _dot(h, wgu_ref[0, :, lo:hi])
        u = _dot(h, wgu_ref[0, :, d_ff + lo:d_ff + hi])
        a_scr[:, lo:hi] = (_silu(g) * u).astype(BF16)
    y = x + mod[G2:G2 + 1] * _dot(a_scr[...], wd_ref[0])
    if final_norm:
        y = _rms(y, fn_ref[...])
    o_ref[0] = y


def _out_ffn(a, x, mod, mod_row, w_o, b_o, ln, w_gu, w_down, layer, final_g, *, row_tile, final_norm):
    batch, rows, d = x.shape
    k = a.shape[2]
    d_ff = w_down.shape[1]
    row = lambda b, j: (b, j, 0)
    layer_block = lambda shape: pl.BlockSpec((1,) + shape, lambda b, j: (layer, 0, 0), pipeline_mode=pl.Buffered(1))
    return pl.pallas_call(
        functools.partial(_out_ffn_kernel, d_ff=d_ff, final_norm=final_norm),
        grid=(batch, rows // row_tile),
        in_specs=[
            pl.BlockSpec((1, row_tile, k), row),
            pl.BlockSpec((1, row_tile, d), row),
            _resident((k, d)),
            _resident((1, d)),
            _mod_spec(d, mod_row),
            _resident((1, d)),
            layer_block((d, 2 * d_ff)),
            layer_block((d_ff, d)),
            _resident((1, d)),
        ],
        out_specs=pl.BlockSpec((1, row_tile, d), row),
        out_shape=jax.ShapeDtypeStruct((batch, rows, d), F32),
        scratch_shapes=[pltpu.VMEM((row_tile, d_ff), BF16)],
        compiler_params=_params(2),
        name="out_ffn",
    )(a, x, w_o, b_o, mod, ln, w_gu, w_down, final_g)


def _half_split(w, n_comp, comp_dim):
    lead = w.shape[:-1]
    heads = w.shape[-1] // HEAD_W
    w = w.reshape(lead + (heads, n_comp, comp_dim // 2, 2))
    return jnp.moveaxis(w, -1, -3).reshape(lead + (heads * HEAD_W,))


def _rope_lane_tables(seq, head_dim):
    axis_dim = head_dim // 2
    rows = jnp.arange(seq, dtype=jnp.int32) // GRID_W
    cols = jnp.arange(seq, dtype=jnp.int32) % GRID_W
    inv_freq = ROPE_THETA ** (-jnp.arange(0, axis_dim, 2, dtype=F32) / axis_dim)
    ang = jnp.concatenate([rows[:, None].astype(F32) * inv_freq,
                           cols[:, None].astype(F32) * inv_freq], axis=-1)
    reps = (HEAD_W // 2) // (head_dim // 2)
    cos = jnp.tile(jnp.cos(ang), (1, 2 * reps))
    sin = jnp.tile(jnp.sin(ang), (1, reps))
    return cos, jnp.concatenate([-sin, sin], axis=-1)


def _dft_tables(n, scale, n_rows=None, negate_sin=True):
    n_rows = n if n_rows is None else n_rows
    r = 1
    while r * r < n:
        r *= 2
    q = n // r
    k = jnp.arange(n_rows, dtype=jnp.int32)[:, None]
    a_idx = (k * jnp.arange(q, dtype=jnp.int32)[None, :] * r) % n
    b_idx = (k * jnp.arange(r, dtype=jnp.int32)[None, :]) % n
    w = 2.0 * math.pi / n
    ca, sa = jnp.cos(a_idx.astype(F32) * w), jnp.sin(a_idx.astype(F32) * w)
    cb, sb = jnp.cos(b_idx.astype(F32) * w) * scale, jnp.sin(b_idx.astype(F32) * w) * scale
    cos = ca[:, :, None] * cb[:, None, :] - sa[:, :, None] * sb[:, None, :]
    sin = sa[:, :, None] * cb[:, None, :] + ca[:, :, None] * sb[:, None, :]
    sin = -sin if negate_sin else sin
    return cos.reshape(n_rows, n).astype(BF16), sin.reshape(n_rows, n).astype(BF16)


def _channel_dft_weight(gd):
    idx = (np.arange(gd)[:, None] * np.arange(gd)[None, :]) % gd
    ang = 2.0 * np.pi * idx / gd
    w = np.concatenate([np.cos(ang), np.sin(ang)], axis=1) / math.sqrt(gd)
    return jnp.asarray(w, dtype=BF16)


def kernel(x, c, ctx, c_ctx, mod_w, mod_b, ln_mix, ln_ffn, ffn_w_gu, ffn_w_down, a_w_qkv, a_lam, a_subln, a_w_o, b_w_qkv, b_q_norm, b_k_norm, b_w_o, c_w_o, c_b_o, final_norm):
    batch, seq, d = x.shape
    n_ctx = ctx.shape[1]
    depth = mod_w.shape[0]
    assert seq % FFN_ROW_TILE == 0 and n_ctx == ROW_TILE and batch < MOD_ROWS

    cs = jnp.concatenate([c, c_ctx[None, :], jnp.zeros((MOD_ROWS - batch - 1, d), F32)], axis=0)
    mod_all = _modulation(cs, mod_w, mod_b).reshape(depth, MOD_ROWS, 6, d)
    lat = dict(mod_row=None, row_tile=BIG_ROW_TILE)
    con = dict(mod_row=batch, row_tile=n_ctx)

    rope_a = _rope_lane_tables(seq, DIFF_HEAD_DIM)
    rope_b = _rope_lane_tables(seq, GQA_HEAD_DIM)
    rope_ctx = (jnp.ones((n_ctx, HEAD_W), F32), jnp.zeros((n_ctx, HEAD_W), F32))
    ones_head = jnp.ones((1, HEAD_W), F32)
    no_bias = jnp.zeros((1, d), F32)

    w_gu_all, w_down_all = ffn_w_gu.astype(BF16), ffn_w_down.astype(BF16)
    x_lat, x_ctx = x, ctx
    for i in range(depth):
        last = i == depth - 1
        kind, j = i % N_MIXERS, i // N_MIXERS
        mod = mod_all[i]
        ln1 = ln_mix[i][None, :]

        if kind == 2:
            w_chan = _channel_dft_weight(d // FNET_GROUPS)
            mixed = _position_dft(*_channel_dft(x_lat, mod, None, ln1, w_chan, BIG_ROW_TILE))
            if not last:
                mixed_ctx = _position_dft_small(*_channel_dft(x_ctx, mod, batch, ln1, w_chan, n_ctx))
            w_o, b_o = c_w_o[j], c_b_o[j][None, :]
        else:
            if kind == 0:
                n_q = n_k = DIFF_HEADS * HEAD_W
                split, rope, w = (2, DIFF_HEAD_DIM), rope_a, a_w_qkv[j]
                proj = dict(qn=ones_head, kn=ones_head, n_q=n_q, n_k=n_k, head_norm=False,
                            q_scale=DIFF_HEAD_DIM ** -0.5 * LOG2E)
                attn = dict(lam=a_lam[j], subln=a_subln[j][None, :], diff=True, lambda_init=_lambda_init(i),
                            n_heads=DIFF_HEADS, heads_per_step=DIFF_HEADS_PER_STEP,
                            kv_heads_per_step=DIFF_HEADS_PER_STEP)
                w_o = a_w_o[j]
            else:
                n_q, n_k = GQA_HEADS * HEAD_W, GQA_KV_HEADS * HEAD_W
                split, rope, w = (1, GQA_HEAD_DIM), rope_b, b_w_qkv[j]
                proj = dict(qn=_half_split(b_q_norm[j], *split)[None, :], kn=_half_split(b_k_norm[j], *split)[None, :],
                            n_q=n_q, n_k=n_k, head_norm=True, q_scale=GQA_HEAD_DIM ** -0.5 * LOG2E)
                attn = dict(lam=jnp.zeros((4, DIFF_HEAD_DIM), F32), subln=ones_head, diff=False, lambda_init=0.0,
                            n_heads=GQA_HEADS, heads_per_step=GQA_HEADS, kv_heads_per_step=GQA_KV_HEADS)
                w_o = b_w_o[j]
            w_qk = _half_split(w[:, :n_q + n_k].astype(BF16), *split)
            w_vt = w[:, n_q + n_k:].T.astype(BF16)
            lat_proj = dict(lat, row_tile=ROW_TILE) if proj["head_norm"] else lat
            q, k, vt = _qkv_project(x_lat, mod, ln=ln1, w_qk=w_qk, w_vt=w_vt, cosf=rope[0], sinf=rope[1],
                                    **lat_proj, **proj)
            q_ctx, k_ctx, vt_ctx = _qkv_project(x_ctx, mod, ln=ln1, w_qk=w_qk, w_vt=w_vt, cosf=rope_ctx[0],
                                                sinf=rope_ctx[1], **con, **proj)
            mixed, mixed_ctx = _attention(q, k, vt, q_ctx, k_ctx, vt_ctx, need_ctx=not last, **attn)
            b_o = no_bias

        ffn = dict(w_o=w_o.astype(BF16), b_o=b_o, ln=ln_ffn[i][None, :], w_gu=w_gu_all, w_down=w_down_all,
                   layer=i, final_g=final_norm[None, :], final_norm=last)
        if not last:
            x_ctx = _out_ffn(mixed_ctx, x_ctx, mod, **con, **ffn)
        x_lat = _out_ffn(mixed, x_lat, mod, **dict(lat, row_tile=FFN_ROW_TILE), **ffn)
    return x_lat
```

```python
import functools
import math

import numpy as np
import jax
import jax.numpy as jnp
from jax import lax
from jax.experimental import pallas as pl
from jax.experimental.pallas import tpu as pltpu

GRID_W = 64
ROPE_THETA = 10000.0
NORM_EPS = 1e-6
DIFF_HEADS = 8
DIFF_HEAD_DIM = 64
GQA_HEADS = 8
GQA_KV_HEADS = 2
GQA_HEAD_DIM = 128
FNET_GROUPS = 4
N_MIXERS = 3

LANES = 128
SUBLANES = 8
VMEM_LIMIT = 56 * 1024 * 1024

ROW_TILE = 256
BIG_ROW_TILE = 1024
FFN_ROW_TILE = 1024
HEAD_W = 128
KEY_CHUNK = 256
SCORE_LEAD = 4
FF_CHUNK = 256
ONES_ROWS = 16
MOD_ROWS = 16
MOD_COL_TILE = 1536
DIFF_HEADS_PER_STEP = 4
LOG2E = math.log2(math.e)

BF16 = jnp.bfloat16
F32 = jnp.float32

SH1, SC1, G1, SH2, SC2, G2 = range(6)


def _lambda_init(layer_idx):
    return 0.8 - 0.6 * float(np.exp(-0.3 * layer_idx))


def _params(n_grid):
    return pltpu.CompilerParams(
        dimension_semantics=("arbitrary",) * n_grid, vmem_limit_bytes=VMEM_LIMIT)


def _resident(shape):
    nd = len(shape)
    return pl.BlockSpec(shape, lambda *_: (0,) * nd, pipeline_mode=pl.Buffered(1))


def _dot(a, b):
    return jnp.dot(a, b, preferred_element_type=F32)


def _dot_nt(a, b):
    return lax.dot_general(a, b, (((1,), (1,)), ((), ())), preferred_element_type=F32)


def _rms(x, g):
    return x * lax.rsqrt(jnp.mean(x * x, axis=-1, keepdims=True) + NORM_EPS) * g


def _silu(x):
    return x / (1.0 + jnp.exp(-x))


def _mod_kernel(cs_ref, w_ref, b_ref, o_ref):
    s = _silu(cs_ref[...]).astype(BF16)
    o_ref[0] = _dot(s, w_ref[0].astype(BF16)) + b_ref[0]


def _modulation(cs, mod_w, mod_b):
    depth, d, n = mod_w.shape
    tn = MOD_COL_TILE
    return pl.pallas_call(
        _mod_kernel,
        grid=(depth, n // tn),
        in_specs=[
            pl.BlockSpec((MOD_ROWS, d), lambda i, j: (0, 0)),
            pl.BlockSpec((1, d, tn), lambda i, j: (i, 0, j)),
            pl.BlockSpec((1, 1, tn), lambda i, j: (i, 0, j)),
        ],
        out_specs=pl.BlockSpec((1, MOD_ROWS, tn), lambda i, j: (i, 0, j)),
        out_shape=jax.ShapeDtypeStruct((depth, MOD_ROWS, n), F32),
        compiler_params=_params(2),
        name="modulation",
    )(cs, mod_w, mod_b.reshape(depth, 1, n))


def _mod_spec(d, mod_row):
    if mod_row is None:
        return pl.BlockSpec((1, 6, d), lambda b, j: (b, 0, 0))
    return pl.BlockSpec((1, 6, d), lambda b, j: (mod_row, 0, 0))


def _norm_mod(x, ln, mod, shift, scale):
    y = _rms(x, ln)
    return (y * (1.0 + mod[scale:scale + 1]) + mod[shift:shift + 1]).astype(BF16)


def _qkv_kernel(x_ref, mod_ref, ln_ref, w_ref, wvt_ref, cos_ref, sin_ref, qn_ref, kn_ref,
                q_ref, k_ref, vt_ref, *, n_q, n_k, head_norm, q_scale):
    h = _norm_mod(x_ref[0], ln_ref[...], mod_ref[0], SH1, SC1)
    cos, sin = cos_ref[...], sin_ref[...]
    for head in range((n_q + n_k) // HEAD_W):
        is_q = head < n_q // HEAD_W
        if head % 2 == 0:
            pair = _dot(h, w_ref[:, head * HEAD_W:(head + 2) * HEAD_W])
        t = pair[:, (head % 2) * HEAD_W:(head % 2 + 1) * HEAD_W]
        if head_norm:
            t = _rms(t, qn_ref[...] if is_q else kn_ref[...])
        t = t * cos + pltpu.roll(t, HEAD_W // 2, 1) * sin
        if is_q:
            q_ref[0, :, head * HEAD_W:(head + 1) * HEAD_W] = (t * q_scale).astype(BF16)
        else:
            c0 = head * HEAD_W - n_q
            k_ref[0, :, c0:c0 + HEAD_W] = t.astype(BF16)
    vt_ref[0] = _dot_nt(wvt_ref[...], h).astype(BF16)


def _qkv_project(x, mod, mod_row, ln, w_qk, w_vt, cosf, sinf, qn, kn, *, row_tile, n_q, n_k, head_norm,
                 q_scale):
    batch, rows, d = x.shape
    n_v = w_vt.shape[0]
    row = lambda b, j: (b, j, 0)
    kern = functools.partial(_qkv_kernel, n_q=n_q, n_k=n_k, head_norm=head_norm, q_scale=q_scale)
    return pl.pallas_call(
        kern,
        grid=(batch, rows // row_tile),
        in_specs=[
            pl.BlockSpec((1, row_tile, d), row),
            _mod_spec(d, mod_row),
            _resident((1, d)),
            _resident((d, n_q + n_k)),
            _resident((n_v, d)),
            pl.BlockSpec((row_tile, HEAD_W), lambda b, j: (j, 0)),
            pl.BlockSpec((row_tile, HEAD_W), lambda b, j: (j, 0)),
            _resident((1, HEAD_W)),
            _resident((1, HEAD_W)),
        ],
        out_specs=[
            pl.BlockSpec((1, row_tile, n_q), row),
            pl.BlockSpec((1, row_tile, n_k), row),
            pl.BlockSpec((1, n_v, row_tile), lambda b, j: (b, 0, j)),
        ],
        out_shape=[
            jax.ShapeDtypeStruct((batch, rows, n_q), BF16),
            jax.ShapeDtypeStruct((batch, rows, n_k), BF16),
            jax.ShapeDtypeStruct((batch, n_v, rows), BF16),
        ],
        compiler_params=_params(2),
        name="qkv_project",
    )(x, mod, ln, w_qk, w_vt, cosf, sinf, qn, kn)


def _chan_dft_kernel(x_ref, mod_ref, ln_ref, w_ref, oc_ref, os_ref):
    h = _norm_mod(x_ref[0], ln_ref[...], mod_ref[0], SH1, SC1)
    d = h.shape[1]
    gd = w_ref.shape[0]
    for g in range(d // gd):
        y = _dot(h[:, g * gd:(g + 1) * gd], w_ref[...])
        oc_ref[0, :, g * gd:(g + 1) * gd] = y[:, :gd].astype(BF16)
        os_ref[0, :, g * gd:(g + 1) * gd] = y[:, gd:].astype(BF16)


def _channel_dft(x, mod, mod_row, ln, w, row_tile):
    batch, rows, d = x.shape
    row = lambda b, j: (b, j, 0)
    return pl.pallas_call(
        _chan_dft_kernel,
        grid=(batch, rows // row_tile),
        in_specs=[
            pl.BlockSpec((1, row_tile, d), row),
            _mod_spec(d, mod_row),
            _resident((1, d)),
            _resident(w.shape),
        ],
        out_specs=[pl.BlockSpec((1, row_tile, d), row)] * 2,
        out_shape=[jax.ShapeDtypeStruct((batch, rows, d), BF16)] * 2,
        compiler_params=_params(2),
        name="channel_dft",
    )(x, mod, ln, w)


def _kv_chunks(klat_ref, kctx_ref, vtlat_ref, vtctx_ref, n_lat_chunks):
    def pick(lat_ref, ctx_ref, c):
        return (lat_ref, c) if c < n_lat_chunks else (ctx_ref, c - n_lat_chunks)

    def key_chunk(c, kvh):
        ref, cc = pick(klat_ref, kctx_ref, c)
        return ref[0, cc * KEY_CHUNK:(cc + 1) * KEY_CHUNK, kvh * HEAD_W:(kvh + 1) * HEAD_W]

    def val_chunk(c, kvh):
        ref, cc = pick(vtlat_ref, vtctx_ref, c)
        return ref[0, kvh * HEAD_W:(kvh + 1) * HEAD_W, cc * KEY_CHUNK:(cc + 1) * KEY_CHUNK]

    return key_chunk, val_chunk


def _score_chunk(key_chunk, s_scr, q, kvh, buf, c, n_chunks, m8):
    tq = q.shape[0]
    s = _dot_nt(key_chunk(c, kvh), q)
    lo = (buf * n_chunks + c) * KEY_CHUNK
    s_scr[lo:lo + KEY_CHUNK, :] = s
    return jnp.maximum(m8, jnp.max(s.reshape(KEY_CHUNK // SUBLANES, SUBLANES, tq), axis=0))


def _softmax_pv_streams(score_list, weigh_list, offset, key_chunk, val_chunk, s_scr, m8_carried, n_chunks,
                        on_done):
    tq = score_list[0][0].shape[0]
    ones_rows = jnp.ones((ONES_ROWS, KEY_CHUNK), BF16)

    def weigh(kvh, buf, c, mx, acc):
        lo = (buf * n_chunks + c) * KEY_CHUNK
        e = jnp.exp2(s_scr[lo:lo + KEY_CHUNK, :] - mx)
        v_ones = jnp.concatenate([val_chunk(c, kvh), ones_rows], axis=0)
        return acc + _dot(v_ones, e.astype(BF16))

    m8 = [None] * len(score_list)
    mx = acc = None
    n_steps = max(len(score_list) * n_chunks, offset + len(weigh_list) * n_chunks)
    for g in range(n_steps):
        if g < len(score_list) * n_chunks:
            j, c = divmod(g, n_chunks)
            q, kvh, buf = score_list[j]
            if c == 0:
                m8[j] = jnp.full((SUBLANES, tq), -jnp.inf, F32)
            m8[j] = _score_chunk(key_chunk, s_scr, q, kvh, buf, c, n_chunks, m8[j])
        ge = g - offset
        if 0 <= ge < len(weigh_list) * n_chunks:
            p, c = divmod(ge, n_chunks)
            kvh, buf, max_src = weigh_list[p]
            if c == 0:
                src = m8_carried if max_src is None else m8[max_src]
                mx = jnp.max(src, axis=0, keepdims=True)
                acc = jnp.zeros((HEAD_W + ONES_ROWS, tq), F32)
            acc = weigh(kvh, buf, c, mx, acc)
            if c == n_chunks - 1:
                on_done(p, acc[:HEAD_W] * (1.0 / acc[HEAD_W:HEAD_W + 1]))
    return m8


def _head_problems(q_ref, diff, heads_per_step, kv_heads_per_step, n_heads=None):
    problems = []
    for hh in range(heads_per_step if n_heads is None else n_heads):
        q = q_ref[0, :, hh * HEAD_W:(hh + 1) * HEAD_W]
        kvh = hh * kv_heads_per_step // heads_per_step
        if diff:
            lane = lax.broadcasted_iota(jnp.int32, q.shape, 1)
            first = (lane % (HEAD_W // 2)) < (HEAD_W // 4)
            zero = jnp.zeros_like(q)
            problems += [(jnp.where(first, q, zero), kvh), (jnp.where(first, zero, q), kvh)]
        else:
            problems.append((q, kvh))
    return problems


def _head_writer(o_ref, lam_ref, subln_ref, diff, lambda_init):
    first_comp = {}

    def on_done(p, o_t):
        if not diff:
            o_ref[0, :, p * HEAD_W:(p + 1) * HEAD_W] = o_t.T.astype(BF16)
            return
        hh = p // 2
        if p % 2 == 0:
            first_comp[hh] = o_t
            return
        lv = lam_ref[...]
        lam = (jnp.exp(jnp.sum(lv[0:1] * lv[1:2], axis=-1, keepdims=True))
               - jnp.exp(jnp.sum(lv[2:3] * lv[3:4], axis=-1, keepdims=True)) + lambda_init)
        o = (first_comp.pop(hh) - lam * o_t).T
        o = _rms(o, subln_ref[...]) * (1.0 - lambda_init)
        o_ref[0, :, hh * HEAD_W:(hh + 1) * HEAD_W] = o.astype(BF16)

    return on_done


def _attn_latent_kernel(lam_ref, subln_ref, q_ref, qn_ref, klat_ref, kctx_ref, vtlat_ref, vtctx_ref, o_ref,
                        s_scr, m8_scr, *, diff, lambda_init, heads_per_step, kv_heads_per_step):
    n_lat_chunks = klat_ref.shape[1] // KEY_CHUNK
    n_chunks = n_lat_chunks + kctx_ref.shape[1] // KEY_CHUNK
    key_chunk, val_chunk = _kv_chunks(klat_ref, kctx_ref, vtlat_ref, vtctx_ref, n_lat_chunks)
    cur = _head_problems(q_ref, diff, heads_per_step, kv_heads_per_step)
    nxt_q, nxt_kvh = _head_problems(qn_ref, diff, heads_per_step, kv_heads_per_step, n_heads=1)[0]
    n_prob = len(cur)
    assert n_prob % 2 == 0

    @pl.when(pl.program_id(2) == 0)
    def _():
        q, kvh = cur[0]
        m8 = jnp.full((SUBLANES, q.shape[0]), -jnp.inf, F32)
        for c in range(n_chunks):
            m8 = _score_chunk(key_chunk, s_scr, q, kvh, 0, c, n_chunks, m8)
        m8_scr[...] = m8

    score_list = [(q, kvh, p % 2) for p, (q, kvh) in enumerate(cur) if p > 0] + [(nxt_q, nxt_kvh, 0)]
    weigh_list = [(kvh, p % 2, None if p == 0 else p - 1) for p, (_, kvh) in enumerate(cur)]
    m8 = _softmax_pv_streams(
        score_list, weigh_list, min(SCORE_LEAD, n_chunks - 1), key_chunk, val_chunk, s_scr, m8_scr[...],
        n_chunks, _head_writer(o_ref, lam_ref, subln_ref, diff, lambda_init))
    m8_scr[...] = m8[-1]


def _attn_ctx_kernel(lam_ref, subln_ref, q_ref, k_ref, vt_ref, o_ref, s_scr,
                     *, diff, lambda_init, heads_per_step, kv_heads_per_step):
    n_chunks = k_ref.shape[1] // KEY_CHUNK
    key_chunk, val_chunk = _kv_chunks(None, k_ref, None, vt_ref, 0)
    cur = _head_problems(q_ref, diff, heads_per_step, kv_heads_per_step)
    score_list = [(q, kvh, p % 2) for p, (q, kvh) in enumerate(cur)]
    weigh_list = [(kvh, p % 2, p) for p, (_, kvh) in enumerate(cur)]
    _softmax_pv_streams(
        score_list, weigh_list, n_chunks + min(SCORE_LEAD, n_chunks - 1), key_chunk, val_chunk, s_scr, None,
        n_chunks, _head_writer(o_ref, lam_ref, subln_ref, diff, lambda_init))


def _attention(q, k, vt, q_ctx, k_ctx, vt_ctx, lam, subln, *, diff, lambda_init, n_heads, heads_per_step,
               kv_heads_per_step, need_ctx):
    batch, n_lat, _ = q.shape
    n_ctx = k_ctx.shape[1]
    qw = heads_per_step * HEAD_W
    kw = kv_heads_per_step * HEAD_W
    n_groups = n_heads // heads_per_step
    n_lat_tiles = n_lat // ROW_TILE
    static = dict(diff=diff, lambda_init=lambda_init, heads_per_step=heads_per_step,
                  kv_heads_per_step=kv_heads_per_step)
    name = "diff_attention" if diff else "gqa_attention"
    o_lat = pl.pallas_call(
        functools.partial(_attn_latent_kernel, **static),
        grid=(batch, n_groups, n_lat_tiles),
        in_specs=[
            _resident(lam.shape),
            _resident(subln.shape),
            pl.BlockSpec((1, ROW_TILE, qw), lambda b, h, i: (b, i, h)),
            pl.BlockSpec((1, ROW_TILE, qw), lambda b, h, i: (b, jnp.minimum(i + 1, n_lat_tiles - 1), h)),
            pl.BlockSpec((1, n_lat, kw), lambda b, h, i: (b, 0, h)),
            pl.BlockSpec((1, n_ctx, kw), lambda b, h, i: (b, 0, h)),
            pl.BlockSpec((1, kw, n_lat), lambda b, h, i: (b, h, 0)),
            pl.BlockSpec((1, kw, n_ctx), lambda b, h, i: (b, h, 0)),
        ],
        out_specs=pl.BlockSpec((1, ROW_TILE, qw), lambda b, h, i: (b, i, h)),
        out_shape=jax.ShapeDtypeStruct((batch, n_lat, n_heads * HEAD_W), BF16),
        scratch_shapes=[pltpu.VMEM((2 * (n_lat + n_ctx), ROW_TILE), F32), pltpu.VMEM((SUBLANES, ROW_TILE), F32)],
        compiler_params=_params(3),
        name=name,
    )(lam, subln, q, q, k, k_ctx, vt, vt_ctx)
    if not need_ctx:
        return o_lat, None
    o_ctx = pl.pallas_call(
        functools.partial(_attn_ctx_kernel, **static),
        grid=(batch, n_groups),
        in_specs=[
            _resident(lam.shape),
            _resident(subln.shape),
            pl.BlockSpec((1, n_ctx, qw), lambda b, h: (b, 0, h)),
            pl.BlockSpec((1, n_ctx, kw), lambda b, h: (b, 0, h)),
            pl.BlockSpec((1, kw, n_ctx), lambda b, h: (b, h, 0)),
        ],
        out_specs=pl.BlockSpec((1, n_ctx, qw), lambda b, h: (b, 0, h)),
        out_shape=jax.ShapeDtypeStruct((batch, n_ctx, n_heads * HEAD_W), BF16),
        scratch_shapes=[pltpu.VMEM((2 * n_ctx, n_ctx), F32)],
        compiler_params=_params(2),
        name=name + "_ctx",
    )(lam, subln, q_ctx, k_ctx, vt_ctx)
    return o_lat, o_ctx


DFT_COL_TILE = 1024
DFT_EXTRA_ROWS = 16


def _dft_mirror_kernel(cl_ref, sl_ref, ch_ref, flip_ref, xc_ref, xs_ref, o_ref, carry_scr, *, n_steps):
    s = pl.program_id(2)
    t = n_steps - 1 - s

    @pl.when(s == 0)
    def _():
        carry_scr[...] = _dot(ch_ref[...], xc_ref[0])

    a = _dot(cl_ref[...], xc_ref[0])
    b = _dot(sl_ref[...], xs_ref[0])
    total = a + b
    row = lax.broadcasted_iota(jnp.int32, total.shape, 0)
    mirror = _dot(flip_ref[...], total.astype(BF16))
    mirror = jnp.where(row == 0, carry_scr[0:1, :], mirror)
    carry_scr[0:1, :] = total[0:1]
    o_ref[0, pl.ds(pl.multiple_of(t * ROW_TILE, ROW_TILE), ROW_TILE), :] = (a - b).astype(BF16)
    lo = pl.multiple_of((2 * n_steps - 1 - t) * ROW_TILE, ROW_TILE)
    o_ref[0, pl.ds(lo, ROW_TILE), :] = mirror.astype(BF16)


def _position_dft(xc, xs):
    batch, n, d = xc.shape
    half = n // 2
    n_steps = half // ROW_TILE
    cl, sl = _dft_tables(n, n ** -0.5, n_rows=half, negate_sin=False)
    alt = jnp.where(jnp.arange(n, dtype=jnp.int32) % 2 == 0, n ** -0.5, -(n ** -0.5)).astype(BF16)
    c_half = jnp.zeros((DFT_EXTRA_ROWS, n), BF16).at[0].set(alt)
    flip = np.zeros((ROW_TILE, ROW_TILE), np.float32)
    flip[np.arange(1, ROW_TILE), ROW_TILE - np.arange(1, ROW_TILE)] = 1.0
    tile_rows = lambda b, c, s: (n_steps - 1 - s, 0)
    return pl.pallas_call(
        functools.partial(_dft_mirror_kernel, n_steps=n_steps),
        grid=(batch, d // DFT_COL_TILE, n_steps),
        in_specs=[
            pl.BlockSpec((ROW_TILE, n), tile_rows),
            pl.BlockSpec((ROW_TILE, n), tile_rows),
            _resident((DFT_EXTRA_ROWS, n)),
            _resident((ROW_TILE, ROW_TILE)),
            pl.BlockSpec((1, n, DFT_COL_TILE), lambda b, c, s: (b, 0, c), pipeline_mode=pl.Buffered(1)),
            pl.BlockSpec((1, n, DFT_COL_TILE), lambda b, c, s: (b, 0, c), pipeline_mode=pl.Buffered(1)),
        ],
        out_specs=pl.BlockSpec((1, n, DFT_COL_TILE), lambda b, c, s: (b, 0, c)),
        out_shape=jax.ShapeDtypeStruct((batch, n, d), BF16),
        scratch_shapes=[pltpu.VMEM((DFT_EXTRA_ROWS, DFT_COL_TILE), F32)],
        compiler_params=_params(3),
        name="position_dft",
    )(cl, sl, c_half, jnp.asarray(flip, dtype=BF16), xc, xs)


def _dft_small_kernel(c_ref, s_ref, xc_ref, xs_ref, o_ref):
    o_ref[0] = (_dot(c_ref[...], xc_ref[0]) + _dot(s_ref[...], xs_ref[0])).astype(BF16)


def _position_dft_small(xc, xs):
    batch, n, d = xc.shape
    cc, sc = _dft_tables(n, n ** -0.5)
    whole = lambda b: (b, 0, 0)
    return pl.pallas_call(
        _dft_small_kernel,
        grid=(batch,),
        in_specs=[
            _resident((n, n)),
            _resident((n, n)),
            pl.BlockSpec((1, n, d), whole),
            pl.BlockSpec((1, n, d), whole),
        ],
        out_specs=pl.BlockSpec((1, n, d), whole),
        out_shape=jax.ShapeDtypeStruct((batch, n, d), BF16),
        compiler_params=_params(1),
        name="position_dft_ctx",
    )(cc, sc, xc, xs)


def _out_ffn_kernel(a_ref, x_ref, wo_ref, bo_ref, mod_ref, ln_ref, wgu_ref, wd_ref, fn_ref, o_ref, a_scr,
                    *, d_ff, final_norm):
    mod = mod_ref[0]
    x = x_ref[0] + mod[G1:G1 + 1] * (_dot(a_ref[0], wo_ref[...]) + bo_ref[...])
    h = _norm_mod(x, ln_ref[...], mod, SH2, SC2)
    for lo in range(0, d_ff, FF_CHUNK):
        hi = min(lo + FF_CHUNK, d_ff)
        g = _dot(h, wgu_ref[0, :, lo:hi])
        u = _dot(h, wgu_ref[0, :, d_ff + lo:d_ff + hi])
        a_scr[:, lo:hi] = (_silu(g) * u).astype(BF16)
    y = x + mod[G2:G2 + 1] * _dot(a_scr[...], wd_ref[0])
    if final_norm:
        y = _rms(y, fn_ref[...])
    o_ref[0] = y


def _out_ffn(a, x, mod, mod_row, w_o, b_o, ln, w_gu, w_down, layer, final_g, *, row_tile, final_norm):
    batch, rows, d = x.shape
    k = a.shape[2]
    d_ff = w_down.shape[1]
    row = lambda b, j: (b, j, 0)
    layer_block = lambda shape: pl.BlockSpec((1,) + shape, lambda b, j: (layer, 0, 0), pipeline_mode=pl.Buffered(1))
    return pl.pallas_call(
        functools.partial(_out_ffn_kernel, d_ff=d_ff, final_norm=final_norm),
        grid=(batch, rows // row_tile),
        in_specs=[
            pl.BlockSpec((1, row_tile, k), row),
            pl.BlockSpec((1, row_tile, d), row),
            _resident((k, d)),
            _resident((1, d)),
            _mod_spec(d, mod_row),
            _resident((1, d)),
            layer_block((d, 2 * d_ff)),
            layer_block((d_ff, d)),
            _resident((1, d)),
        ],
        out_specs=pl.BlockSpec((1, row_tile, d), row),
        out_shape=jax.ShapeDtypeStruct((batch, rows, d), F32),
        scratch_shapes=[pltpu.VMEM((row_tile, d_ff), BF16)],
        compiler_params=_params(2),
        name="out_ffn",
    )(a, x, w_o, b_o, mod, ln, w_gu, w_down, final_g)


def _half_split(w, n_comp, comp_dim):
    lead = w.shape[:-1]
    heads = w.shape[-1] // HEAD_W
    w = w.reshape(lead + (heads, n_comp, comp_dim // 2, 2))
    return jnp.moveaxis(w, -1, -3).reshape(lead + (heads * HEAD_W,))


def _rope_lane_tables(seq, head_dim):
    axis_dim = head_dim // 2
    rows = jnp.arange(seq, dtype=jnp.int32) // GRID_W
    cols = jnp.arange(seq, dtype=jnp.int32) % GRID_W
    inv_freq = ROPE_THETA ** (-jnp.arange(0, axis_dim, 2, dtype=F32) / axis_dim)
    ang = jnp.concatenate([rows[:, None].astype(F32) * inv_freq,
                           cols[:, None].astype(F32) * inv_freq], axis=-1)
    reps = (HEAD_W // 2) // (head_dim // 2)
    cos = jnp.tile(jnp.cos(ang), (1, 2 * reps))
    sin = jnp.tile(jnp.sin(ang), (1, reps))
    return cos, jnp.concatenate([-sin, sin], axis=-1)


def _dft_tables(n, scale, n_rows=None, negate_sin=True):
    n_rows = n if n_rows is None else n_rows
    r = 1
    while r * r < n:
        r *= 2
    q = n // r
    k = jnp.arange(n_rows, dtype=jnp.int32)[:, None]
    a_idx = (k * jnp.arange(q, dtype=jnp.int32)[None, :] * r) % n
    b_idx = (k * jnp.arange(r, dtype=jnp.int32)[None, :]) % n
    w = 2.0 * math.pi / n
    ca, sa = jnp.cos(a_idx.astype(F32) * w), jnp.sin(a_idx.astype(F32) * w)
    cb, sb = jnp.cos(b_idx.astype(F32) * w) * scale, jnp.sin(b_idx.astype(F32) * w) * scale
    cos = ca[:, :, None] * cb[:, None, :] - sa[:, :, None] * sb[:, None, :]
    sin = sa[:, :, None] * cb[:, None, :] + ca[:, :, None] * sb[:, None, :]
    sin = -sin if negate_sin else sin
    return cos.reshape(n_rows, n).astype(BF16), sin.reshape(n_rows, n).astype(BF16)


def _channel_dft_weight(gd):
    pos = jnp.arange(gd, dtype=jnp.int32)
    ang = ((pos[:, None] * pos[None, :]) % gd).astype(F32) * (2.0 * math.pi / gd)
    return (jnp.concatenate([jnp.cos(ang), jnp.sin(ang)], axis=1) * gd ** -0.5).astype(BF16)


def kernel(x, c, ctx, c_ctx, mod_w, mod_b, ln_mix, ln_ffn, ffn_w_gu, ffn_w_down, a_w_qkv, a_lam, a_subln, a_w_o, b_w_qkv, b_q_norm, b_k_norm, b_w_o, c_w_o, c_b_o, final_norm):
    batch, seq, d = x.shape
    n_ctx = ctx.shape[1]
    depth = mod_w.shape[0]
    assert seq % FFN_ROW_TILE == 0 and n_ctx == ROW_TILE and batch < MOD_ROWS

    cs = jnp.concatenate([c, c_ctx[None, :], jnp.zeros((MOD_ROWS - batch - 1, d), F32)], axis=0)
    mod_all = _modulation(cs, mod_w, mod_b).reshape(depth, MOD_ROWS, 6, d)
    lat = dict(mod_row=None, row_tile=BIG_ROW_TILE)
    con = dict(mod_row=batch, row_tile=n_ctx)

    rope_a = _rope_lane_tables(seq, DIFF_HEAD_DIM)
    rope_b = _rope_lane_tables(seq, GQA_HEAD_DIM)
    rope_ctx = (jnp.ones((n_ctx, HEAD_W), F32), jnp.zeros((n_ctx, HEAD_W), F32))
    ones_head = jnp.ones((1, HEAD_W), F32)
    no_bias = jnp.zeros((1, d), F32)

    w_gu_all, w_down_all = ffn_w_gu.astype(BF16), ffn_w_down.astype(BF16)
    x_lat, x_ctx = x, ctx
    for i in range(depth):
        last = i == depth - 1
        kind, j = i % N_MIXERS, i // N_MIXERS
        mod = mod_all[i]
        ln1 = ln_mix[i][None, :]

        if kind == 2:
            w_chan = _channel_dft_weight(d // FNET_GROUPS)
            mixed = _position_dft(*_channel_dft(x_lat, mod, None, ln1, w_chan, BIG_ROW_TILE))
            if not last:
                mixed_ctx = _position_dft_small(*_channel_dft(x_ctx, mod, batch, ln1, w_chan, n_ctx))
            w_o, b_o = c_w_o[j], c_b_o[j][None, :]
        else:
            if kind == 0:
                n_q = n_k = DIFF_HEADS * HEAD_W
                split, rope, w = (2, DIFF_HEAD_DIM), rope_a, a_w_qkv[j]
                proj = dict(qn=ones_head, kn=ones_head, n_q=n_q, n_k=n_k, head_norm=False,
                            q_scale=DIFF_HEAD_DIM ** -0.5 * LOG2E)
                attn = dict(lam=a_lam[j], subln=a_subln[j][None, :], diff=True, lambda_init=_lambda_init(i),
                            n_heads=DIFF_HEADS, heads_per_step=DIFF_HEADS_PER_STEP,
                            kv_heads_per_step=DIFF_HEADS_PER_STEP)
                w_o = a_w_o[j]
            else:
                n_q, n_k = GQA_HEADS * HEAD_W, GQA_KV_HEADS * HEAD_W
                split, rope, w = (1, GQA_HEAD_DIM), rope_b, b_w_qkv[j]
                proj = dict(qn=_half_split(b_q_norm[j], *split)[None, :], kn=_half_split(b_k_norm[j], *split)[None, :],
                            n_q=n_q, n_k=n_k, head_norm=True, q_scale=GQA_HEAD_DIM ** -0.5 * LOG2E)
                attn = dict(lam=jnp.zeros((4, DIFF_HEAD_DIM), F32), subln=ones_head, diff=False, lambda_init=0.0,
                            n_heads=GQA_HEADS, heads_per_step=GQA_HEADS, kv_heads_per_step=GQA_KV_HEADS)
                w_o = b_w_o[j]
            w_qk = _half_split(w[:, :n_q + n_k].astype(BF16), *split)
            w_vt = w[:, n_q + n_k:].T.astype(BF16)
            lat_proj = dict(lat, row_tile=ROW_TILE) if proj["head_norm"] else lat
            q, k, vt = _qkv_project(x_lat, mod, ln=ln1, w_qk=w_qk, w_vt=w_vt, cosf=rope[0], sinf=rope[1],
                                    **lat_proj, **proj)
            q_ctx, k_ctx, vt_ctx = _qkv_project(x_ctx, mod, ln=ln1, w_qk=w_qk, w_vt=w_vt, cosf=rope_ctx[0],
                                                sinf=rope_ctx[1], **con, **proj)
            mixed, mixed_ctx = _attention(q, k, vt, q_ctx, k_ctx, vt_ctx, need_ctx=not last, **attn)
            b_o = no_bias

        ffn = dict(w_o=w_o.astype(BF16), b_o=b_o, ln=ln_ffn[i][None, :], w_gu=w_gu_all, w_down=w_down_all,
                   layer=i, final_g=final_norm[None, :], final_norm=last)
        if not last:
            x_ctx = _out_ffn(mixed_ctx, x_ctx, mod, **con, **ffn)
        x_lat = _out_ffn(mixed, x_lat, mod, **dict(lat, row_tile=FFN_ROW_TILE), **ffn)
    return x_lat
```

```python
import functools
import math

import numpy as np
import jax
import jax.numpy as jnp
from jax import lax
from jax.experimental import pallas as pl
from jax.experimental.pallas import tpu as pltpu

GRID_W = 64
ROPE_THETA = 10000.0
NORM_EPS = 1e-6
DIFF_HEADS = 8
DIFF_HEAD_DIM = 64
GQA_HEADS = 8
GQA_KV_HEADS = 2
GQA_HEAD_DIM = 128
FNET_GROUPS = 4
N_MIXERS = 3

LANES = 128
SUBLANES = 8
VMEM_LIMIT = 56 * 1024 * 1024

ROW_TILE = 256
BIG_ROW_TILE = 1024
FFN_ROW_TILE = 1024
HEAD_W = 128
KEY_CHUNK = 256
LAT_KEY_CHUNK = 1024
SCORE_LEAD = 1
FF_CHUNK = 256
ONES_ROWS = 16
MOD_ROWS = 16
MOD_COL_TILE = 1536
DIFF_HEADS_PER_STEP = 4
LOG2E = math.log2(math.e)

BF16 = jnp.bfloat16
F32 = jnp.float32

SH1, SC1, G1, SH2, SC2, G2 = range(6)


def _lambda_init(layer_idx):
    return 0.8 - 0.6 * float(np.exp(-0.3 * layer_idx))


def _params(n_grid):
    return pltpu.CompilerParams(
        dimension_semantics=("arbitrary",) * n_grid, vmem_limit_bytes=VMEM_LIMIT)


def _resident(shape):
    nd = len(shape)
    return pl.BlockSpec(shape, lambda *_: (0,) * nd, pipeline_mode=pl.Buffered(1))


def _dot(a, b):
    return jnp.dot(a, b, preferred_element_type=F32)


def _dot_nt(a, b):
    return lax.dot_general(a, b, (((1,), (1,)), ((), ())), preferred_element_type=F32)


def _rms(x, g):
    return x * lax.rsqrt(jnp.mean(x * x, axis=-1, keepdims=True) + NORM_EPS) * g


def _silu(x):
    return x / (1.0 + jnp.exp(-x))


def _mod_kernel(cs_ref, w_ref, b_ref, o_ref):
    s = _silu(cs_ref[...]).astype(BF16)
    o_ref[0] = _dot(s, w_ref[0].astype(BF16)) + b_ref[0]


def _modulation(cs, mod_w, mod_b):
    depth, d, n = mod_w.shape
    tn = MOD_COL_TILE
    return pl.pallas_call(
        _mod_kernel,
        grid=(depth, n // tn),
        in_specs=[
            pl.BlockSpec((MOD_ROWS, d), lambda i, j: (0, 0)),
            pl.BlockSpec((1, d, tn), lambda i, j: (i, 0, j)),
            pl.BlockSpec((1, 1, tn), lambda i, j: (i, 0, j)),
        ],
        out_specs=pl.BlockSpec((1, MOD_ROWS, tn), lambda i, j: (i, 0, j)),
        out_shape=jax.ShapeDtypeStruct((depth, MOD_ROWS, n), F32),
        compiler_params=_params(2),
        name="modulation",
    )(cs, mod_w, mod_b.reshape(depth, 1, n))


def _mod_spec(d, mod_row):
    if mod_row is None:
        return pl.BlockSpec((1, 6, d), lambda b, j: (b, 0, 0))
    return pl.BlockSpec((1, 6, d), lambda b, j: (mod_row, 0, 0))


def _norm_mod(x, ln, mod, shift, scale):
    y = _rms(x, ln)
    return (y * (1.0 + mod[scale:scale + 1]) + mod[shift:shift + 1]).astype(BF16)


def _qkv_kernel(x_ref, mod_ref, ln_ref, w_ref, wvt_ref, cos_ref, sin_ref, qn_ref, kn_ref,
                q_ref, k_ref, vt_ref, *, n_q, n_k, head_norm, q_scale):
    h = _norm_mod(x_ref[0], ln_ref[...], mod_ref[0], SH1, SC1)
    cos, sin = cos_ref[...], sin_ref[...]
    for head in range((n_q + n_k) // HEAD_W):
        is_q = head < n_q // HEAD_W
        if head % 2 == 0:
            pair = _dot(h, w_ref[:, head * HEAD_W:(head + 2) * HEAD_W])
        t = pair[:, (head % 2) * HEAD_W:(head % 2 + 1) * HEAD_W]
        if head_norm:
            t = _rms(t, qn_ref[...] if is_q else kn_ref[...])
        t = t * cos + pltpu.roll(t, HEAD_W // 2, 1) * sin
        if is_q:
            q_ref[0, :, head * HEAD_W:(head + 1) * HEAD_W] = (t * q_scale).astype(BF16)
        else:
            c0 = head * HEAD_W - n_q
            k_ref[0, :, c0:c0 + HEAD_W] = t.astype(BF16)
    vt_ref[0] = _dot_nt(wvt_ref[...], h).astype(BF16)


def _qkv_project(x, mod, mod_row, ln, w_qk, w_vt, cosf, sinf, qn, kn, *, row_tile, n_q, n_k, head_norm,
                 q_scale):
    batch, rows, d = x.shape
    n_v = w_vt.shape[0]
    row = lambda b, j: (b, j, 0)
    kern = functools.partial(_qkv_kernel, n_q=n_q, n_k=n_k, head_norm=head_norm, q_scale=q_scale)
    return pl.pallas_call(
        kern,
        grid=(batch, rows // row_tile),
        in_specs=[
            pl.BlockSpec((1, row_tile, d), row),
            _mod_spec(d, mod_row),
            _resident((1, d)),
            _resident((d, n_q + n_k)),
            _resident((n_v, d)),
            pl.BlockSpec((row_tile, HEAD_W), lambda b, j: (j, 0)),
            pl.BlockSpec((row_tile, HEAD_W), lambda b, j: (j, 0)),
            _resident((1, HEAD_W)),
            _resident((1, HEAD_W)),
        ],
        out_specs=[
            pl.BlockSpec((1, row_tile, n_q), row),
            pl.BlockSpec((1, row_tile, n_k), row),
            pl.BlockSpec((1, n_v, row_tile), lambda b, j: (b, 0, j)),
        ],
        out_shape=[
            jax.ShapeDtypeStruct((batch, rows, n_q), BF16),
            jax.ShapeDtypeStruct((batch, rows, n_k), BF16),
            jax.ShapeDtypeStruct((batch, n_v, rows), BF16),
        ],
        compiler_params=_params(2),
        name="qkv_project",
    )(x, mod, ln, w_qk, w_vt, cosf, sinf, qn, kn)


def _chan_dft_kernel(x_ref, mod_ref, ln_ref, w_ref, oc_ref, os_ref):
    h = _norm_mod(x_ref[0], ln_ref[...], mod_ref[0], SH1, SC1)
    d = h.shape[1]
    gd = w_ref.shape[0]
    for g in range(d // gd):
        y = _dot(h[:, g * gd:(g + 1) * gd], w_ref[...])
        oc_ref[0, :, g * gd:(g + 1) * gd] = y[:, :gd].astype(BF16)
        os_ref[0, :, g * gd:(g + 1) * gd] = y[:, gd:].astype(BF16)


def _channel_dft(x, mod, mod_row, ln, w, row_tile):
    batch, rows, d = x.shape
    row = lambda b, j: (b, j, 0)
    return pl.pallas_call(
        _chan_dft_kernel,
        grid=(batch, rows // row_tile),
        in_specs=[
            pl.BlockSpec((1, row_tile, d), row),
            _mod_spec(d, mod_row),
            _resident((1, d)),
            _resident(w.shape),
        ],
        out_specs=[pl.BlockSpec((1, row_tile, d), row)] * 2,
        out_shape=[jax.ShapeDtypeStruct((batch, rows, d), BF16)] * 2,
        compiler_params=_params(2),
        name="channel_dft",
    )(x, mod, ln, w)


def _kv_chunks(klat_ref, kctx_ref, vtlat_ref, vtctx_ref):
    table, n_keys = [], 0
    for k_ref, vt_ref, size in ((klat_ref, vtlat_ref, LAT_KEY_CHUNK), (kctx_ref, vtctx_ref, KEY_CHUNK)):
        if k_ref is not None:
            for start in range(0, k_ref.shape[1], size):
                table.append((k_ref, vt_ref, start, size, n_keys))
                n_keys += size
    return table, n_keys


def _score_chunk(chunk, n_keys, s_scr, q, kvh, buf, m8):
    k_ref, _, start, size, row = chunk
    tq = q.shape[0]
    s = _dot_nt(k_ref[0, start:start + size, kvh * HEAD_W:(kvh + 1) * HEAD_W], q)
    s_scr[buf * n_keys + row:buf * n_keys + row + size, :] = s
    return jnp.maximum(m8, jnp.max(s.reshape(size // SUBLANES, SUBLANES, tq), axis=0))


def _softmax_pv_streams(score_list, weigh_list, offset, chunks, n_keys, s_scr, m8_carried, on_done):
    tq = score_list[0][0].shape[0]
    n_chunks = len(chunks)

    def weigh(kvh, buf, c, mx, acc):
        _, vt_ref, start, size, row = chunks[c]
        e = jnp.exp2(s_scr[buf * n_keys + row:buf * n_keys + row + size, :] - mx)
        v_ones = jnp.concatenate([vt_ref[0, kvh * HEAD_W:(kvh + 1) * HEAD_W, start:start + size],
                                  jnp.ones((ONES_ROWS, size), BF16)], axis=0)
        return acc + _dot(v_ones, e.astype(BF16))

    m8 = [None] * len(score_list)
    mx = acc = None
    n_steps = max(len(score_list) * n_chunks, offset + len(weigh_list) * n_chunks)
    for g in range(n_steps):
        if g < len(score_list) * n_chunks:
            j, c = divmod(g, n_chunks)
            q, kvh, buf = score_list[j]
            if c == 0:
                m8[j] = jnp.full((SUBLANES, tq), -jnp.inf, F32)
            m8[j] = _score_chunk(chunks[c], n_keys, s_scr, q, kvh, buf, m8[j])
        ge = g - offset
        if 0 <= ge < len(weigh_list) * n_chunks:
            p, c = divmod(ge, n_chunks)
            kvh, buf, max_src = weigh_list[p]
            if c == 0:
                src = m8_carried if max_src is None else m8[max_src]
                mx = jnp.max(src, axis=0, keepdims=True)
                acc = jnp.zeros((HEAD_W + ONES_ROWS, tq), F32)
            acc = weigh(kvh, buf, c, mx, acc)
            if c == n_chunks - 1:
                on_done(p, acc[:HEAD_W] * (1.0 / acc[HEAD_W:HEAD_W + 1]))
    return m8


def _head_problems(q_ref, diff, heads_per_step, kv_heads_per_step, n_heads=None):
    problems = []
    for hh in range(heads_per_step if n_heads is None else n_heads):
        q = q_ref[0, :, hh * HEAD_W:(hh + 1) * HEAD_W]
        kvh = hh * kv_heads_per_step // heads_per_step
        if diff:
            lane = lax.broadcasted_iota(jnp.int32, q.shape, 1)
            first = (lane % (HEAD_W // 2)) < (HEAD_W // 4)
            zero = jnp.zeros_like(q)
            problems += [(jnp.where(first, q, zero), kvh), (jnp.where(first, zero, q), kvh)]
        else:
            problems.append((q, kvh))
    return problems


def _head_writer(o_ref, lam_ref, subln_ref, diff, lambda_init):
    first_comp = {}

    def on_done(p, o_t):
        if not diff:
            o_ref[0, :, p * HEAD_W:(p + 1) * HEAD_W] = o_t.T.astype(BF16)
            return
        hh = p // 2
        if p % 2 == 0:
            first_comp[hh] = o_t
            return
        lv = lam_ref[...]
        lam = (jnp.exp(jnp.sum(lv[0:1] * lv[1:2], axis=-1, keepdims=True))
               - jnp.exp(jnp.sum(lv[2:3] * lv[3:4], axis=-1, keepdims=True)) + lambda_init)
        o = (first_comp.pop(hh) - lam * o_t).T
        o = _rms(o, subln_ref[...]) * (1.0 - lambda_init)
        o_ref[0, :, hh * HEAD_W:(hh + 1) * HEAD_W] = o.astype(BF16)

    return on_done


def _attn_latent_kernel(lam_ref, subln_ref, q_ref, qn_ref, klat_ref, kctx_ref, vtlat_ref, vtctx_ref, o_ref,
                        s_scr, m8_scr, *, diff, lambda_init, heads_per_step, kv_heads_per_step):
    chunks, n_keys = _kv_chunks(klat_ref, kctx_ref, vtlat_ref, vtctx_ref)
    cur = _head_problems(q_ref, diff, heads_per_step, kv_heads_per_step)
    nxt_q, nxt_kvh = _head_problems(qn_ref, diff, heads_per_step, kv_heads_per_step, n_heads=1)[0]
    n_prob = len(cur)
    assert n_prob % 2 == 0

    @pl.when(pl.program_id(2) == 0)
    def _():
        q, kvh = cur[0]
        m8 = jnp.full((SUBLANES, q.shape[0]), -jnp.inf, F32)
        for chunk in chunks:
            m8 = _score_chunk(chunk, n_keys, s_scr, q, kvh, 0, m8)
        m8_scr[...] = m8

    score_list = [(q, kvh, p % 2) for p, (q, kvh) in enumerate(cur) if p > 0] + [(nxt_q, nxt_kvh, 0)]
    weigh_list = [(kvh, p % 2, None if p == 0 else p - 1) for p, (_, kvh) in enumerate(cur)]
    m8 = _softmax_pv_streams(
        score_list, weigh_list, min(SCORE_LEAD, len(chunks) - 1), chunks, n_keys, s_scr, m8_scr[...],
        _head_writer(o_ref, lam_ref, subln_ref, diff, lambda_init))
    m8_scr[...] = m8[-1]


def _attn_ctx_kernel(lam_ref, subln_ref, q_ref, k_ref, vt_ref, o_ref, s_scr,
                     *, diff, lambda_init, heads_per_step, kv_heads_per_step):
    chunks, n_keys = _kv_chunks(None, k_ref, None, vt_ref)
    n_chunks = len(chunks)
    cur = _head_problems(q_ref, diff, heads_per_step, kv_heads_per_step)
    score_list = [(q, kvh, p % 2) for p, (q, kvh) in enumerate(cur)]
    weigh_list = [(kvh, p % 2, p) for p, (_, kvh) in enumerate(cur)]
    _softmax_pv_streams(
        score_list, weigh_list, n_chunks + min(SCORE_LEAD, n_chunks - 1), chunks, n_keys, s_scr, None,
        _head_writer(o_ref, lam_ref, subln_ref, diff, lambda_init))


def _attention(q, k, vt, q_ctx, k_ctx, vt_ctx, lam, subln, *, diff, lambda_init, n_heads, heads_per_step,
               kv_heads_per_step, need_ctx):
    batch, n_lat, _ = q.shape
    n_ctx = k_ctx.shape[1]
    qw = heads_per_step * HEAD_W
    kw = kv_heads_per_step * HEAD_W
    n_groups = n_heads // heads_per_step
    n_lat_tiles = n_lat // ROW_TILE
    static = dict(diff=diff, lambda_init=lambda_init, heads_per_step=heads_per_step,
                  kv_heads_per_step=kv_heads_per_step)
    name = "diff_attention" if diff else "gqa_attention"
    o_lat = pl.pallas_call(
        functools.partial(_attn_latent_kernel, **static),
        grid=(batch, n_groups, n_lat_tiles),
        in_specs=[
            _resident(lam.shape),
            _resident(subln.shape),
            pl.BlockSpec((1, ROW_TILE, qw), lambda b, h, i: (b, i, h)),
            pl.BlockSpec((1, ROW_TILE, qw), lambda b, h, i: (b, jnp.minimum(i + 1, n_lat_tiles - 1), h)),
            pl.BlockSpec((1, n_lat, kw), lambda b, h, i: (b, 0, h)),
            pl.BlockSpec((1, n_ctx, kw), lambda b, h, i: (b, 0, h)),
            pl.BlockSpec((1, kw, n_lat), lambda b, h, i: (b, h, 0)),
            pl.BlockSpec((1, kw, n_ctx), lambda b, h, i: (b, h, 0)),
        ],
        out_specs=pl.BlockSpec((1, ROW_TILE, qw), lambda b, h, i: (b, i, h)),
        out_shape=jax.ShapeDtypeStruct((batch, n_lat, n_heads * HEAD_W), BF16),
        scratch_shapes=[pltpu.VMEM((2 * (n_lat + n_ctx), ROW_TILE), F32), pltpu.VMEM((SUBLANES, ROW_TILE), F32)],
        compiler_params=_params(3),
        name=name,
    )(lam, subln, q, q, k, k_ctx, vt, vt_ctx)
    if not need_ctx:
        return o_lat, None
    o_ctx = pl.pallas_call(
        functools.partial(_attn_ctx_kernel, **static),
        grid=(batch, n_groups),
        in_specs=[
            _resident(lam.shape),
            _resident(subln.shape),
            pl.BlockSpec((1, n_ctx, qw), lambda b, h: (b, 0, h)),
            pl.BlockSpec((1, n_ctx, kw), lambda b, h: (b, 0, h)),
            pl.BlockSpec((1, kw, n_ctx), lambda b, h: (b, h, 0)),
        ],
        out_specs=pl.BlockSpec((1, n_ctx, qw), lambda b, h: (b, 0, h)),
        out_shape=jax.ShapeDtypeStruct((batch, n_ctx, n_heads * HEAD_W), BF16),
        scratch_shapes=[pltpu.VMEM((2 * n_ctx, n_ctx), F32)],
        compiler_params=_params(2),
        name=name + "_ctx",
    )(lam, subln, q_ctx, k_ctx, vt_ctx)
    return o_lat, o_ctx


DFT_COL_TILE = 1024
DFT_EXTRA_ROWS = 16


def _dft_mirror_kernel(cl_ref, sl_ref, ch_ref, flip_ref, xc_ref, xs_ref, o_ref, carry_scr, *, n_steps):
    s = pl.program_id(2)
    t = n_steps - 1 - s

    @pl.when(s == 0)
    def _():
        carry_scr[...] = _dot(ch_ref[...], xc_ref[0])

    a = _dot(cl_ref[...], xc_ref[0])
    b = _dot(sl_ref[...], xs_ref[0])
    total = a + b
    row = lax.broadcasted_iota(jnp.int32, total.shape, 0)
    mirror = _dot(flip_ref[...], total.astype(BF16))
    mirror = jnp.where(row == 0, carry_scr[0:1, :], mirror)
    carry_scr[0:1, :] = total[0:1]
    o_ref[0, pl.ds(pl.multiple_of(t * ROW_TILE, ROW_TILE), ROW_TILE), :] = (a - b).astype(BF16)
    lo = pl.multiple_of((2 * n_steps - 1 - t) * ROW_TILE, ROW_TILE)
    o_ref[0, pl.ds(lo, ROW_TILE), :] = mirror.astype(BF16)


def _position_dft(xc, xs):
    batch, n, d = xc.shape
    half = n // 2
    n_steps = half // ROW_TILE
    cl, sl = _dft_tables(n, n ** -0.5, n_rows=half, negate_sin=False)
    alt = jnp.where(jnp.arange(n, dtype=jnp.int32) % 2 == 0, n ** -0.5, -(n ** -0.5)).astype(BF16)
    c_half = jnp.zeros((DFT_EXTRA_ROWS, n), BF16).at[0].set(alt)
    flip = np.zeros((ROW_TILE, ROW_TILE), np.float32)
    flip[np.arange(1, ROW_TILE), ROW_TILE - np.arange(1, ROW_TILE)] = 1.0
    tile_rows = lambda b, c, s: (n_steps - 1 - s, 0)
    return pl.pallas_call(
        functools.partial(_dft_mirror_kernel, n_steps=n_steps),
        grid=(batch, d // DFT_COL_TILE, n_steps),
        in_specs=[
            pl.BlockSpec((ROW_TILE, n), tile_rows),
            pl.BlockSpec((ROW_TILE, n), tile_rows),
            _resident((DFT_EXTRA_ROWS, n)),
            _resident((ROW_TILE, ROW_TILE)),
            pl.BlockSpec((1, n, DFT_COL_TILE), lambda b, c, s: (b, 0, c), pipeline_mode=pl.Buffered(1)),
            pl.BlockSpec((1, n, DFT_COL_TILE), lambda b, c, s: (b, 0, c), pipeline_mode=pl.Buffered(1)),
        ],
        out_specs=pl.BlockSpec((1, n, DFT_COL_TILE), lambda b, c, s: (b, 0, c)),
        out_shape=jax.ShapeDtypeStruct((batch, n, d), BF16),
        scratch_shapes=[pltpu.VMEM((DFT_EXTRA_ROWS, DFT_COL_TILE), F32)],
        compiler_params=_params(3),
        name="position_dft",
    )(cl, sl, c_half, jnp.asarray(flip, dtype=BF16), xc, xs)


def _dft_small_kernel(c_ref, s_ref, xc_ref, xs_ref, o_ref):
    o_ref[0] = (_dot(c_ref[...], xc_ref[0]) + _dot(s_ref[...], xs_ref[0])).astype(BF16)


def _position_dft_small(xc, xs):
    batch, n, d = xc.shape
    cc, sc = _dft_tables(n, n ** -0.5)
    whole = lambda b: (b, 0, 0)
    return pl.pallas_call(
        _dft_small_kernel,
        grid=(batch,),
        in_specs=[
            _resident((n, n)),
            _resident((n, n)),
            pl.BlockSpec((1, n, d), whole),
            pl.BlockSpec((1, n, d), whole),
        ],
        out_specs=pl.BlockSpec((1, n, d), whole),
        out_shape=jax.ShapeDtypeStruct((batch, n, d), BF16),
        compiler_params=_params(1),
        name="position_dft_ctx",
    )(cc, sc, xc, xs)


def _out_ffn_kernel(a_ref, x_ref, wo_ref, bo_ref, mod_ref, ln_ref, wgu_ref, wd_ref, fn_ref, o_ref, a_scr,
                    *, d_ff, final_norm):
    mod = mod_ref[0]
    rows = x_ref.shape[1]
    n_part = 2 if rows >= 2 * ROW_TILE else 1
    part = rows // n_part
    xs, hs = [], []
    for p in range(n_part):
        r = slice(p * part, (p + 1) * part)
        x = x_ref[0, r, :] + mod[G1:G1 + 1] * (_dot(a_ref[0, r, :], wo_ref[...]) + bo_ref[...])
        xs.append(x)
        hs.append(_norm_mod(x, ln_ref[...], mod, SH2, SC2))
    for lo in range(0, d_ff, FF_CHUNK):
        hi = min(lo + FF_CHUNK, d_ff)
        for p in range(n_part):
            g = _dot(hs[p], wgu_ref[0, :, lo:hi])
            u = _dot(hs[p], wgu_ref[0, :, d_ff + lo:d_ff + hi])
            a_scr[p * part:(p + 1) * part, lo:hi] = (_silu(g) * u).astype(BF16)
    for p in range(n_part):
        r = slice(p * part, (p + 1) * part)
        y = xs[p] + mod[G2:G2 + 1] * _dot(a_scr[r, :], wd_ref[0])
        if final_norm:
            y = _rms(y, fn_ref[...])
        o_ref[0, r, :] = y


def _out_ffn(a, x, mod, mod_row, w_o, b_o, ln, w_gu, w_down, layer, final_g, *, row_tile, final_norm):
    batch, rows, d = x.shape
    k = a.shape[2]
    d_ff = w_down.shape[1]
    row = lambda b, j: (b, j, 0)
    layer_block = lambda shape: pl.BlockSpec((1,) + shape, lambda b, j: (layer, 0, 0), pipeline_mode=pl.Buffered(1))
    return pl.pallas_call(
        functools.partial(_out_ffn_kernel, d_ff=d_ff, final_norm=final_norm),
        grid=(batch, rows // row_tile),
        in_specs=[
            pl.BlockSpec((1, row_tile, k), row),
            pl.BlockSpec((1, row_tile, d), row),
            _resident((k, d)),
            _resident((1, d)),
            _mod_spec(d, mod_row),
            _resident((1, d)),
            layer_block((d, 2 * d_ff)),
            layer_block((d_ff, d)),
            _resident((1, d)),
        ],
        out_specs=pl.BlockSpec((1, row_tile, d), row),
        out_shape=jax.ShapeDtypeStruct((batch, rows, d), F32),
        scratch_shapes=[pltpu.VMEM((row_tile, d_ff), BF16)],
        compiler_params=_params(2),
        name="out_ffn",
    )(a, x, w_o, b_o, mod, ln, w_gu, w_down, final_g)


def _half_split(w, n_comp, comp_dim):
    lead = w.shape[:-1]
    heads = w.shape[-1] // HEAD_W
    w = w.reshape(lead + (heads, n_comp, comp_dim // 2, 2))
    return jnp.moveaxis(w, -1, -3).reshape(lead + (heads * HEAD_W,))


def _rope_lane_tables(seq, head_dim):
    axis_dim = head_dim // 2
    rows = jnp.arange(seq, dtype=jnp.int32) // GRID_W
    cols = jnp.arange(seq, dtype=jnp.int32) % GRID_W
    inv_freq = ROPE_THETA ** (-jnp.arange(0, axis_dim, 2, dtype=F32) / axis_dim)
    ang = jnp.concatenate([rows[:, None].astype(F32) * inv_freq,
                           cols[:, None].astype(F32) * inv_freq], axis=-1)
    reps = (HEAD_W // 2) // (head_dim // 2)
    cos = jnp.tile(jnp.cos(ang), (1, 2 * reps))
    sin = jnp.tile(jnp.sin(ang), (1, reps))
    return cos, jnp.concatenate([-sin, sin], axis=-1)


def _dft_tables(n, scale, n_rows=None, negate_sin=True):
    n_rows = n if n_rows is None else n_rows
    r = 1
    while r * r < n:
        r *= 2
    q = n // r
    k = jnp.arange(n_rows, dtype=jnp.int32)[:, None]
    a_idx = (k * jnp.arange(q, dtype=jnp.int32)[None, :] * r) % n
    b_idx = (k * jnp.arange(r, dtype=jnp.int32)[None, :]) % n
    w = 2.0 * math.pi / n
    ca, sa = jnp.cos(a_idx.astype(F32) * w), jnp.sin(a_idx.astype(F32) * w)
    cb, sb = jnp.cos(b_idx.astype(F32) * w) * scale, jnp.sin(b_idx.astype(F32) * w) * scale
    cos = ca[:, :, None] * cb[:, None, :] - sa[:, :, None] * sb[:, None, :]
    sin = sa[:, :, None] * cb[:, None, :] + ca[:, :, None] * sb[:, None, :]
    sin = -sin if negate_sin else sin
    return cos.reshape(n_rows, n).astype(BF16), sin.reshape(n_rows, n).astype(BF16)


def _channel_dft_weight(gd):
    pos = jnp.arange(gd, dtype=jnp.int32)
    ang = ((pos[:, None] * pos[None, :]) % gd).astype(F32) * (2.0 * math.pi / gd)
    return (jnp.concatenate([jnp.cos(ang), jnp.sin(ang)], axis=1) * gd ** -0.5).astype(BF16)


def kernel(x, c, ctx, c_ctx, mod_w, mod_b, ln_mix, ln_ffn, ffn_w_gu, ffn_w_down, a_w_qkv, a_lam, a_subln, a_w_o, b_w_qkv, b_q_norm, b_k_norm, b_w_o, c_w_o, c_b_o, final_norm):
    batch, seq, d = x.shape
    n_ctx = ctx.shape[1]
    depth = mod_w.shape[0]
    assert seq % FFN_ROW_TILE == 0 and n_ctx == ROW_TILE and batch < MOD_ROWS

    cs = jnp.concatenate([c, c_ctx[None, :], jnp.zeros((MOD_ROWS - batch - 1, d), F32)], axis=0)
    mod_all = _modulation(cs, mod_w, mod_b).reshape(depth, MOD_ROWS, 6, d)
    lat = dict(mod_row=None, row_tile=BIG_ROW_TILE)
    con = dict(mod_row=batch, row_tile=n_ctx)

    rope_a = _rope_lane_tables(seq, DIFF_HEAD_DIM)
    rope_b = _rope_lane_tables(seq, GQA_HEAD_DIM)
    rope_ctx = (jnp.ones((n_ctx, HEAD_W), F32), jnp.zeros((n_ctx, HEAD_W), F32))
    ones_head = jnp.ones((1, HEAD_W), F32)
    no_bias = jnp.zeros((1, d), F32)

    w_gu_all, w_down_all = ffn_w_gu.astype(BF16), ffn_w_down.astype(BF16)
    x_lat, x_ctx = x, ctx
    for i in range(depth):
        last = i == depth - 1
        kind, j = i % N_MIXERS, i // N_MIXERS
        mod = mod_all[i]
        ln1 = ln_mix[i][None, :]

        if kind == 2:
            w_chan = _channel_dft_weight(d // FNET_GROUPS)
            mixed = _position_dft(*_channel_dft(x_lat, mod, None, ln1, w_chan, BIG_ROW_TILE))
            if not last:
                mixed_ctx = _position_dft_small(*_channel_dft(x_ctx, mod, batch, ln1, w_chan, n_ctx))
            w_o, b_o = c_w_o[j], c_b_o[j][None, :]
        else:
            if kind == 0:
                n_q = n_k = DIFF_HEADS * HEAD_W
                split, rope, w = (2, DIFF_HEAD_DIM), rope_a, a_w_qkv[j]
                proj = dict(qn=ones_head, kn=ones_head, n_q=n_q, n_k=n_k, head_norm=False,
                            q_scale=DIFF_HEAD_DIM ** -0.5 * LOG2E)
                attn = dict(lam=a_lam[j], subln=a_subln[j][None, :], diff=True, lambda_init=_lambda_init(i),
                            n_heads=DIFF_HEADS, heads_per_step=DIFF_HEADS_PER_STEP,
                            kv_heads_per_step=DIFF_HEADS_PER_STEP)
                w_o = a_w_o[j]
            else:
                n_q, n_k = GQA_HEADS * HEAD_W, GQA_KV_HEADS * HEAD_W
                split, rope, w = (1, GQA_HEAD_DIM), rope_b, b_w_qkv[j]
                proj = dict(qn=_half_split(b_q_norm[j], *split)[None, :], kn=_half_split(b_k_norm[j], *split)[None, :],
                            n_q=n_q, n_k=n_k, head_norm=True, q_scale=GQA_HEAD_DIM ** -0.5 * LOG2E)
                attn = dict(lam=jnp.zeros((4, DIFF_HEAD_DIM), F32), subln=ones_head, diff=False, lambda_init=0.0,
                            n_heads=GQA_HEADS, heads_per_step=GQA_HEADS, kv_heads_per_step=GQA_KV_HEADS)
                w_o = b_w_o[j]
            w_qk = _half_split(w[:, :n_q + n_k].astype(BF16), *split)
            w_vt = w[:, n_q + n_k:].T.astype(BF16)
            lat_proj = dict(lat, row_tile=ROW_TILE) if proj["head_norm"] else lat
            q, k, vt = _qkv_project(x_lat, mod, ln=ln1, w_qk=w_qk, w_vt=w_vt, cosf=rope[0], sinf=rope[1],
                                    **lat_proj, **proj)
            q_ctx, k_ctx, vt_ctx = _qkv_project(x_ctx, mod, ln=ln1, w_qk=w_qk, w_vt=w_vt, cosf=rope_ctx[0],
                                                sinf=rope_ctx[1], **con, **proj)
            mixed, mixed_ctx = _attention(q, k, vt, q_ctx, k_ctx, vt_ctx, need_ctx=not last, **attn)
            b_o = no_bias

        ffn = dict(w_o=w_o.astype(BF16), b_o=b_o, ln=ln_ffn[i][None, :], w_gu=w_gu_all, w_down=w_down_all,
                   layer=i, final_g=final_norm[None, :], final_norm=last)
        if not last:
            x_ctx = _out_ffn(mixed_ctx, x_ctx, mod, **con, **ffn)
        x_lat = _out_ffn(mixed, x_lat, mod, **dict(lat, row_tile=FFN_ROW_TILE), **ffn)
    return x_lat
```

```python
import functools
import math

import numpy as np
import jax
import jax.numpy as jnp
from jax import lax
from jax.experimental import pallas as pl
from jax.experimental.pallas import tpu as pltpu

GRID_W = 64
ROPE_THETA = 10000.0
NORM_EPS = 1e-6
DIFF_HEADS = 8
DIFF_HEAD_DIM = 64
GQA_HEADS = 8
GQA_KV_HEADS = 2
GQA_HEAD_DIM = 128
FNET_GROUPS = 4
N_MIXERS = 3

LANES = 128
SUBLANES = 8
VMEM_LIMIT = 56 * 1024 * 1024

ROW_TILE = 256
BIG_ROW_TILE = 1024
FFN_ROW_TILE = 1024
HEAD_W = 128
KEY_CHUNK = 256
LAT_KEY_CHUNK = 1024
SCORE_LEAD = 1
FF_CHUNK = 256
ONES_ROWS = 16
MOD_ROWS = 16
MOD_COL_TILE = 1536
DIFF_HEADS_PER_STEP = 4
LOG2E = math.log2(math.e)

BF16 = jnp.bfloat16
F32 = jnp.float32

SH1, SC1, G1, SH2, SC2, G2 = range(6)


def _lambda_init(layer_idx):
    return 0.8 - 0.6 * float(np.exp(-0.3 * layer_idx))


def _params(n_grid):
    return pltpu.CompilerParams(
        dimension_semantics=("arbitrary",) * n_grid, vmem_limit_bytes=VMEM_LIMIT)


def _resident(shape):
    nd = len(shape)
    return pl.BlockSpec(shape, lambda *_: (0,) * nd, pipeline_mode=pl.Buffered(1))


def _dot(a, b):
    return jnp.dot(a, b, preferred_element_type=F32)


def _dot_nt(a, b):
    return lax.dot_general(a, b, (((1,), (1,)), ((), ())), preferred_element_type=F32)


def _rms(x, g):
    return x * lax.rsqrt(jnp.mean(x * x, axis=-1, keepdims=True) + NORM_EPS) * g


def _silu(x):
    return x / (1.0 + jnp.exp(-x))


def _mod_kernel(cs_ref, w_ref, b_ref, o_ref):
    s = _silu(cs_ref[...]).astype(BF16)
    o_ref[0] = _dot(s, w_ref[0].astype(BF16)) + b_ref[0]


def _modulation(cs, mod_w, mod_b):
    depth, d, n = mod_w.shape
    tn = MOD_COL_TILE
    return pl.pallas_call(
        _mod_kernel,
        grid=(depth, n // tn),
        in_specs=[
            pl.BlockSpec((MOD_ROWS, d), lambda i, j: (0, 0)),
            pl.BlockSpec((1, d, tn), lambda i, j: (i, 0, j)),
            pl.BlockSpec((1, 1, tn), lambda i, j: (i, 0, j)),
        ],
        out_specs=pl.BlockSpec((1, MOD_ROWS, tn), lambda i, j: (i, 0, j)),
        out_shape=jax.ShapeDtypeStruct((depth, MOD_ROWS, n), F32),
        compiler_params=_params(2),
        name="modulation",
    )(cs, mod_w, mod_b.reshape(depth, 1, n))


def _mod_spec(d, mod_row):
    if mod_row is None:
        return pl.BlockSpec((1, 6, d), lambda b, j: (b, 0, 0))
    return pl.BlockSpec((1, 6, d), lambda b, j: (mod_row, 0, 0))


def _norm_mod(x, ln, mod, shift, scale):
    y = _rms(x, ln)
    return (y * (1.0 + mod[scale:scale + 1]) + mod[shift:shift + 1]).astype(BF16)


def _qkv_kernel(x_ref, mod_ref, ln_ref, w_ref, wvt_ref, cos_ref, sin_ref, qn_ref, kn_ref,
                q_ref, k_ref, vt_ref, *, n_q, n_k, head_norm, q_scale):
    h = _norm_mod(x_ref[0], ln_ref[...], mod_ref[0], SH1, SC1)
    cos, sin = cos_ref[...], sin_ref[...]
    for head in range((n_q + n_k) // HEAD_W):
        is_q = head < n_q // HEAD_W
        if head % 2 == 0:
            pair = _dot(h, w_ref[:, head * HEAD_W:(head + 2) * HEAD_W])
        t = pair[:, (head % 2) * HEAD_W:(head % 2 + 1) * HEAD_W]
        if head_norm:
            t = _rms(t, qn_ref[...] if is_q else kn_ref[...])
        t = t * cos + pltpu.roll(t, HEAD_W // 2, 1) * sin
        if is_q:
            q_ref[0, :, head * HEAD_W:(head + 1) * HEAD_W] = (t * q_scale).astype(BF16)
        else:
            c0 = head * HEAD_W - n_q
            k_ref[0, :, c0:c0 + HEAD_W] = t.astype(BF16)
    vt_ref[0] = _dot_nt(wvt_ref[...], h).astype(BF16)


def _qkv_project(x, mod, mod_row, ln, w_qk, w_vt, cosf, sinf, qn, kn, *, row_tile, n_q, n_k, head_norm,
                 q_scale):
    batch, rows, d = x.shape
    n_v = w_vt.shape[0]
    row = lambda b, j: (b, j, 0)
    kern = functools.partial(_qkv_kernel, n_q=n_q, n_k=n_k, head_norm=head_norm, q_scale=q_scale)
    return pl.pallas_call(
        kern,
        grid=(batch, rows // row_tile),
        in_specs=[
            pl.BlockSpec((1, row_tile, d), row),
            _mod_spec(d, mod_row),
            _resident((1, d)),
            _resident((d, n_q + n_k)),
            _resident((n_v, d)),
            pl.BlockSpec((row_tile, HEAD_W), lambda b, j: (j, 0)),
            pl.BlockSpec((row_tile, HEAD_W), lambda b, j: (j, 0)),
            _resident((1, HEAD_W)),
            _resident((1, HEAD_W)),
        ],
        out_specs=[
            pl.BlockSpec((1, row_tile, n_q), row),
            pl.BlockSpec((1, row_tile, n_k), row),
            pl.BlockSpec((1, n_v, row_tile), lambda b, j: (b, 0, j)),
        ],
        out_shape=[
            jax.ShapeDtypeStruct((batch, rows, n_q), BF16),
            jax.ShapeDtypeStruct((batch, rows, n_k), BF16),
            jax.ShapeDtypeStruct((batch, n_v, rows), BF16),
        ],
        compiler_params=_params(2),
        name="qkv_project",
    )(x, mod, ln, w_qk, w_vt, cosf, sinf, qn, kn)


def _chan_dft_kernel(x_ref, mod_ref, ln_ref, w_ref, oc_ref, os_ref):
    h = _norm_mod(x_ref[0], ln_ref[...], mod_ref[0], SH1, SC1)
    d = h.shape[1]
    gd = w_ref.shape[0]
    for g in range(d // gd):
        y = _dot(h[:, g * gd:(g + 1) * gd], w_ref[...])
        oc_ref[0, :, g * gd:(g + 1) * gd] = y[:, :gd].astype(BF16)
        os_ref[0, :, g * gd:(g + 1) * gd] = y[:, gd:].astype(BF16)


def _channel_dft(x, mod, mod_row, ln, w, row_tile):
    batch, rows, d = x.shape
    row = lambda b, j: (b, j, 0)
    return pl.pallas_call(
        _chan_dft_kernel,
        grid=(batch, rows // row_tile),
        in_specs=[
            pl.BlockSpec((1, row_tile, d), row),
            _mod_spec(d, mod_row),
            _resident((1, d)),
            _resident(w.shape),
        ],
        out_specs=[pl.BlockSpec((1, row_tile, d), row)] * 2,
        out_shape=[jax.ShapeDtypeStruct((batch, rows, d), BF16)] * 2,
        compiler_params=_params(2),
        name="channel_dft",
    )(x, mod, ln, w)


def _kv_chunks(klat_ref, kctx_ref, vtlat_ref, vtctx_ref):
    table, n_keys = [], 0
    for k_ref, vt_ref, size in ((klat_ref, vtlat_ref, LAT_KEY_CHUNK), (kctx_ref, vtctx_ref, KEY_CHUNK)):
        if k_ref is not None:
            for start in range(0, k_ref.shape[1], size):
                table.append((k_ref, vt_ref, start, size, n_keys))
                n_keys += size
    return table, n_keys


def _score_chunk(chunk, n_keys, s_scr, q, kvh, buf, m8):
    k_ref, _, start, size, row = chunk
    tq = q.shape[0]
    s = _dot_nt(k_ref[0, start:start + size, kvh * HEAD_W:(kvh + 1) * HEAD_W], q)
    s_scr[buf * n_keys + row:buf * n_keys + row + size, :] = s
    return jnp.maximum(m8, jnp.max(s.reshape(size // SUBLANES, SUBLANES, tq), axis=0))


def _softmax_pv_streams(score_list, weigh_list, offset, chunks, n_keys, s_scr, m8_carried, on_done):
    tq = score_list[0][0].shape[0]
    n_chunks = len(chunks)

    def weigh(kvh, buf, c, mx, acc):
        _, vt_ref, start, size, row = chunks[c]
        e = jnp.exp2(s_scr[buf * n_keys + row:buf * n_keys + row + size, :] - mx)
        v_ones = jnp.concatenate([vt_ref[0, kvh * HEAD_W:(kvh + 1) * HEAD_W, start:start + size],
                                  jnp.ones((ONES_ROWS, size), BF16)], axis=0)
        return acc + _dot(v_ones, e.astype(BF16))

    m8 = [None] * len(score_list)
    mx = acc = None
    n_steps = max(len(score_list) * n_chunks, offset + len(weigh_list) * n_chunks)
    for g in range(n_steps):
        if g < len(score_list) * n_chunks:
            j, c = divmod(g, n_chunks)
            q, kvh, buf = score_list[j]
            if c == 0:
                m8[j] = jnp.full((SUBLANES, tq), -jnp.inf, F32)
            m8[j] = _score_chunk(chunks[c], n_keys, s_scr, q, kvh, buf, m8[j])
        ge = g - offset
        if 0 <= ge < len(weigh_list) * n_chunks:
            p, c = divmod(ge, n_chunks)
            kvh, buf, max_src = weigh_list[p]
            if c == 0:
                src = m8_carried if max_src is None else m8[max_src]
                mx = jnp.max(src, axis=0, keepdims=True)
                acc = jnp.zeros((HEAD_W + ONES_ROWS, tq), F32)
            acc = weigh(kvh, buf, c, mx, acc)
            if c == n_chunks - 1:
                on_done(p, acc[:HEAD_W] * (1.0 / acc[HEAD_W:HEAD_W + 1]))
    return m8


def _head_problems(q_ref, diff, heads_per_step, kv_heads_per_step, n_heads=None):
    problems = []
    for hh in range(heads_per_step if n_heads is None else n_heads):
        q = q_ref[0, :, hh * HEAD_W:(hh + 1) * HEAD_W]
        kvh = hh * kv_heads_per_step // heads_per_step
        if diff:
            lane = lax.broadcasted_iota(jnp.int32, q.shape, 1)
            first = (lane % (HEAD_W // 2)) < (HEAD_W // 4)
            zero = jnp.zeros_like(q)
            problems += [(jnp.where(first, q, zero), kvh), (jnp.where(first, zero, q), kvh)]
        else:
            problems.append((q, kvh))
    return problems


def _head_writer(o_ref, lam_ref, subln_ref, diff, lambda_init):
    first_comp = {}

    def on_done(p, o_t):
        if not diff:
            o_ref[0, :, p * HEAD_W:(p + 1) * HEAD_W] = o_t.T.astype(BF16)
            return
        hh = p // 2
        if p % 2 == 0:
            first_comp[hh] = o_t
            return
        lv = lam_ref[...]
        lam = (jnp.exp(jnp.sum(lv[0:1] * lv[1:2], axis=-1, keepdims=True))
               - jnp.exp(jnp.sum(lv[2:3] * lv[3:4], axis=-1, keepdims=True)) + lambda_init)
        o = (first_comp.pop(hh) - lam * o_t).T
        o = _rms(o, subln_ref[...]) * (1.0 - lambda_init)
        o_ref[0, :, hh * HEAD_W:(hh + 1) * HEAD_W] = o.astype(BF16)

    return on_done


def _attn_latent_kernel(lam_ref, subln_ref, q_ref, qn_ref, klat_ref, kctx_ref, vtlat_ref, vtctx_ref, o_ref,
                        s_scr, m8_scr, *, diff, lambda_init, heads_per_step, kv_heads_per_step):
    chunks, n_keys = _kv_chunks(klat_ref, kctx_ref, vtlat_ref, vtctx_ref)
    cur = _head_problems(q_ref, diff, heads_per_step, kv_heads_per_step)
    nxt_q, nxt_kvh = _head_problems(qn_ref, diff, heads_per_step, kv_heads_per_step, n_heads=1)[0]
    n_prob = len(cur)
    assert n_prob % 2 == 0

    @pl.when(pl.program_id(2) == 0)
    def _():
        q, kvh = cur[0]
        m8 = jnp.full((SUBLANES, q.shape[0]), -jnp.inf, F32)
        for chunk in chunks:
            m8 = _score_chunk(chunk, n_keys, s_scr, q, kvh, 0, m8)
        m8_scr[...] = m8

    score_list = [(q, kvh, p % 2) for p, (q, kvh) in enumerate(cur) if p > 0] + [(nxt_q, nxt_kvh, 0)]
    weigh_list = [(kvh, p % 2, None if p == 0 else p - 1) for p, (_, kvh) in enumerate(cur)]
    m8 = _softmax_pv_streams(
        score_list, weigh_list, min(SCORE_LEAD, len(chunks) - 1), chunks, n_keys, s_scr, m8_scr[...],
        _head_writer(o_ref, lam_ref, subln_ref, diff, lambda_init))
    m8_scr[...] = m8[-1]


def _attn_ctx_kernel(lam_ref, subln_ref, q_ref, k_ref, vt_ref, o_ref, s_scr,
                     *, diff, lambda_init, heads_per_step, kv_heads_per_step):
    chunks, n_keys = _kv_chunks(None, k_ref, None, vt_ref)
    n_chunks = len(chunks)
    cur = _head_problems(q_ref, diff, heads_per_step, kv_heads_per_step)
    score_list = [(q, kvh, p % 2) for p, (q, kvh) in enumerate(cur)]
    weigh_list = [(kvh, p % 2, p) for p, (_, kvh) in enumerate(cur)]
    _softmax_pv_streams(
        score_list, weigh_list, n_chunks + min(SCORE_LEAD, n_chunks - 1), chunks, n_keys, s_scr, None,
        _head_writer(o_ref, lam_ref, subln_ref, diff, lambda_init))


def _attention(q, k, vt, q_ctx, k_ctx, vt_ctx, lam, subln, *, diff, lambda_init, n_heads, heads_per_step,
               kv_heads_per_step, need_ctx):
    batch, n_lat, _ = q.shape
    n_ctx = k_ctx.shape[1]
    qw = heads_per_step * HEAD_W
    kw = kv_heads_per_step * HEAD_W
    n_groups = n_heads // heads_per_step
    n_lat_tiles = n_lat // ROW_TILE
    static = dict(diff=diff, lambda_init=lambda_init, heads_per_step=heads_per_step,
                  kv_heads_per_step=kv_heads_per_step)
    name = "diff_attention" if diff else "gqa_attention"
    o_lat = pl.pallas_call(
        functools.partial(_attn_latent_kernel, **static),
        grid=(batch, n_groups, n_lat_tiles),
        in_specs=[
            _resident(lam.shape),
            _resident(subln.shape),
            pl.BlockSpec((1, ROW_TILE, qw), lambda b, h, i: (b, i, h)),
            pl.BlockSpec((1, ROW_TILE, qw), lambda b, h, i: (b, jnp.minimum(i + 1, n_lat_tiles - 1), h)),
            pl.BlockSpec((1, n_lat, kw), lambda b, h, i: (b, 0, h)),
            pl.BlockSpec((1, n_ctx, kw), lambda b, h, i: (b, 0, h)),
            pl.BlockSpec((1, kw, n_lat), lambda b, h, i: (b, h, 0)),
            pl.BlockSpec((1, kw, n_ctx), lambda b, h, i: (b, h, 0)),
        ],
        out_specs=pl.BlockSpec((1, ROW_TILE, qw), lambda b, h, i: (b, i, h)),
        out_shape=jax.ShapeDtypeStruct((batch, n_lat, n_heads * HEAD_W), BF16),
        scratch_shapes=[pltpu.VMEM((2 * (n_lat + n_ctx), ROW_TILE), F32), pltpu.VMEM((SUBLANES, ROW_TILE), F32)],
        compiler_params=_params(3),
        name=name,
    )(lam, subln, q, q, k, k_ctx, vt, vt_ctx)
    if not need_ctx:
        return o_lat, None
    o_ctx = pl.pallas_call(
        functools.partial(_attn_ctx_kernel, **static),
        grid=(batch, n_groups),
        in_specs=[
            _resident(lam.shape),
            _resident(subln.shape),
            pl.BlockSpec((1, n_ctx, qw), lambda b, h: (b, 0, h)),
            pl.BlockSpec((1, n_ctx, kw), lambda b, h: (b, 0, h)),
            pl.BlockSpec((1, kw, n_ctx), lambda b, h: (b, h, 0)),
        ],
        out_specs=pl.BlockSpec((1, n_ctx, qw), lambda b, h: (b, 0, h)),
        out_shape=jax.ShapeDtypeStruct((batch, n_ctx, n_heads * HEAD_W), BF16),
        scratch_shapes=[pltpu.VMEM((2 * n_ctx, n_ctx), F32)],
        compiler_params=_params(2),
        name=name + "_ctx",
    )(lam, subln, q_ctx, k_ctx, vt_ctx)
    return o_lat, o_ctx


DFT_COL_TILE = 1024
DFT_EXTRA_ROWS = 16


def _dft_mirror_kernel(cl_ref, sl_ref, ch_ref, flip_ref, xc_ref, xs_ref, o_ref, carry_scr, *, n_steps):
    s = pl.program_id(2)
    t = n_steps - 1 - s

    @pl.when(s == 0)
    def _():
        carry_scr[...] = _dot(ch_ref[...], xc_ref[0])

    a = _dot(cl_ref[...], xc_ref[0])
    b = _dot(sl_ref[...], xs_ref[0])
    total = a + b
    row = lax.broadcasted_iota(jnp.int32, total.shape, 0)
    mirror = _dot(flip_ref[...], total.astype(BF16))
    mirror = jnp.where(row == 0, carry_scr[0:1, :], mirror)
    carry_scr[0:1, :] = total[0:1]
    o_ref[0, pl.ds(pl.multiple_of(t * ROW_TILE, ROW_TILE), ROW_TILE), :] = (a - b).astype(BF16)
    lo = pl.multiple_of((2 * n_steps - 1 - t) * ROW_TILE, ROW_TILE)
    o_ref[0, pl.ds(lo, ROW_TILE), :] = mirror.astype(BF16)


def _position_dft(xc, xs):
    batch, n, d = xc.shape
    half = n // 2
    n_steps = half // ROW_TILE
    cl, sl = _dft_tables(n, n ** -0.5, n_rows=half, negate_sin=False)
    alt = jnp.where(jnp.arange(n, dtype=jnp.int32) % 2 == 0, n ** -0.5, -(n ** -0.5)).astype(BF16)
    c_half = jnp.zeros((DFT_EXTRA_ROWS, n), BF16).at[0].set(alt)
    flip = np.zeros((ROW_TILE, ROW_TILE), np.float32)
    flip[np.arange(1, ROW_TILE), ROW_TILE - np.arange(1, ROW_TILE)] = 1.0
    tile_rows = lambda b, c, s: (n_steps - 1 - s, 0)
    return pl.pallas_call(
        functools.partial(_dft_mirror_kernel, n_steps=n_steps),
        grid=(batch, d // DFT_COL_TILE, n_steps),
        in_specs=[
            pl.BlockSpec((ROW_TILE, n), tile_rows),
            pl.BlockSpec((ROW_TILE, n), tile_rows),
            _resident((DFT_EXTRA_ROWS, n)),
            _resident((ROW_TILE, ROW_TILE)),
            pl.BlockSpec((1, n, DFT_COL_TILE), lambda b, c, s: (b, 0, c), pipeline_mode=pl.Buffered(1)),
            pl.BlockSpec((1, n, DFT_COL_TILE), lambda b, c, s: (b, 0, c), pipeline_mode=pl.Buffered(1)),
        ],
        out_specs=pl.BlockSpec((1, n, DFT_COL_TILE), lambda b, c, s: (b, 0, c)),
        out_shape=jax.ShapeDtypeStruct((batch, n, d), BF16),
        scratch_shapes=[pltpu.VMEM((DFT_EXTRA_ROWS, DFT_COL_TILE), F32)],
        compiler_params=_params(3),
        name="position_dft",
    )(cl, sl, c_half, jnp.asarray(flip, dtype=BF16), xc, xs)


def _dft_small_kernel(c_ref, s_ref, xc_ref, xs_ref, o_ref):
    o_ref[0] = (_dot(c_ref[...], xc_ref[0]) + _dot(s_ref[...], xs_ref[0])).astype(BF16)


def _position_dft_small(xc, xs):
    batch, n, d = xc.shape
    cc, sc = _dft_tables(n, n ** -0.5)
    whole = lambda b: (b, 0, 0)
    return pl.pallas_call(
        _dft_small_kernel,
        grid=(batch,),
        in_specs=[
            _resident((n, n)),
            _resident((n, n)),
            pl.BlockSpec((1, n, d), whole),
            pl.BlockSpec((1, n, d), whole),
        ],
        out_specs=pl.BlockSpec((1, n, d), whole),
        out_shape=jax.ShapeDtypeStruct((batch, n, d), BF16),
        compiler_params=_params(1),
        name="position_dft_ctx",
    )(cc, sc, xc, xs)


def _out_ffn_kernel(*refs, d_ff, final_norm, next_chan):
    a_ref, x_ref, wo_ref, bo_ref, mod_ref, ln_ref, wgu_ref, wd_ref, fn_ref = refs[:9]
    if next_chan:
        nmod_ref, nln_ref, wch_ref, o_ref, oc_ref, os_ref, a_scr = refs[9:]
    else:
        o_ref, a_scr = refs[9:]
    mod = mod_ref[0]
    rows = x_ref.shape[1]
    n_part = 2 if rows >= 2 * ROW_TILE else 1
    part = rows // n_part
    xs, hs = [], []
    for p in range(n_part):
        r = slice(p * part, (p + 1) * part)
        x = x_ref[0, r, :] + mod[G1:G1 + 1] * (_dot(a_ref[0, r, :], wo_ref[...]) + bo_ref[...])
        xs.append(x)
        hs.append(_norm_mod(x, ln_ref[...], mod, SH2, SC2))
    for lo in range(0, d_ff, FF_CHUNK):
        hi = min(lo + FF_CHUNK, d_ff)
        for p in range(n_part):
            g = _dot(hs[p], wgu_ref[0, :, lo:hi])
            u = _dot(hs[p], wgu_ref[0, :, d_ff + lo:d_ff + hi])
            a_scr[p * part:(p + 1) * part, lo:hi] = (_silu(g) * u).astype(BF16)
    for p in range(n_part):
        r = slice(p * part, (p + 1) * part)
        y = xs[p] + mod[G2:G2 + 1] * _dot(a_scr[r, :], wd_ref[0])
        if final_norm:
            y = _rms(y, fn_ref[...])
        o_ref[0, r, :] = y
        if next_chan:
            hn = _norm_mod(y, nln_ref[...], nmod_ref[0], SH1, SC1)
            gd = wch_ref.shape[0]
            for g in range(hn.shape[1] // gd):
                yy = _dot(hn[:, g * gd:(g + 1) * gd], wch_ref[...])
                oc_ref[0, r, g * gd:(g + 1) * gd] = yy[:, :gd].astype(BF16)
                os_ref[0, r, g * gd:(g + 1) * gd] = yy[:, gd:].astype(BF16)


def _out_ffn(a, x, mod, mod_row, w_o, b_o, ln, w_gu, w_down, layer, final_g, *, row_tile, final_norm,
             next_chan=None):
    batch, rows, d = x.shape
    k = a.shape[2]
    d_ff = w_down.shape[1]
    row = lambda b, j: (b, j, 0)
    layer_block = lambda shape: pl.BlockSpec((1,) + shape, lambda b, j: (layer, 0, 0), pipeline_mode=pl.Buffered(1))
    in_specs = [
        pl.BlockSpec((1, row_tile, k), row),
        pl.BlockSpec((1, row_tile, d), row),
        _resident((k, d)),
        _resident((1, d)),
        _mod_spec(d, mod_row),
        _resident((1, d)),
        layer_block((d, 2 * d_ff)),
        layer_block((d_ff, d)),
        _resident((1, d)),
    ]
    operands = [a, x, w_o, b_o, mod, ln, w_gu, w_down, final_g]
    out_specs = [pl.BlockSpec((1, row_tile, d), row)]
    out_shape = [jax.ShapeDtypeStruct((batch, rows, d), F32)]
    if next_chan is not None:
        in_specs += [_mod_spec(d, mod_row), _resident((1, d)), _resident(next_chan[2].shape)]
        operands += list(next_chan)
        out_specs += [pl.BlockSpec((1, row_tile, d), row)] * 2
        out_shape += [jax.ShapeDtypeStruct((batch, rows, d), BF16)] * 2
    outs = pl.pallas_call(
        functools.partial(_out_ffn_kernel, d_ff=d_ff, final_norm=final_norm, next_chan=next_chan is not None),
        grid=(batch, rows // row_tile),
        in_specs=in_specs,
        out_specs=out_specs,
        out_shape=out_shape,
        scratch_shapes=[pltpu.VMEM((row_tile, d_ff), BF16)],
        compiler_params=_params(2),
        name="out_ffn",
    )(*operands)
    return outs[0] if next_chan is None else outs


def _half_split(w, n_comp, comp_dim):
    lead = w.shape[:-1]
    heads = w.shape[-1] // HEAD_W
    w = w.reshape(lead + (heads, n_comp, comp_dim // 2, 2))
    return jnp.moveaxis(w, -1, -3).reshape(lead + (heads * HEAD_W,))


def _rope_lane_tables(seq, head_dim):
    axis_dim = head_dim // 2
    rows = jnp.arange(seq, dtype=jnp.int32) // GRID_W
    cols = jnp.arange(seq, dtype=jnp.int32) % GRID_W
    inv_freq = ROPE_THETA ** (-jnp.arange(0, axis_dim, 2, dtype=F32) / axis_dim)
    ang = jnp.concatenate([rows[:, None].astype(F32) * inv_freq,
                           cols[:, None].astype(F32) * inv_freq], axis=-1)
    reps = (HEAD_W // 2) // (head_dim // 2)
    cos = jnp.tile(jnp.cos(ang), (1, 2 * reps))
    sin = jnp.tile(jnp.sin(ang), (1, reps))
    return cos, jnp.concatenate([-sin, sin], axis=-1)


def _dft_tables(n, scale, n_rows=None, negate_sin=True):
    n_rows = n if n_rows is None else n_rows
    r = 1
    while r * r < n:
        r *= 2
    q = n // r
    k = jnp.arange(n_rows, dtype=jnp.int32)[:, None]
    a_idx = (k * jnp.arange(q, dtype=jnp.int32)[None, :] * r) % n
    b_idx = (k * jnp.arange(r, dtype=jnp.int32)[None, :]) % n
    w = 2.0 * math.pi / n
    ca, sa = jnp.cos(a_idx.astype(F32) * w), jnp.sin(a_idx.astype(F32) * w)
    cb, sb = jnp.cos(b_idx.astype(F32) * w) * scale, jnp.sin(b_idx.astype(F32) * w) * scale
    cos = ca[:, :, None] * cb[:, None, :] - sa[:, :, None] * sb[:, None, :]
    sin = sa[:, :, None] * cb[:, None, :] + ca[:, :, None] * sb[:, None, :]
    sin = -sin if negate_sin else sin
    return cos.reshape(n_rows, n).astype(BF16), sin.reshape(n_rows, n).astype(BF16)


def _channel_dft_weight(gd):
    pos = jnp.arange(gd, dtype=jnp.int32)
    ang = ((pos[:, None] * pos[None, :]) % gd).astype(F32) * (2.0 * math.pi / gd)
    return (jnp.concatenate([jnp.cos(ang), jnp.sin(ang)], axis=1) * gd ** -0.5).astype(BF16)


def kernel(x, c, ctx, c_ctx, mod_w, mod_b, ln_mix, ln_ffn, ffn_w_gu, ffn_w_down, a_w_qkv, a_lam, a_subln, a_w_o, b_w_qkv, b_q_norm, b_k_norm, b_w_o, c_w_o, c_b_o, final_norm):
    batch, seq, d = x.shape
    n_ctx = ctx.shape[1]
    depth = mod_w.shape[0]
    assert seq % FFN_ROW_TILE == 0 and n_ctx == ROW_TILE and batch < MOD_ROWS

    cs = jnp.concatenate([c, c_ctx[None, :], jnp.zeros((MOD_ROWS - batch - 1, d), F32)], axis=0)
    mod_all = _modulation(cs, mod_w, mod_b).reshape(depth, MOD_ROWS, 6, d)
    lat = dict(mod_row=None, row_tile=BIG_ROW_TILE)
    con = dict(mod_row=batch, row_tile=n_ctx)

    rope_a = _rope_lane_tables(seq, DIFF_HEAD_DIM)
    rope_b = _rope_lane_tables(seq, GQA_HEAD_DIM)
    rope_ctx = (jnp.ones((n_ctx, HEAD_W), F32), jnp.zeros((n_ctx, HEAD_W), F32))
    ones_head = jnp.ones((1, HEAD_W), F32)
    no_bias = jnp.zeros((1, d), F32)

    w_gu_all, w_down_all = ffn_w_gu.astype(BF16), ffn_w_down.astype(BF16)
    x_lat, x_ctx, lat_chan = x, ctx, None
    for i in range(depth):
        last = i == depth - 1
        kind, j = i % N_MIXERS, i // N_MIXERS
        mod = mod_all[i]
        ln1 = ln_mix[i][None, :]

        if kind == 2:
            w_chan = _channel_dft_weight(d // FNET_GROUPS)
            if lat_chan is None:
                lat_chan = _channel_dft(x_lat, mod, None, ln1, w_chan, BIG_ROW_TILE)
            mixed = _position_dft(*lat_chan)
            if not last:
                mixed_ctx = _position_dft_small(*_channel_dft(x_ctx, mod, batch, ln1, w_chan, n_ctx))
            w_o, b_o = c_w_o[j], c_b_o[j][None, :]
        else:
            if kind == 0:
                n_q = n_k = DIFF_HEADS * HEAD_W
                split, rope, w = (2, DIFF_HEAD_DIM), rope_a, a_w_qkv[j]
                proj = dict(qn=ones_head, kn=ones_head, n_q=n_q, n_k=n_k, head_norm=False,
                            q_scale=DIFF_HEAD_DIM ** -0.5 * LOG2E)
                attn = dict(lam=a_lam[j], subln=a_subln[j][None, :], diff=True, lambda_init=_lambda_init(i),
                            n_heads=DIFF_HEADS, heads_per_step=DIFF_HEADS_PER_STEP,
                            kv_heads_per_step=DIFF_HEADS_PER_STEP)
                w_o = a_w_o[j]
            else:
                n_q, n_k = GQA_HEADS * HEAD_W, GQA_KV_HEADS * HEAD_W
                split, rope, w = (1, GQA_HEAD_DIM), rope_b, b_w_qkv[j]
                proj = dict(qn=_half_split(b_q_norm[j], *split)[None, :], kn=_half_split(b_k_norm[j], *split)[None, :],
                            n_q=n_q, n_k=n_k, head_norm=True, q_scale=GQA_HEAD_DIM ** -0.5 * LOG2E)
                attn = dict(lam=jnp.zeros((4, DIFF_HEAD_DIM), F32), subln=ones_head, diff=False, lambda_init=0.0,
                            n_heads=GQA_HEADS, heads_per_step=GQA_HEADS, kv_heads_per_step=GQA_KV_HEADS)
                w_o = b_w_o[j]
            w_qk = _half_split(w[:, :n_q + n_k].astype(BF16), *split)
            w_vt = w[:, n_q + n_k:].T.astype(BF16)
            lat_proj = dict(lat, row_tile=ROW_TILE) if proj["head_norm"] else lat
            q, k, vt = _qkv_project(x_lat, mod, ln=ln1, w_qk=w_qk, w_vt=w_vt, cosf=rope[0], sinf=rope[1],
                                    **lat_proj, **proj)
            q_ctx, k_ctx, vt_ctx = _qkv_project(x_ctx, mod, ln=ln1, w_qk=w_qk, w_vt=w_vt, cosf=rope_ctx[0],
                                                sinf=rope_ctx[1], **con, **proj)
            mixed, mixed_ctx = _attention(q, k, vt, q_ctx, k_ctx, vt_ctx, need_ctx=not last, **attn)
            b_o = no_bias

        ffn = dict(w_o=w_o.astype(BF16), b_o=b_o, ln=ln_ffn[i][None, :], w_gu=w_gu_all, w_down=w_down_all,
                   layer=i, final_g=final_norm[None, :], final_norm=last)
        if not last:
            x_ctx = _out_ffn(mixed_ctx, x_ctx, mod, **con, **ffn)
        if i + 1 < depth and (i + 1) % N_MIXERS == 2:
            nxt = (mod_all[i + 1], ln_mix[i + 1][None, :], _channel_dft_weight(d // FNET_GROUPS))
            x_lat, *lat_chan = _out_ffn(mixed, x_lat, mod, **dict(lat, row_tile=BIG_ROW_TILE // 2), next_chan=nxt,
                                        **ffn)
        else:
            lat_chan = None
            x_lat = _out_ffn(mixed, x_lat, mod, **dict(lat, row_tile=FFN_ROW_TILE), **ffn)
    return x_lat
```
